```python
import math, functools
import jax, jax.numpy as jnp
from jax import lax
import numpy as np

D_MODEL = 2048
BATCH = 2
SEQ = 4096
DEPTH = 4
DEC_BATCH = 8
DEC_SEQ = 4
PAST_LEN = 16384
PAGE_SIZE = 128

N_MIXERS = 3
N_LAYERS_A = (DEPTH + 2) // 3
N_LAYERS_B = (DEPTH + 1) // 3
N_LAYERS_C = DEPTH // 3

GDN_QK_HEADS = 16
GDN_V_HEADS = 32
GDN_HEAD_K = 128
GDN_HEAD_V = 128
GDN_CONV = 4
GDN_CHUNK = 64
GDN_KEY_DIM = GDN_QK_HEADS * GDN_HEAD_K
GDN_VAL_DIM = GDN_V_HEADS * GDN_HEAD_V
GDN_CONV_DIM = 2 * GDN_KEY_DIM + GDN_VAL_DIM
GDN_IN_DIM = GDN_CONV_DIM + GDN_VAL_DIM + 2 * GDN_V_HEADS

FOX_HEADS = 16
FOX_HEAD_DIM = D_MODEL // FOX_HEADS
FOX_IN_DIM = 3 * D_MODEL + FOX_HEADS

DSA_HEADS = 16
DSA_HEAD_DIM = D_MODEL // DSA_HEADS
DSA_IDX_HEADS = 16
DSA_IDX_DIM = 128
DSA_TOPK_MAX = 256
DSA_IN_DIM = 3 * D_MODEL + DSA_IDX_HEADS * DSA_IDX_DIM + DSA_IDX_DIM + DSA_IDX_HEADS

D_FF = 5632
FFN_CONV = 3

PLE_DIM = 256
Q_BLOCK = 128
LN_EPS = 1e-5
RMS_EPS = 1e-6
L2_EPS = 1e-6
DEEPNORM_ALPHA = (2.0 * DEPTH) ** 0.25
DEEPNORM_BETA = (8.0 * DEPTH) ** -0.25

kernel_name = 'hybrid_gdn_fox_dsa_step'


def layer_norm(x, g, b):
    xf = x.astype(jnp.float32)
    mu = jnp.mean(xf, -1, keepdims=True)
    var = jnp.mean(jnp.square(xf - mu), -1, keepdims=True)
    return ((xf - mu) * lax.rsqrt(var + LN_EPS) * g + b).astype(x.dtype)


def l2norm(x):
    xf = x.astype(jnp.float32)
    return xf * lax.rsqrt(jnp.sum(xf * xf, -1, keepdims=True) + L2_EPS)


def causal_dwconv(x_ext, w):
    width = w.shape[0]
    n = x_ext.shape[1] - width + 1
    return sum(x_ext[:, j:j + n] * w[j] for j in range(width))


def q_block_size(n):
    return Q_BLOCK if n % Q_BLOCK == 0 else n


def to_blocks(a, qb):
    b, n = a.shape[:2]
    return a.reshape((b, n // qb, qb) + a.shape[2:]).swapaxes(0, 1)


def from_blocks(a):
    a = a.swapaxes(0, 1)
    return a.reshape((a.shape[0], a.shape[1] * a.shape[2]) + a.shape[3:])


def paged_rows(pool, page_table):
    g = pool[page_table]
    return g.reshape((g.shape[0], g.shape[1] * g.shape[2]) + g.shape[3:])


def take_rows(a, idx):
    return jax.vmap(lambda ab, ib: ab[ib])(a, idx)


def chunk_gated_delta(q, k, v, g, beta, s0):
    b, n, h, dk = k.shape
    dv = v.shape[-1]
    c = min(GDN_CHUNK, n)
    pad = (-n) % c
    if pad:
        padf = lambda a: jnp.pad(a, [(0, 0), (0, pad)] + [(0, 0)] * (a.ndim - 2))
        q, k, v, g, beta = padf(q), padf(k), padf(v), padf(g), padf(beta)
    nc = (n + pad) // c
    qh, kh, vh = [a.transpose(0, 2, 1, 3).reshape(b, h, nc, c, a.shape[-1]) for a in (q, k, v)]
    gh, bh = [a.transpose(0, 2, 1).reshape(b, h, nc, c) for a in (g, beta)]
    gc = jnp.cumsum(gh, -1)
    diff = gc[..., :, None] - gc[..., None, :]
    lower = jnp.tril(jnp.ones((c, c), bool))
    strict = jnp.tril(jnp.ones((c, c), bool), -1)
    decay = jnp.where(lower, jnp.exp(jnp.where(lower, diff, 0.0)), 0.0)
    kb = kh * bh[..., None]
    a_mat = jnp.where(strict, jnp.einsum('bhnid,bhnjd->bhnij', kb, kh) * decay, 0.0)
    rhs = jnp.concatenate([vh * bh[..., None], kb * jnp.exp(gc)[..., None]], -1)
    sol = lax.linalg.triangular_solve(a_mat + jnp.eye(c, dtype=a_mat.dtype), rhs,
                                      left_side=True, lower=True, unit_diagonal=True)
    u, w = sol[..., :dv], sol[..., dv:]
    qk = jnp.where(lower, jnp.einsum('bhnid,bhnjd->bhnij', qh, kh) * decay, 0.0)
    q_dec = qh * jnp.exp(gc)[..., None]
    k_tail = kh * jnp.exp(gc[..., -1:] - gc)[..., None]
    g_tot = jnp.exp(gc[..., -1])

    def step(s, xs_n):
        u_n, w_n, qk_n, qd_n, kt_n, gt_n = xs_n
        v_new = u_n - jnp.einsum('bhcd,bhde->bhce', w_n, s)
        o = jnp.einsum('bhcd,bhde->bhce', qd_n, s) + jnp.einsum('bhij,bhje->bhie', qk_n, v_new)
        s = s * gt_n[..., None, None] + jnp.einsum('bhcd,bhce->bhde', kt_n, v_new)
        return s, o

    xs_all = tuple(jnp.moveaxis(a, 2, 0) for a in (u, w, qk, q_dec, k_tail, g_tot))
    s_fin, o = lax.scan(step, s0, xs_all)
    o = jnp.moveaxis(o, 0, 2).reshape(b, h, nc * c, dv)[:, :, :n].transpose(0, 2, 1, 3)
    return o, s_fin


def gdn_mixer(x, conv_buf, s0, w_in, conv_w, a_log, dt_bias, norm_w, w_out):
    b, n, _ = x.shape
    proj = x @ w_in
    qkv, z, bt, at = jnp.split(proj, [GDN_CONV_DIM, GDN_CONV_DIM + GDN_VAL_DIM,
                                      GDN_CONV_DIM + GDN_VAL_DIM + GDN_V_HEADS], axis=-1)
    ext = jnp.concatenate([conv_buf.astype(qkv.dtype), qkv], axis=1)
    new_buf = ext[:, -(GDN_CONV - 1):]
    qkv = jax.nn.silu(causal_dwconv(ext, conv_w))
    q, k, v = jnp.split(qkv, [GDN_KEY_DIM, 2 * GDN_KEY_DIM], axis=-1)
    rep = GDN_V_HEADS // GDN_QK_HEADS
    q = jnp.repeat(l2norm(q.reshape(b, n, GDN_QK_HEADS, GDN_HEAD_K)), rep, axis=2) * (GDN_HEAD_K ** -0.5)
    k = jnp.repeat(l2norm(k.reshape(b, n, GDN_QK_HEADS, GDN_HEAD_K)), rep, axis=2)
    v = v.reshape(b, n, GDN_V_HEADS, GDN_HEAD_V).astype(jnp.float32)
    beta = jax.nn.sigmoid(bt.astype(jnp.float32))
    g = -jnp.exp(a_log.astype(jnp.float32)) * jax.nn.softplus(at.astype(jnp.float32) + dt_bias)
    o, s_fin = chunk_gated_delta(q, k, v, g, beta, s0.astype(jnp.float32))
    o = o * lax.rsqrt(jnp.mean(o * o, -1, keepdims=True) + RMS_EPS) * norm_w
    o = o * jax.nn.silu(z.reshape(b, n, GDN_V_HEADS, GDN_HEAD_V).astype(jnp.float32))
    y = o.reshape(b, n, GDN_VAL_DIM).astype(x.dtype) @ w_out
    return y, new_buf, s_fin.astype(s0.dtype)


def fox_qkvf(x, w_in, b_f):
    b, n, _ = x.shape
    q, k, v, f = jnp.split(x @ w_in, [D_MODEL, 2 * D_MODEL, 3 * D_MODEL], axis=-1)
    heads = lambda a: a.reshape(b, n, FOX_HEADS, FOX_HEAD_DIM)
    logf = jax.nn.log_sigmoid(f.astype(jnp.float32) + b_f)
    return heads(q), heads(k), heads(v), logf


def suffix_exclusive(logf):
    return lax.cumsum(logf, axis=1, reverse=True) - logf


def fox_attention(q, k, v, r_q, r_k, q_pos, k_pos):
    b, nq, h, d = q.shape
    qb = q_block_size(nq)
    rk = r_k.transpose(0, 2, 1)

    def block(args):
        q_b, rq_b, pos_b = args
        s = jnp.einsum('bqhd,bkhd->bhqk', q_b, k, preferred_element_type=jnp.float32) * (d ** -0.5)
        s = s + rk[:, :, None, :] - rq_b.transpose(0, 2, 1)[:, :, :, None]
        s = jnp.where(k_pos[None, None, None, :] <= pos_b[None, None, :, None], s, -jnp.inf)
        p = jax.nn.softmax(s, axis=-1)
        return jnp.einsum('bhqk,bkhd->bqhd', p.astype(v.dtype), v)

    o = lax.map(block, (to_blocks(q, qb), to_blocks(r_q, qb), q_pos.reshape(nq // qb, qb)))
    return from_blocks(o)


def dsa_project(x, w_in, ln_g, ln_b):
    b, n, _ = x.shape
    c0 = 3 * D_MODEL
    c1 = c0 + DSA_IDX_HEADS * DSA_IDX_DIM
    c2 = c1 + DSA_IDX_DIM
    q, k, v, qi, ki, wh = jnp.split(x @ w_in, [D_MODEL, 2 * D_MODEL, c0, c1, c2], axis=-1)
    heads = lambda a: a.reshape(b, n, DSA_HEADS, DSA_HEAD_DIM)
    qi = qi.reshape(b, n, DSA_IDX_HEADS, DSA_IDX_DIM)
    ki = layer_norm(ki, ln_g, ln_b)
    wh = wh * (DSA_IDX_HEADS ** -0.5)
    return heads(q), heads(k), heads(v), qi, ki, wh


def dsa_attention(q, qi, wh, q_pos, k_idx, n_keys, gather_kv):
    b, nq, h, d = q.shape
    topk = min(DSA_TOPK_MAX, n_keys // 4)
    qb = q_block_size(nq)
    k_pos = jnp.arange(k_idx.shape[1])

    def block(args):
        q_b, qi_b, w_b, pos_b = args
        rel = jax.nn.relu(jnp.einsum('bqhd,bkd->bqhk', qi_b, k_idx, preferred_element_type=jnp.float32)
                          * (DSA_IDX_DIM ** -0.5))
        score = jnp.einsum('bqhk,bqh->bqk', rel, w_b.astype(jnp.float32))
        score = jnp.where(k_pos[None, None, :] <= pos_b[None, :, None], score, -jnp.inf)
        _, sel = lax.top_k(score, topk)
        valid = sel <= pos_b[None, :, None]
        k_sel, v_sel = gather_kv(sel)
        s = jnp.einsum('bqhd,bqkhd->bqhk', q_b, k_sel, preferred_element_type=jnp.float32) * (d ** -0.5)
        s = jnp.where(valid[:, :, None, :], s, -jnp.inf)
        p = jax.nn.softmax(s, axis=-1)
        return jnp.einsum('bqhk,bqkhd->bqhd', p.astype(v_sel.dtype), v_sel)

    o = lax.map(block, (to_blocks(q, qb), to_blocks(qi, qb), to_blocks(wh, qb), q_pos.reshape(nq // qb, qb)))
    return from_blocks(o)


def gather_paged_or_new(pool_k, pool_v, page_table, k_new, v_new, sel):
    n_past = page_table.shape[1] * PAGE_SIZE
    nb = sel.shape[0]
    in_past = (sel < n_past)[..., None, None]
    sp = jnp.minimum(sel, n_past - 1)
    phys = jnp.take_along_axis(page_table, (sp // PAGE_SIZE).reshape(nb, -1), axis=1).reshape(sel.shape)
    off = sp % PAGE_SIZE
    sn = jnp.clip(sel - n_past, 0, k_new.shape[1] - 1)
    k_sel = jnp.where(in_past, pool_k[phys, off], take_rows(k_new, sn))
    v_sel = jnp.where(in_past, pool_v[phys, off], take_rows(v_new, sn))
    return k_sel, v_sel


def conv_ffn(x, conv_buf, w_up, conv_w, conv_b, w_down):
    h = x @ w_up
    ext = jnp.concatenate([conv_buf.astype(h.dtype), h], axis=1)
    new_buf = ext[:, -(FFN_CONV - 1):]
    h = causal_dwconv(ext, conv_w) + conv_b
    gate, val = jnp.split(h, [D_FF], axis=-1)
    return (jax.nn.silu(gate) * val) @ w_down, new_buf


def per_layer_embed(x, p, w_proj, w_gate):
    return x + (p @ w_proj) * jax.nn.sigmoid(x @ w_gate)


def setup_inputs(seed: int = 0) -> dict:
    key = jax.random.key(seed)
    ks = jax.random.split(key, 64)
    ctr = [0]

    def nk():
        ctr[0] += 1
        return ks[ctr[0] - 1]

    def normal(shape, scale=1.0):
        return jax.random.normal(nk(), shape, jnp.float32) * scale

    def dense(shape, fan_in, scale=1.0):
        return normal(shape, scale * fan_in ** -0.5)

    def gain(shape):
        return 1.0 + normal(shape, 0.02)

    def bias(shape):
        return normal(shape, 0.02)

    n_pages = PAST_LEN // PAGE_SIZE
    n_pool = (5 * DEC_BATCH * n_pages) // 4
    page_table = jax.random.permutation(nk(), n_pool)[: DEC_BATCH * n_pages].reshape(DEC_BATCH, n_pages).astype(jnp.int32)

    x_prompt = normal((BATCH, SEQ, D_MODEL))
    x_sample = normal((DEC_BATCH, DEC_SEQ, D_MODEL))
    cache_fox_k = normal((N_LAYERS_B, n_pool, PAGE_SIZE, FOX_HEADS, FOX_HEAD_DIM))
    cache_fox_v = normal((N_LAYERS_B, n_pool, PAGE_SIZE, FOX_HEADS, FOX_HEAD_DIM))
    cache_fox_logf = jax.nn.log_sigmoid(3.0 + normal((N_LAYERS_B, n_pool, PAGE_SIZE, FOX_HEADS), 0.5))
    cache_dsa_k = normal((N_LAYERS_C, n_pool, PAGE_SIZE, DSA_HEADS, DSA_HEAD_DIM))
    cache_dsa_v = normal((N_LAYERS_C, n_pool, PAGE_SIZE, DSA_HEADS, DSA_HEAD_DIM))
    cache_dsa_kidx = normal((N_LAYERS_C, n_pool, PAGE_SIZE, DSA_IDX_DIM))
    state_gdn = normal((N_LAYERS_A, DEC_BATCH, GDN_V_HEADS, GDN_HEAD_K, GDN_HEAD_V), 0.1)
    state_gdn_conv = normal((N_LAYERS_A, DEC_BATCH, GDN_CONV - 1, GDN_CONV_DIM))
    state_ffn_conv = normal((DEPTH, DEC_BATCH, FFN_CONV - 1, 2 * D_FF))
    p_prompt = normal((DEPTH, BATCH, SEQ, PLE_DIM))
    p_sample = normal((DEPTH, DEC_BATCH, DEC_SEQ, PLE_DIM))

    gdn_w_in = dense((N_LAYERS_A, D_MODEL, GDN_IN_DIM), D_MODEL)
    gdn_conv_w = dense((N_LAYERS_A, GDN_CONV, GDN_CONV_DIM), GDN_CONV)
    gdn_a_log = jnp.log(jax.random.uniform(nk(), (N_LAYERS_A, GDN_V_HEADS), jnp.float32, 1.0, 16.0))
    dt = jnp.exp(jax.random.uniform(nk(), (N_LAYERS_A, GDN_V_HEADS), jnp.float32, math.log(1e-3), math.log(1e-1)))
    gdn_dt_bias = dt + jnp.log(-jnp.expm1(-dt))
    gdn_norm_w = gain((N_LAYERS_A, GDN_HEAD_V))
    gdn_w_out = dense((N_LAYERS_A, GDN_VAL_DIM, D_MODEL), GDN_VAL_DIM, DEEPNORM_BETA)
    fox_w_in = dense((N_LAYERS_B, D_MODEL, FOX_IN_DIM), D_MODEL)
    fox_b_f = jax.random.uniform(nk(), (N_LAYERS_B, FOX_HEADS), jnp.float32, 1.0, 5.0)
    fox_w_out = dense((N_LAYERS_B, D_MODEL, D_MODEL), D_MODEL, DEEPNORM_BETA)
    dsa_w_in = dense((N_LAYERS_C, D_MODEL, DSA_IN_DIM), D_MODEL)
    dsa_idx_ln_g = gain((N_LAYERS_C, DSA_IDX_DIM))
    dsa_idx_ln_b = bias((N_LAYERS_C, DSA_IDX_DIM))
    dsa_w_out = dense((N_LAYERS_C, D_MODEL, D_MODEL), D_MODEL, DEEPNORM_BETA)
    ffn_w_up = dense((DEPTH, D_MODEL, 2 * D_FF), D_MODEL)
    ffn_conv_w = dense((DEPTH, FFN_CONV, 2 * D_FF), FFN_CONV)
    ffn_conv_b = bias((DEPTH, 2 * D_FF))
    ffn_w_down = dense((DEPTH, D_FF, D_MODEL), D_FF, DEEPNORM_BETA)
    ln_mix_g = gain((DEPTH, D_MODEL))
    ln_mix_b = bias((DEPTH, D_MODEL))
    ln_ffn_g = gain((DEPTH, D_MODEL))
    ln_ffn_b = bias((DEPTH, D_MODEL))
    ple_w_proj = dense((DEPTH, PLE_DIM, D_MODEL), PLE_DIM)
    ple_w_gate = dense((DEPTH, D_MODEL, D_MODEL), D_MODEL)
    return {
        'x_prompt': x_prompt, 'x_sample': x_sample,
        'cache_fox_k': cache_fox_k, 'cache_fox_v': cache_fox_v, 'cache_fox_logf': cache_fox_logf,
        'cache_dsa_k': cache_dsa_k, 'cache_dsa_v': cache_dsa_v, 'cache_dsa_kidx': cache_dsa_kidx,
        'state_gdn': state_gdn, 'state_gdn_conv': state_gdn_conv, 'state_ffn_conv': state_ffn_conv,
        'page_table': page_table, 'p_prompt': p_prompt, 'p_sample': p_sample,
        'gdn_w_in': gdn_w_in, 'gdn_conv_w': gdn_conv_w, 'gdn_a_log': gdn_a_log, 'gdn_dt_bias': gdn_dt_bias,
        'gdn_norm_w': gdn_norm_w, 'gdn_w_out': gdn_w_out,
        'fox_w_in': fox_w_in, 'fox_b_f': fox_b_f, 'fox_w_out': fox_w_out,
        'dsa_w_in': dsa_w_in, 'dsa_idx_ln_g': dsa_idx_ln_g, 'dsa_idx_ln_b': dsa_idx_ln_b, 'dsa_w_out': dsa_w_out,
        'ffn_w_up': ffn_w_up, 'ffn_conv_w': ffn_conv_w, 'ffn_conv_b': ffn_conv_b, 'ffn_w_down': ffn_w_down,
        'ln_mix_g': ln_mix_g, 'ln_mix_b': ln_mix_b, 'ln_ffn_g': ln_ffn_g, 'ln_ffn_b': ln_ffn_b,
        'ple_w_proj': ple_w_proj, 'ple_w_gate': ple_w_gate,
    }


def reference(x_prompt, x_sample, cache_fox_k, cache_fox_v, cache_fox_logf,
              cache_dsa_k, cache_dsa_v, cache_dsa_kidx, state_gdn, state_gdn_conv, state_ffn_conv,
              page_table, p_prompt, p_sample,
              gdn_w_in, gdn_conv_w, gdn_a_log, gdn_dt_bias, gdn_norm_w, gdn_w_out,
              fox_w_in, fox_b_f, fox_w_out,
              dsa_w_in, dsa_idx_ln_g, dsa_idx_ln_b, dsa_w_out,
              ffn_w_up, ffn_conv_w, ffn_conv_b, ffn_w_down,
              ln_mix_g, ln_mix_b, ln_ffn_g, ln_ffn_b, ple_w_proj, ple_w_gate):
    b, n, _ = x_prompt.shape
    db, ns, _ = x_sample.shape
    n_past = page_table.shape[1] * PAGE_SIZE
    pos_p = jnp.arange(n)
    pos_sq = n_past + jnp.arange(ns)
    pos_sk = jnp.arange(n_past + ns)
    alpha = DEEPNORM_ALPHA
    xp, xs = x_prompt, x_sample
    gdn_s_p, gdn_c_p, gdn_s_s, gdn_c_s = [], [], [], []
    fox_k_p, fox_v_p, fox_lf_p, fox_k_s, fox_v_s, fox_lf_s = [], [], [], [], [], []
    dsa_k_p, dsa_v_p, dsa_ki_p, dsa_k_s, dsa_v_s, dsa_ki_s = [], [], [], [], [], []
    ffn_c_p, ffn_c_s = [], []
    for i in range(DEPTH):
        kind, j = i % N_MIXERS, i // N_MIXERS
        if kind == 0:
            gw = (gdn_w_in[j], gdn_conv_w[j], gdn_a_log[j], gdn_dt_bias[j], gdn_norm_w[j], gdn_w_out[j])
            yp, c_p, s_p = gdn_mixer(xp, jnp.zeros((b, GDN_CONV - 1, GDN_CONV_DIM), xp.dtype),
                                     jnp.zeros((b, GDN_V_HEADS, GDN_HEAD_K, GDN_HEAD_V), jnp.float32), *gw)
            ys, c_s, s_s = gdn_mixer(xs, state_gdn_conv[j], state_gdn[j], *gw)
            gdn_s_p.append(s_p); gdn_c_p.append(c_p); gdn_s_s.append(s_s); gdn_c_s.append(c_s)
        elif kind == 1:
            q, k, v, lf = fox_qkvf(xp, fox_w_in[j], fox_b_f[j])
            r = suffix_exclusive(lf)
            yp = fox_attention(q, k, v, r, r, pos_p, pos_p).reshape(b, n, D_MODEL) @ fox_w_out[j]
            fox_k_p.append(k); fox_v_p.append(v); fox_lf_p.append(lf.astype(cache_fox_logf.dtype))
            q, k, v, lf = fox_qkvf(xs, fox_w_in[j], fox_b_f[j])
            k_all = jnp.concatenate([paged_rows(cache_fox_k[j], page_table), k], axis=1)
            v_all = jnp.concatenate([paged_rows(cache_fox_v[j], page_table), v], axis=1)
            lf_all = jnp.concatenate([paged_rows(cache_fox_logf[j], page_table).astype(jnp.float32), lf], axis=1)
            r = suffix_exclusive(lf_all)
            ys = fox_attention(q, k_all, v_all, r[:, n_past:], r, pos_sq, pos_sk).reshape(db, ns, D_MODEL) @ fox_w_out[j]
            fox_k_s.append(k); fox_v_s.append(v); fox_lf_s.append(lf.astype(cache_fox_logf.dtype))
        else:
            q, k, v, qi, ki, wh = dsa_project(xp, dsa_w_in[j], dsa_idx_ln_g[j], dsa_idx_ln_b[j])
            gather_p = functools.partial(lambda kk, vv, sel: (take_rows(kk, sel), take_rows(vv, sel)), k, v)
            yp = dsa_attention(q, qi, wh, pos_p, ki, n, gather_p).reshape(b, n, D_MODEL) @ dsa_w_out[j]
            dsa_k_p.append(k); dsa_v_p.append(v); dsa_ki_p.append(ki)
            q, k, v, qi, ki, wh = dsa_project(xs, dsa_w_in[j], dsa_idx_ln_g[j], dsa_idx_ln_b[j])
            ki_all = jnp.concatenate([paged_rows(cache_dsa_kidx[j], page_table), ki], axis=1)
            gather_s = functools.partial(gather_paged_or_new, cache_dsa_k[j], cache_dsa_v[j], page_table, k, v)
            ys = dsa_attention(q, qi, wh, pos_sq, ki_all, n_past + ns, gather_s).reshape(db, ns, D_MODEL) @ dsa_w_out[j]
            dsa_k_s.append(k); dsa_v_s.append(v); dsa_ki_s.append(ki)
        xp = layer_norm(alpha * xp + yp, ln_mix_g[i], ln_mix_b[i])
        xs = layer_norm(alpha * xs + ys, ln_mix_g[i], ln_mix_b[i])
        fw = (ffn_w_up[i], ffn_conv_w[i], ffn_conv_b[i], ffn_w_down[i])
        fp, c_p = conv_ffn(xp, jnp.zeros((b, FFN_CONV - 1, 2 * D_FF), xp.dtype), *fw)
        fs, c_s = conv_ffn(xs, state_ffn_conv[i], *fw)
        ffn_c_p.append(c_p); ffn_c_s.append(c_s)
        xp = per_layer_embed(layer_norm(alpha * xp + fp, ln_ffn_g[i], ln_ffn_b[i]), p_prompt[i], ple_w_proj[i], ple_w_gate[i])
        xs = per_layer_embed(layer_norm(alpha * xs + fs, ln_ffn_g[i], ln_ffn_b[i]), p_sample[i], ple_w_proj[i], ple_w_gate[i])
    return (xp, xs,
            jnp.stack(gdn_s_p), jnp.stack(gdn_c_p), jnp.stack(fox_k_p), jnp.stack(fox_v_p), jnp.stack(fox_lf_p),
            jnp.stack(dsa_k_p), jnp.stack(dsa_v_p), jnp.stack(dsa_ki_p), jnp.stack(ffn_c_p),
            jnp.stack(gdn_s_s), jnp.stack(gdn_c_s), jnp.stack(fox_k_s), jnp.stack(fox_v_s), jnp.stack(fox_lf_s),
            jnp.stack(dsa_k_s), jnp.stack(dsa_v_s), jnp.stack(dsa_ki_s), jnp.stack(ffn_c_s))
```

```python
import functools
import math

import jax
import jax.numpy as jnp
from jax import lax
from jax.experimental import pallas as pl
from jax.experimental.pallas import tpu as pltpu

F32 = jnp.float32
BF16 = jnp.bfloat16

PAGE = 128
GDN_QK_HEADS = 16
GDN_V_HEADS = 32
GDN_HEAD = 128
GDN_CHUNK = 64
ATT_HEADS = 16
HEAD_DIM = 128
IDX_HEADS = 16
IDX_DIM = 128
TOPK_MAX = 256
LN_EPS = 1e-5
RMS_EPS = 1e-6
L2_EPS = 1e-6

V7X_VMEM_LIMIT_BYTES = 56 * 1024 * 1024
NEG_BIG = -1e30
INT_MIN = -(2 ** 31)


def _params(*sem):
    return pltpu.CompilerParams(dimension_semantics=sem, vmem_limit_bytes=V7X_VMEM_LIMIT_BYTES)


def _mm_body(*refs, nk, has_res, alpha):
    if has_res:
        x_ref, w_ref, r_ref, o_ref, *scratch = refs
    else:
        x_ref, w_ref, o_ref, *scratch = refs
        r_ref = None

    def finish(acc):
        if has_res:
            acc = alpha * r_ref[...] + acc
        o_ref[...] = acc.astype(o_ref.dtype)

    part = jnp.dot(x_ref[...], w_ref[...].astype(BF16), preferred_element_type=F32)
    if nk == 1:
        finish(part)
        return
    acc_ref, = scratch
    k = pl.program_id(2)

    @pl.when(k == 0)
    def _():
        acc_ref[...] = part

    @pl.when(k > 0)
    def _():
        acc_ref[...] += part

    @pl.when(k == nk - 1)
    def _():
        finish(acc_ref[...])


def matmul(x, w, layer, *, col0=0, n=None, tm, tn, tk=None, out_dtype=F32, res=None, alpha=None, name="mm"):
    m, kdim = x.shape
    assert x.dtype == BF16 and w.shape[1] == kdim
    n = w.shape[2] - col0 if n is None else n
    tk = kdim if tk is None else tk
    assert m % tm == 0 and n % tn == 0 and kdim % tk == 0 and col0 % tn == 0
    nk = kdim // tk
    cb0 = col0 // tn
    in_specs = [
        pl.BlockSpec((tm, tk), lambda i, j, k: (i, k)),
        pl.BlockSpec((None, tk, tn), lambda i, j, k: (layer, k, cb0 + j)),
    ]
    args = [x, w]
    if res is not None:
        assert res.shape == (m, n)
        in_specs.append(pl.BlockSpec((tm, tn), lambda i, j, k: (i, j)))
        args.append(res)
    return pl.pallas_call(
        functools.partial(_mm_body, nk=nk, has_res=res is not None, alpha=alpha),
        out_shape=jax.ShapeDtypeStruct((m, n), out_dtype),
        grid=(m // tm, n // tn, nk),
        in_specs=in_specs,
        out_specs=pl.BlockSpec((tm, tn), lambda i, j, k: (i, j)),
        scratch_shapes=[pltpu.VMEM((tm, tn), F32)] if nk > 1 else [],
        compiler_params=_params("parallel", "parallel", "arbitrary"),
        name=name,
    )(*args)


def _ln_body(x_ref, g_ref, b_ref, of_ref, ob_ref, *, eps):
    x = x_ref[...]
    mu = jnp.mean(x, -1, keepdims=True)
    xc = x - mu
    var = jnp.mean(xc * xc, -1, keepdims=True)
    y = xc * lax.rsqrt(var + eps) * g_ref[...] + b_ref[...]
    of_ref[...] = y
    ob_ref[...] = y.astype(BF16)


def layer_norm(x, g, b, layer, *, tm, eps=LN_EPS, name="ln"):
    m, n = x.shape
    assert m % tm == 0
    g3 = g.reshape(g.shape[0], 1, n)
    b3 = b.reshape(b.shape[0], 1, n)
    vec = pl.BlockSpec((None, 1, n), lambda i: (layer, 0, 0))
    row = pl.BlockSpec((tm, n), lambda i: (i, 0))
    return pl.pallas_call(
        functools.partial(_ln_body, eps=eps),
        out_shape=(jax.ShapeDtypeStruct((m, n), F32), jax.ShapeDtypeStruct((m, n), BF16)),
        grid=(m // tm,),
        in_specs=[row, vec, vec],
        out_specs=(row, row),
        compiler_params=_params("parallel"),
        name=name,
    )(x, g3, b3)


def _ple_body(xb_ref, pb_ref, wg_ref, wp_ref, xr_ref, of_ref, ob_ref):
    gate = jnp.dot(xb_ref[...], wg_ref[...].astype(BF16), preferred_element_type=F32)
    proj = jnp.dot(pb_ref[...], wp_ref[...].astype(BF16), preferred_element_type=F32)
    y = xr_ref[...] + proj * jax.nn.sigmoid(gate)
    of_ref[...] = y
    ob_ref[...] = y.astype(BF16)


def per_layer_embed(x_f, x_b, p_b, w_proj, w_gate, layer, *, tm, tn, name="ple"):
    m, d = x_f.shape
    pd = p_b.shape[1]
    assert m % tm == 0 and d % tn == 0
    out = pl.BlockSpec((tm, tn), lambda i, j: (i, j))
    return pl.pallas_call(
        _ple_body,
        out_shape=(jax.ShapeDtypeStruct((m, d), F32), jax.ShapeDtypeStruct((m, d), BF16)),
        grid=(m // tm, d // tn),
        in_specs=[
            pl.BlockSpec((tm, d), lambda i, j: (i, 0)),
            pl.BlockSpec((tm, pd), lambda i, j: (i, 0)),
            pl.BlockSpec((None, d, tn), lambda i, j: (layer, 0, j)),
            pl.BlockSpec((None, pd, tn), lambda i, j: (layer, 0, j)),
            out,
        ],
        out_specs=(out, out),
        compiler_params=_params("parallel", "parallel"),
        name=name,
    )(x_b, p_b, w_gate, w_proj, x_f)


HALO_ROWS = 16


def _ffn_up_body(x_ref, xh_ref, wg_ref, wv_ref, cwg_ref, cwv_ref, cbg_ref, cbv_ref, hg_ref, hv_ref,
                 act_ref, tg_ref, tv_ref, hs_ref, *, tiles_per_seq, tm):
    first = (pl.program_id(0) % tiles_per_seq) == 0
    x = x_ref[...]
    xh = xh_ref[...]
    conv = []
    for w_ref, cw_ref, cb_ref, hist_ref, tail_ref in ((wg_ref, cwg_ref, cbg_ref, hg_ref, tg_ref),
                                                     (wv_ref, cwv_ref, cbv_ref, hv_ref, tv_ref)):
        w = w_ref[...].astype(BF16)
        h = jnp.dot(x, w, preferred_element_type=F32)
        halo = jnp.dot(xh, w, preferred_element_type=F32)
        hs_ref[0:8, :] = jnp.where(first, hist_ref[...], halo[HALO_ROWS - 8:, :])
        hs_ref[8:8 + tm, :] = h
        cw = cw_ref[...]
        c = hs_ref[6:6 + tm, :] * cw[0:1, :] + hs_ref[7:7 + tm, :] * cw[1:2, :] + h * cw[2:3, :] + cb_ref[...]
        conv.append(c)
        tail_ref[...] = hs_ref[8 + tm - 2:8 + tm, :]
    act_ref[...] = (jax.nn.silu(conv[0]) * conv[1]).astype(act_ref.dtype)


def ffn_up(x_b, hist, w_up, conv_w, conv_b, layer, *, seq_len, tm, tn, name="ffn_up"):
    m, d = x_b.shape
    f2 = w_up.shape[2]
    f = f2 // 2
    nb = m // seq_len
    assert seq_len % tm == 0 and f % tn == 0 and tm % HALO_ROWS == 0
    tiles_per_seq = seq_len // tm
    nj = f // tn
    hist8 = jnp.concatenate([jnp.zeros((nb, 6, f2), F32), hist.astype(F32)], axis=1)
    cw = conv_w
    cb = conv_b.reshape(conv_b.shape[0], 1, f2)
    halo_blocks = tm // HALO_ROWS

    def wspec(off):
        return pl.BlockSpec((None, d, tn), lambda i, j: (layer, 0, off + j))

    def cwspec(off):
        return pl.BlockSpec((None, cw.shape[1], tn), lambda i, j: (layer, 0, off + j))

    def cbspec(off):
        return pl.BlockSpec((None, 1, tn), lambda i, j: (layer, 0, off + j))

    def hspec(off):
        return pl.BlockSpec((None, 8, tn), lambda i, j: (i // tiles_per_seq, 0, off + j))

    tail = pl.BlockSpec((None, 2, tn), lambda i, j: (i, 0, j))
    act, tail_g, tail_v = pl.pallas_call(
        functools.partial(_ffn_up_body, tiles_per_seq=tiles_per_seq, tm=tm),
        out_shape=(jax.ShapeDtypeStruct((m, f), BF16),
                   jax.ShapeDtypeStruct((m // tm, 2, f), F32), jax.ShapeDtypeStruct((m // tm, 2, f), F32)),
        grid=(m // tm, nj),
        in_specs=[
            pl.BlockSpec((tm, d), lambda i, j: (i, 0)),
            pl.BlockSpec((HALO_ROWS, d), lambda i, j: (jnp.maximum(i * halo_blocks - 1, 0), 0)),
            wspec(0), wspec(nj), cwspec(0), cwspec(nj), cbspec(0), cbspec(nj), hspec(0), hspec(nj),
        ],
        out_specs=(pl.BlockSpec((tm, tn), lambda i, j: (i, j)), tail, tail),
        scratch_shapes=[pltpu.VMEM((tm + 8, tn), F32)],
        compiler_params=_params("arbitrary", "arbitrary"),
        name=name,
    )(x_b, x_b, w_up, w_up, cw, cw, cb, cb, hist8, hist8)
    last = slice(tiles_per_seq - 1, None, tiles_per_seq)
    return act, jnp.concatenate([tail_g[last], tail_v[last]], axis=-1)


def _flash_body(*refs, tq, tk, scale, has_bias, has_mask):
    refs = list(refs)
    q_ref, k_ref, v_ref = refs[:3]
    rest = refs[3:]
    if has_bias:
        rq_ref, rk_ref = rest[:2]
        rest = rest[2:]
    if has_mask:
        mask_ref = rest[0]
        rest = rest[1:]
    o_ref, m_ref, l_ref, acc_ref = rest
    qi = pl.program_id(2)
    q = q_ref[...]
    m_ref[...] = jnp.full(m_ref.shape, NEG_BIG, F32)
    l_ref[...] = jnp.zeros(l_ref.shape, F32)
    acc_ref[...] = jnp.zeros(acc_ref.shape, F32)
    q_pos = qi * tq + lax.broadcasted_iota(jnp.int32, (tq, tk), 0)
    k_iota = lax.broadcasted_iota(jnp.int32, (tq, tk), 1)

    def chunk(c, carry):
        off = pl.multiple_of(c * tk, tk)
        kc = k_ref[pl.ds(off, tk), :].astype(BF16)
        vc = v_ref[pl.ds(off, tk), :].astype(BF16)
        s = lax.dot_general(q, kc, (((1,), (1,)), ((), ())), preferred_element_type=F32) * scale
        if has_bias:
            s = s + rk_ref[c] - rq_ref[...]
        if has_mask:
            ok = mask_ref[c].astype(jnp.int32) != 0
        else:
            ok = (off + k_iota) <= q_pos
        s = jnp.where(ok, s, NEG_BIG)
        m_old = m_ref[...]
        m_new = jnp.maximum(m_old, jnp.max(s, -1, keepdims=True))
        p = jnp.exp(s - m_new)
        a = jnp.exp(m_old - m_new)
        l_ref[...] = a * l_ref[...] + jnp.sum(p, -1, keepdims=True)
        acc_ref[...] = a * acc_ref[...] + jnp.dot(p.astype(BF16), vc, preferred_element_type=F32)
        m_ref[...] = m_new
        return carry

    n_chunks = ((qi + 1) * tq + tk - 1) // tk
    lax.fori_loop(0, n_chunks, chunk, 0)
    o_ref[...] = (acc_ref[...] / l_ref[...]).astype(o_ref.dtype)


def flash_attention(q, k, v, *, r=None, mask=None, tq, tk, name="flash"):
    b, n, hd = q.shape
    h = hd // HEAD_DIM
    assert n % tq == 0 and n % tk == 0
    nkc = n // tk
    in_specs = [
        pl.BlockSpec((None, tq, HEAD_DIM), lambda bi, hi, qi: (bi, qi, hi)),
        pl.BlockSpec((None, n, HEAD_DIM), lambda bi, hi, qi: (bi, 0, hi)),
        pl.BlockSpec((None, n, HEAD_DIM), lambda bi, hi, qi: (bi, 0, hi)),
    ]
    args = [q, k, v]
    if r is not None:
        rt = r.transpose(0, 2, 1)
        in_specs.append(pl.BlockSpec((None, None, tq, 1), lambda bi, hi, qi: (bi, hi, qi, 0)))
        in_specs.append(pl.BlockSpec((None, None, nkc, 1, tk), lambda bi, hi, qi: (bi, hi, 0, 0, 0)))
        args += [rt.reshape(b, h, n, 1), rt.reshape(b, h, nkc, 1, tk)]
    if mask is not None:
        in_specs.append(pl.BlockSpec((None, nkc, tq, tk), lambda bi, hi, qi: (bi, 0, qi, 0)))
        args.append(mask)
    return pl.pallas_call(
        functools.partial(_flash_body, tq=tq, tk=tk, scale=HEAD_DIM ** -0.5,
                          has_bias=r is not None, has_mask=mask is not None),
        out_shape=jax.ShapeDtypeStruct((b, n, hd), BF16),
        grid=(b, h, n // tq),
        in_specs=in_specs,
        out_specs=pl.BlockSpec((None, tq, HEAD_DIM), lambda bi, hi, qi: (bi, qi, hi)),
        scratch_shapes=[pltpu.VMEM((tq, 1), F32), pltpu.VMEM((tq, 1), F32), pltpu.VMEM((tq, HEAD_DIM), F32)],
        compiler_params=_params("parallel", "parallel", "arbitrary"),
        name=name,
    )(*args)


def _dsa_select_body(q_ref, kidx_ref, wh_ref, mask_ref, keys_ref, *, tq, tk, topk, n_keys):
    qi = pl.program_id(1)
    n_valid = ((qi + 1) * tq + tk - 1) // tk
    whs = wh_ref[...] * (IDX_DIM ** -0.5)
    q_pos = qi * tq + lax.broadcasted_iota(jnp.int32, (tq, tk), 0)
    k_iota = lax.broadcasted_iota(jnp.int32, (tq, tk), 1)
    n_heads = wh_ref.shape[-1]
    lanes = 128

    def score_chunk(c, carry):
        off = pl.multiple_of(c * tk, tk)
        kc = kidx_ref[pl.ds(off, tk), :].astype(BF16)
        acc = jnp.zeros((tq, tk), F32)
        for h in range(n_heads):
            d = lax.dot_general(q_ref[:, h * IDX_DIM:(h + 1) * IDX_DIM], kc, (((1,), (1,)), ((), ())),
                                preferred_element_type=F32)
            acc = acc + jnp.maximum(d, 0.0) * whs[:, h:h + 1]
        bits = pltpu.bitcast(acc, jnp.int32)
        key = bits ^ ((bits >> 31) & 0x7FFFFFFF)
        keys_ref[c] = jnp.where((off + k_iota) <= q_pos, key, INT_MIN)
        return carry

    lax.fori_loop(0, n_valid, score_chunk, 0)

    def count(pred):
        def body(c, part):
            ind = jnp.where(pred(keys_ref[c], c * tk), 1.0, 0.0)
            for s in range(tk // lanes):
                part = part + ind[:, s * lanes:(s + 1) * lanes]
            return part
        part = lax.fori_loop(0, n_valid, body, jnp.zeros((tq, lanes), F32))
        return jnp.sum(part, -1, keepdims=True)

    kf = float(topk)

    def value_bit(it, t):
        cand = t + lax.shift_left(jnp.int32(1), 31 - it)
        cnt = count(lambda kk, off: kk >= cand)
        return jnp.where(cnt >= kf, cand, t)

    thr = lax.fori_loop(0, 32, value_bit, jnp.full((tq, 1), INT_MIN, jnp.int32))
    cnt_ge = count(lambda kk, off: kk >= thr)
    cnt_gt = count(lambda kk, off: kk > thr)
    need = kf - cnt_gt
    idx_bits = int(n_keys).bit_length()

    def tie_search(_):
        def index_bit(it, p):
            cand = p + lax.shift_left(jnp.int32(1), idx_bits - 1 - it)
            cnt = count(lambda kk, off: (kk == thr) & ((off + k_iota) < cand))
            return jnp.where(cnt < need, cand, p)
        return lax.fori_loop(0, idx_bits, index_bit, jnp.zeros((tq, 1), jnp.int32))

    excess = jnp.max(cnt_ge - kf) > 0.0
    last_tie = lax.cond(excess, tie_search, lambda _: jnp.full((tq, 1), n_keys, jnp.int32), 0)

    mask_ref[...] = jnp.zeros(mask_ref.shape, mask_ref.dtype)

    def emit(c, carry):
        off = c * tk
        kk = keys_ref[c]
        k_pos = off + k_iota
        sel = (kk > thr) | ((kk == thr) & (k_pos <= last_tie))
        sel = sel & (k_pos <= q_pos)
        mask_ref[c] = jnp.where(sel, 1, 0).astype(mask_ref.dtype)
        return carry

    lax.fori_loop(0, n_valid, emit, 0)


def dsa_select(qidx, kidx, wh, *, tq, tk, topk, name="dsa_select"):
    b, n, _ = qidx.shape
    assert n % tq == 0 and n % tk == 0
    nkc = n // tk
    return pl.pallas_call(
        functools.partial(_dsa_select_body, tq=tq, tk=tk, topk=topk, n_keys=n),
        out_shape=jax.ShapeDtypeStruct((b, nkc, n, tk), jnp.int8),
        grid=(b, n // tq),
        in_specs=[
            pl.BlockSpec((None, tq, qidx.shape[2]), lambda bi, qi: (bi, qi, 0)),
            pl.BlockSpec((None, n, kidx.shape[2]), lambda bi, qi: (bi, 0, 0)),
            pl.BlockSpec((None, tq, wh.shape[2]), lambda bi, qi: (bi, qi, 0)),
        ],
        out_specs=pl.BlockSpec((None, nkc, tq, tk), lambda bi, qi: (bi, 0, qi, 0)),
        scratch_shapes=[pltpu.VMEM((nkc, tq, tk), jnp.int32)],
        compiler_params=_params("parallel", "arbitrary"),
        name=name,
    )(qidx, kidx, wh)


def _l2norm(x):
    return x * lax.rsqrt(jnp.sum(x * x, -1, keepdims=True) + L2_EPS)


def _causal_dwconv(x_ext, w):
    width = w.shape[0]
    n = x_ext.shape[1] - width + 1
    return sum(x_ext[:, j:j + n] * w[j] for j in range(width))


def _chunk_gated_delta(q, k, v, g, beta, s0):
    b, n, h, dk = k.shape
    dv = v.shape[-1]
    c = min(GDN_CHUNK, n)
    pad = (-n) % c
    if pad:
        padf = lambda a: jnp.pad(a, [(0, 0), (0, pad)] + [(0, 0)] * (a.ndim - 2))
        q, k, v, g, beta = padf(q), padf(k), padf(v), padf(g), padf(beta)
    nc = (n + pad) // c
    qh, kh, vh = [a.transpose(0, 2, 1, 3).reshape(b, h, nc, c, a.shape[-1]) for a in (q, k, v)]
    gh, bh = [a.transpose(0, 2, 1).reshape(b, h, nc, c) for a in (g, beta)]
    gc = jnp.cumsum(gh, -1)
    diff = gc[..., :, None] - gc[..., None, :]
    lower = jnp.tril(jnp.ones((c, c), bool))
    strict = jnp.tril(jnp.ones((c, c), bool), -1)
    decay = jnp.where(lower, jnp.exp(jnp.where(lower, diff, 0.0)), 0.0)
    kb = kh * bh[..., None]
    a_mat = jnp.where(strict, jnp.einsum('bhnid,bhnjd->bhnij', kb, kh) * decay, 0.0)
    rhs = jnp.concatenate([vh * bh[..., None], kb * jnp.exp(gc)[..., None]], -1)
    sol = lax.linalg.triangular_solve(a_mat + jnp.eye(c, dtype=a_mat.dtype), rhs,
                                      left_side=True, lower=True, unit_diagonal=True)
    u, w = sol[..., :dv], sol[..., dv:]
    qk = jnp.where(lower, jnp.einsum('bhnid,bhnjd->bhnij', qh, kh) * decay, 0.0)
    q_dec = qh * jnp.exp(gc)[..., None]
    k_tail = kh * jnp.exp(gc[..., -1:] - gc)[..., None]
    g_tot = jnp.exp(gc[..., -1])

    def step(s, xs_n):
        u_n, w_n, qk_n, qd_n, kt_n, gt_n = xs_n
        v_new = u_n - jnp.einsum('bhcd,bhde->bhce', w_n, s)
        o = jnp.einsum('bhcd,bhde->bhce', qd_n, s) + jnp.einsum('bhij,bhje->bhie', qk_n, v_new)
        s = s * gt_n[..., None, None] + jnp.einsum('bhcd,bhce->bhde', kt_n, v_new)
        return s, o

    xs_all = tuple(jnp.moveaxis(a, 2, 0) for a in (u, w, qk, q_dec, k_tail, g_tot))
    s_fin, o = lax.scan(step, s0, xs_all)
    o = jnp.moveaxis(o, 0, 2).reshape(b, h, nc * c, dv)[:, :, :n].transpose(0, 2, 1, 3)
    return o, s_fin


def _gdn_core(qkv, z, bt, at, conv_buf, s0, conv_w, a_log, dt_bias, norm_w):
    b, n, _ = qkv.shape
    key_dim = GDN_QK_HEADS * GDN_HEAD
    ext = jnp.concatenate([conv_buf.astype(qkv.dtype), qkv], axis=1)
    new_buf = ext[:, -(conv_w.shape[0] - 1):]
    qkv = jax.nn.silu(_causal_dwconv(ext, conv_w))
    q, k, v = jnp.split(qkv, [key_dim, 2 * key_dim], axis=-1)
    rep = GDN_V_HEADS // GDN_QK_HEADS
    q = jnp.repeat(_l2norm(q.reshape(b, n, GDN_QK_HEADS, GDN_HEAD)), rep, axis=2) * (GDN_HEAD ** -0.5)
    k = jnp.repeat(_l2norm(k.reshape(b, n, GDN_QK_HEADS, GDN_HEAD)), rep, axis=2)
    v = v.reshape(b, n, GDN_V_HEADS, GDN_HEAD)
    beta = jax.nn.sigmoid(bt)
    g = -jnp.exp(a_log) * jax.nn.softplus(at + dt_bias)
    o, s_fin = _chunk_gated_delta(q, k, v, g, beta, s0)
    o = o * lax.rsqrt(jnp.mean(o * o, -1, keepdims=True) + RMS_EPS) * norm_w
    o = o * jax.nn.silu(z.reshape(b, n, GDN_V_HEADS, GDN_HEAD))
    return o.reshape(b, n, GDN_V_HEADS * GDN_HEAD), new_buf, s_fin


def _suffix_exclusive(logf):
    return lax.cumsum(logf, axis=1, reverse=True) - logf


def _paged_rows(pool, page_table):
    g = pool[page_table]
    return g.reshape((g.shape[0], g.shape[1] * g.shape[2]) + g.shape[3:])


def _take_rows(a, idx):
    return jax.vmap(lambda ab, ib: ab[ib])(a, idx)


def _fox_attention_small(q, k, v, r_q, r_k, q_pos, k_pos):
    d = q.shape[-1]
    rk = r_k.transpose(0, 2, 1)
    s = jnp.einsum('bqhd,bkhd->bhqk', q, k, preferred_element_type=F32) * (d ** -0.5)
    s = s + rk[:, :, None, :] - r_q.transpose(0, 2, 1)[:, :, :, None]
    s = jnp.where(k_pos[None, None, None, :] <= q_pos[None, None, :, None], s, -jnp.inf)
    p = jax.nn.softmax(s, axis=-1)
    return jnp.einsum('bhqk,bkhd->bqhd', p, v)


def _dsa_attention_small(q, qi, wh, q_pos, k_idx, n_keys, gather_kv):
    d = q.shape[-1]
    topk = min(TOPK_MAX, n_keys // 4)
    k_pos = jnp.arange(k_idx.shape[1])
    rel = jax.nn.relu(jnp.einsum('bqhd,bkd->bqhk', qi, k_idx, preferred_element_type=F32) * (IDX_DIM ** -0.5))
    score = jnp.einsum('bqhk,bqh->bqk', rel, wh)
    score = jnp.where(k_pos[None, None, :] <= q_pos[None, :, None], score, -jnp.inf)
    _, sel = lax.top_k(score, topk)
    valid = sel <= q_pos[None, :, None]
    k_sel, v_sel = gather_kv(sel)
    s = jnp.einsum('bqhd,bqkhd->bqhk', q, k_sel, preferred_element_type=F32) * (d ** -0.5)
    s = jnp.where(valid[:, :, None, :], s, -jnp.inf)
    p = jax.nn.softmax(s, axis=-1)
    return jnp.einsum('bqhk,bqkhd->bqhd', p, v_sel)


def _gather_paged_or_new(pool_k, pool_v, page_table, k_new, v_new, sel):
    n_past = page_table.shape[1] * PAGE
    nb = sel.shape[0]
    in_past = (sel < n_past)[..., None, None]
    sp = jnp.minimum(sel, n_past - 1)
    phys = jnp.take_along_axis(page_table, (sp // PAGE).reshape(nb, -1), axis=1).reshape(sel.shape)
    off = sp % PAGE
    sn = jnp.clip(sel - n_past, 0, k_new.shape[1] - 1)
    k_sel = jnp.where(in_past, pool_k[phys, off], _take_rows(k_new, sn))
    v_sel = jnp.where(in_past, pool_v[phys, off], _take_rows(v_new, sn))
    return k_sel, v_sel


def _tile_n(n, col0, cap):
    for t in (1024, 512, 256, 128):
        if t <= cap and n % t == 0 and col0 % t == 0:
            return t
    raise ValueError((n, col0))


def kernel(x_prompt, x_sample, cache_fox_k, cache_fox_v, cache_fox_logf, cache_dsa_k, cache_dsa_v, cache_dsa_kidx, state_gdn, state_gdn_conv, state_ffn_conv, page_table, p_prompt, p_sample, gdn_w_in, gdn_conv_w, gdn_a_log, gdn_dt_bias, gdn_norm_w, gdn_w_out, fox_w_in, fox_b_f, fox_w_out, dsa_w_in, dsa_idx_ln_g, dsa_idx_ln_b, dsa_w_out, ffn_w_up, ffn_conv_w, ffn_conv_b, ffn_w_down, ln_mix_g, ln_mix_b, ln_ffn_g, ln_ffn_b, ple_w_proj, ple_w_gate):
    b, n, d = x_prompt.shape
    db, ns, _ = x_sample.shape
    depth = ffn_w_up.shape[0]
    n_past = page_table.shape[1] * PAGE
    alpha = (2.0 * depth) ** 0.25
    f2 = ffn_w_up.shape[2]
    key_dim = GDN_QK_HEADS * GDN_HEAD
    val_dim = GDN_V_HEADS * GDN_HEAD
    conv_dim = 2 * key_dim + val_dim

    groups = {
        "p": dict(nb=b, n=n, tm=1024, tn_cap=512, xf=x_prompt.reshape(b * n, d)),
        "s": dict(nb=db, n=ns, tm=db * ns, tn_cap=1024, xf=x_sample.reshape(db * ns, d)),
    }
    for gr in groups.values():
        gr["xb"] = gr["xf"].astype(BF16)

    def proj(gr, w, layer, col0, ncols, out_dtype=F32, **kw):
        return matmul(gr["xb"], w, layer, col0=col0, n=ncols, tm=gr["tm"], tn=_tile_n(ncols, col0, gr["tn_cap"]),
                      out_dtype=out_dtype, **kw)

    def out_proj(gr, o_b, w, layer):
        kdim = o_b.shape[1]
        return matmul(o_b, w, layer, tm=gr["tm"], tn=_tile_n(d, 0, gr["tn_cap"]), tk=min(kdim, 2048),
                      res=gr["xf"], alpha=alpha, name="out_proj")

    outs = {k: [] for k in ("gdn_s_p", "gdn_c_p", "gdn_s_s", "gdn_c_s", "fox_k_p", "fox_v_p", "fox_lf_p",
                            "fox_k_s", "fox_v_s", "fox_lf_s", "dsa_k_p", "dsa_v_p", "dsa_ki_p",
                            "dsa_k_s", "dsa_v_s", "dsa_ki_s", "ffn_c_p", "ffn_c_s")}
    pos_sq = n_past + jnp.arange(ns)
    pos_sk = jnp.arange(n_past + ns)

    for i in range(depth):
        kind, j = i % 3, i // 3
        for tag, gr in groups.items():
            nb, nn = gr["nb"], gr["n"]
            if kind == 0:
                qkv = proj(gr, gdn_w_in, j, 0, conv_dim).reshape(nb, nn, conv_dim)
                z = proj(gr, gdn_w_in, j, conv_dim, val_dim).reshape(nb, nn, val_dim)
                ba = proj(gr, gdn_w_in, j, conv_dim + val_dim, 128).reshape(nb, nn, 128)
                bt, at = ba[..., :GDN_V_HEADS], ba[..., GDN_V_HEADS:2 * GDN_V_HEADS]
                if tag == "p":
                    conv_buf = jnp.zeros((nb, gdn_conv_w.shape[1] - 1, conv_dim), F32)
                    s0 = jnp.zeros((nb, GDN_V_HEADS, GDN_HEAD, GDN_HEAD), F32)
                else:
                    conv_buf, s0 = state_gdn_conv[j], state_gdn[j]
                o, c_new, s_new = _gdn_core(qkv, z, bt, at, conv_buf, s0, gdn_conv_w[j], gdn_a_log[j],
                                            gdn_dt_bias[j], gdn_norm_w[j])
                outs["gdn_s_" + tag].append(s_new)
                outs["gdn_c_" + tag].append(c_new)
                r = out_proj(gr, o.reshape(nb * nn, val_dim).astype(BF16), gdn_w_out, j)
            elif kind == 1:
                k = proj(gr, fox_w_in, j, d, d)
                v = proj(gr, fox_w_in, j, 2 * d, d)
                f = proj(gr, fox_w_in, j, 3 * d, 128)[:, :ATT_HEADS].reshape(nb, nn, ATT_HEADS)
                lf = jax.nn.log_sigmoid(f + fox_b_f[j])
                k4, v4 = k.reshape(nb, nn, ATT_HEADS, HEAD_DIM), v.reshape(nb, nn, ATT_HEADS, HEAD_DIM)
                outs["fox_k_" + tag].append(k4)
                outs["fox_v_" + tag].append(v4)
                outs["fox_lf_" + tag].append(lf)
                if tag == "p":
                    q = proj(gr, fox_w_in, j, 0, d, out_dtype=BF16)
                    o = flash_attention(q.reshape(nb, nn, d), k.reshape(nb, nn, d), v.reshape(nb, nn, d),
                                        r=_suffix_exclusive(lf), tq=512, tk=512, name="fox_flash")
                    o = o.reshape(nb * nn, d)
                else:
                    q = proj(gr, fox_w_in, j, 0, d).reshape(nb, nn, ATT_HEADS, HEAD_DIM)
                    k_all = jnp.concatenate([_paged_rows(cache_fox_k[j], page_table), k4], axis=1)
                    v_all = jnp.concatenate([_paged_rows(cache_fox_v[j], page_table), v4], axis=1)
                    lf_all = jnp.concatenate([_paged_rows(cache_fox_logf[j], page_table), lf], axis=1)
                    rr = _suffix_exclusive(lf_all)
                    o = _fox_attention_small(q, k_all, v_all, rr[:, n_past:], rr, pos_sq, pos_sk)
                    o = o.reshape(nb * nn, d).astype(BF16)
                r = out_proj(gr, o, fox_w_out, j)
            else:
                k = proj(gr, dsa_w_in, j, d, d)
                v = proj(gr, dsa_w_in, j, 2 * d, d)
                tail = proj(gr, dsa_w_in, j, 3 * d + IDX_HEADS * IDX_DIM, 256)
                ki, _ = layer_norm(tail[:, :IDX_DIM], dsa_idx_ln_g, dsa_idx_ln_b, j, tm=min(gr["tm"], 512),
                                   name="dsa_ki_ln")
                wh = tail[:, IDX_DIM:IDX_DIM + IDX_HEADS] * (IDX_HEADS ** -0.5)
                k4, v4 = k.reshape(nb, nn, ATT_HEADS, HEAD_DIM), v.reshape(nb, nn, ATT_HEADS, HEAD_DIM)
                ki3 = ki.reshape(nb, nn, IDX_DIM)
                outs["dsa_k_" + tag].append(k4)
                outs["dsa_v_" + tag].append(v4)
                outs["dsa_ki_" + tag].append(ki3)
                if tag == "p":
                    q = proj(gr, dsa_w_in, j, 0, d, out_dtype=BF16)
                    qidx = proj(gr, dsa_w_in, j, 3 * d, IDX_HEADS * IDX_DIM, out_dtype=BF16)
                    mask = dsa_select(qidx.reshape(nb, nn, IDX_HEADS * IDX_DIM), ki3, wh.reshape(nb, nn, IDX_HEADS),
                                      tq=256, tk=512, topk=min(TOPK_MAX, nn // 4))
                    o = flash_attention(q.reshape(nb, nn, d), k.reshape(nb, nn, d), v.reshape(nb, nn, d),
                                        mask=mask, tq=512, tk=512, name="dsa_flash")
                    o = o.reshape(nb * nn, d)
                else:
                    q = proj(gr, dsa_w_in, j, 0, d).reshape(nb, nn, ATT_HEADS, HEAD_DIM)
                    qidx = proj(gr, dsa_w_in, j, 3 * d, IDX_HEADS * IDX_DIM).reshape(nb, nn, IDX_HEADS, IDX_DIM)
                    ki_all = jnp.concatenate([_paged_rows(cache_dsa_kidx[j], page_table), ki3], axis=1)
                    gather_s = functools.partial(_gather_paged_or_new, cache_dsa_k[j], cache_dsa_v[j], page_table, k4, v4)
                    o = _dsa_attention_small(q, qidx, wh.reshape(nb, nn, IDX_HEADS), pos_sq, ki_all, n_past + nn, gather_s)
                    o = o.reshape(nb * nn, d).astype(BF16)
                r = out_proj(gr, o, dsa_w_out, j)

            ln_tm = min(gr["tm"], 512)
            gr["xf"], gr["xb"] = layer_norm(r, ln_mix_g, ln_mix_b, i, tm=ln_tm, name="ln_mix")

            if tag == "p":
                act, c_new = ffn_up(gr["xb"], jnp.zeros((nb, 2, f2), F32), ffn_w_up, ffn_conv_w, ffn_conv_b, i,
                                    seq_len=nn, tm=gr["tm"], tn=512)
            else:
                hcur = proj(gr, ffn_w_up, i, 0, f2).reshape(nb, nn, f2)
                ext = jnp.concatenate([state_ffn_conv[i], hcur], axis=1)
                c_new = ext[:, -(ffn_conv_w.shape[1] - 1):]
                hc = _causal_dwconv(ext, ffn_conv_w[i]) + ffn_conv_b[i]
                gate, val = jnp.split(hc, [f2 // 2], axis=-1)
                act = (jax.nn.silu(gate) * val).reshape(nb * nn, f2 // 2).astype(BF16)
            outs["ffn_c_" + tag].append(c_new)
            r = matmul(act, ffn_w_down, i, tm=gr["tm"], tn=_tile_n(d, 0, gr["tn_cap"]), tk=f2 // 4,
                       res=gr["xf"], alpha=alpha, name="ffn_down")
            x2f, x2b = layer_norm(r, ln_ffn_g, ln_ffn_b, i, tm=ln_tm, name="ln_ffn")
            p_in = (p_prompt if tag == "p" else p_sample)[i].reshape(nb * nn, -1).astype(BF16)
            gr["xf"], gr["xb"] = per_layer_embed(x2f, x2b, p_in, ple_w_proj, ple_w_gate, i, tm=gr["tm"],
                                                 tn=_tile_n(d, 0, gr["tn_cap"]))

    st = lambda key: jnp.stack(outs[key])
    return (groups["p"]["xf"].reshape(b, n, d), groups["s"]["xf"].reshape(db, ns, d),
            st("gdn_s_p"), st("gdn_c_p"), st("fox_k_p"), st("fox_v_p"), st("fox_lf_p"),
            st("dsa_k_p"), st("dsa_v_p"), st("dsa_ki_p"), st("ffn_c_p"),
            st("gdn_s_s"), st("gdn_c_s"), st("fox_k_s"), st("fox_v_s"), st("fox_lf_s"),
            st("dsa_k_s"), st("dsa_v_s"), st("dsa_ki_s"), st("ffn_c_s"))
```

```python
import functools
import math

import jax
import jax.numpy as jnp
from jax import lax
from jax.experimental import pallas as pl
from jax.experimental.pallas import tpu as pltpu

F32 = jnp.float32
BF16 = jnp.bfloat16

PAGE = 128
GDN_QK_HEADS = 16
GDN_V_HEADS = 32
GDN_HEAD = 128
GDN_CHUNK = 64
ATT_HEADS = 16
HEAD_DIM = 128
IDX_HEADS = 16
IDX_DIM = 128
TOPK_MAX = 256
LN_EPS = 1e-5
RMS_EPS = 1e-6
L2_EPS = 1e-6

V7X_VMEM_LIMIT_BYTES = 56 * 1024 * 1024
NEG_BIG = -1e30
INT_MIN = -(2 ** 31)


def _params(*sem):
    return pltpu.CompilerParams(dimension_semantics=sem, vmem_limit_bytes=V7X_VMEM_LIMIT_BYTES)


def _mm_body(*refs, nk, has_res, alpha):
    if has_res:
        x_ref, w_ref, r_ref, o_ref, *scratch = refs
    else:
        x_ref, w_ref, o_ref, *scratch = refs
        r_ref = None

    def finish(acc):
        if has_res:
            acc = alpha * r_ref[...] + acc
        o_ref[...] = acc.astype(o_ref.dtype)

    part = jnp.dot(x_ref[...], w_ref[...].astype(BF16), preferred_element_type=F32)
    if nk == 1:
        finish(part)
        return
    acc_ref, = scratch
    k = pl.program_id(2)

    @pl.when(k == 0)
    def _():
        acc_ref[...] = part

    @pl.when(k > 0)
    def _():
        acc_ref[...] += part

    @pl.when(k == nk - 1)
    def _():
        finish(acc_ref[...])


def matmul(x, w, layer, *, col0=0, n=None, tm, tn, tk=None, out_dtype=F32, res=None, alpha=None, name="mm"):
    m, kdim = x.shape
    assert x.dtype == BF16 and w.shape[1] == kdim
    n = w.shape[2] - col0 if n is None else n
    tk = kdim if tk is None else tk
    assert m % tm == 0 and n % tn == 0 and kdim % tk == 0 and col0 % tn == 0
    nk = kdim // tk
    cb0 = col0 // tn
    in_specs = [
        pl.BlockSpec((tm, tk), lambda i, j, k: (i, k)),
        pl.BlockSpec((None, tk, tn), lambda i, j, k: (layer, k, cb0 + j)),
    ]
    args = [x, w]
    if res is not None:
        assert res.shape == (m, n)
        in_specs.append(pl.BlockSpec((tm, tn), lambda i, j, k: (i, j)))
        args.append(res)
    return pl.pallas_call(
        functools.partial(_mm_body, nk=nk, has_res=res is not None, alpha=alpha),
        out_shape=jax.ShapeDtypeStruct((m, n), out_dtype),
        grid=(m // tm, n // tn, nk),
        in_specs=in_specs,
        out_specs=pl.BlockSpec((tm, tn), lambda i, j, k: (i, j)),
        scratch_shapes=[pltpu.VMEM((tm, tn), F32)] if nk > 1 else [],
        compiler_params=_params("parallel", "parallel", "arbitrary"),
        name=name,
    )(*args)


def _ln_body(x_ref, g_ref, b_ref, of_ref, ob_ref, *, eps):
    x = x_ref[...]
    mu = jnp.mean(x, -1, keepdims=True)
    xc = x - mu
    var = jnp.mean(xc * xc, -1, keepdims=True)
    y = xc * lax.rsqrt(var + eps) * g_ref[...] + b_ref[...]
    of_ref[...] = y
    ob_ref[...] = y.astype(BF16)


def layer_norm(x, g, b, layer, *, tm, eps=LN_EPS, name="ln"):
    m, n = x.shape
    assert m % tm == 0
    g3 = g.reshape(g.shape[0], 1, n)
    b3 = b.reshape(b.shape[0], 1, n)
    vec = pl.BlockSpec((None, 1, n), lambda i: (layer, 0, 0))
    row = pl.BlockSpec((tm, n), lambda i: (i, 0))
    return pl.pallas_call(
        functools.partial(_ln_body, eps=eps),
        out_shape=(jax.ShapeDtypeStruct((m, n), F32), jax.ShapeDtypeStruct((m, n), BF16)),
        grid=(m // tm,),
        in_specs=[row, vec, vec],
        out_specs=(row, row),
        compiler_params=_params("parallel"),
        name=name,
    )(x, g3, b3)


def _ple_body(xb_ref, pb_ref, wg_ref, wp_ref, xr_ref, of_ref, ob_ref):
    gate = jnp.dot(xb_ref[...], wg_ref[...].astype(BF16), preferred_element_type=F32)
    proj = jnp.dot(pb_ref[...], wp_ref[...].astype(BF16), preferred_element_type=F32)
    y = xr_ref[...] + proj * jax.nn.sigmoid(gate)
    of_ref[...] = y
    ob_ref[...] = y.astype(BF16)


def per_layer_embed(x_f, x_b, p_b, w_proj, w_gate, layer, *, tm, tn, name="ple"):
    m, d = x_f.shape
    pd = p_b.shape[1]
    assert m % tm == 0 and d % tn == 0
    out = pl.BlockSpec((tm, tn), lambda i, j: (i, j))
    return pl.pallas_call(
        _ple_body,
        out_shape=(jax.ShapeDtypeStruct((m, d), F32), jax.ShapeDtypeStruct((m, d), BF16)),
        grid=(m // tm, d // tn),
        in_specs=[
            pl.BlockSpec((tm, d), lambda i, j: (i, 0)),
            pl.BlockSpec((tm, pd), lambda i, j: (i, 0)),
            pl.BlockSpec((None, d, tn), lambda i, j: (layer, 0, j)),
            pl.BlockSpec((None, pd, tn), lambda i, j: (layer, 0, j)),
            out,
        ],
        out_specs=(out, out),
        compiler_params=_params("parallel", "parallel"),
        name=name,
    )(x_b, p_b, w_gate, w_proj, x_f)


HALO_ROWS = 16


def _ffn_up_body(x_ref, xh_ref, wg_ref, wv_ref, cwg_ref, cwv_ref, cbg_ref, cbv_ref, hg_ref, hv_ref,
                 act_ref, tg_ref, tv_ref, hs_ref, *, tiles_per_seq, tm):
    first = (pl.program_id(0) % tiles_per_seq) == 0
    x = x_ref[...]
    xh = xh_ref[...]
    conv = []
    for w_ref, cw_ref, cb_ref, hist_ref, tail_ref in ((wg_ref, cwg_ref, cbg_ref, hg_ref, tg_ref),
                                                     (wv_ref, cwv_ref, cbv_ref, hv_ref, tv_ref)):
        w = w_ref[...].astype(BF16)
        h = jnp.dot(x, w, preferred_element_type=F32)
        halo = jnp.dot(xh, w, preferred_element_type=F32)
        hs_ref[0:8, :] = jnp.where(first, hist_ref[...], halo[HALO_ROWS - 8:, :])
        hs_ref[8:8 + tm, :] = h
        cw = cw_ref[...]
        c = hs_ref[6:6 + tm, :] * cw[0:1, :] + hs_ref[7:7 + tm, :] * cw[1:2, :] + h * cw[2:3, :] + cb_ref[...]
        conv.append(c)
        tail_ref[...] = hs_ref[8 + tm - 2:8 + tm, :]
    act_ref[...] = (jax.nn.silu(conv[0]) * conv[1]).astype(act_ref.dtype)


def ffn_up(x_b, hist, w_up, conv_w, conv_b, layer, *, seq_len, tm, tn, name="ffn_up"):
    m, d = x_b.shape
    f2 = w_up.shape[2]
    f = f2 // 2
    nb = m // seq_len
    assert seq_len % tm == 0 and f % tn == 0 and tm % HALO_ROWS == 0
    tiles_per_seq = seq_len // tm
    nj = f // tn
    hist8 = jnp.concatenate([jnp.zeros((nb, 6, f2), F32), hist.astype(F32)], axis=1)
    cw = conv_w
    cb = conv_b.reshape(conv_b.shape[0], 1, f2)
    halo_blocks = tm // HALO_ROWS

    def wspec(off):
        return pl.BlockSpec((None, d, tn), lambda i, j: (layer, 0, off + j))

    def cwspec(off):
        return pl.BlockSpec((None, cw.shape[1], tn), lambda i, j: (layer, 0, off + j))

    def cbspec(off):
        return pl.BlockSpec((None, 1, tn), lambda i, j: (layer, 0, off + j))

    def hspec(off):
        return pl.BlockSpec((None, 8, tn), lambda i, j: (i // tiles_per_seq, 0, off + j))

    tail = pl.BlockSpec((None, 2, tn), lambda i, j: (i, 0, j))
    act, tail_g, tail_v = pl.pallas_call(
        functools.partial(_ffn_up_body, tiles_per_seq=tiles_per_seq, tm=tm),
        out_shape=(jax.ShapeDtypeStruct((m, f), BF16),
                   jax.ShapeDtypeStruct((m // tm, 2, f), F32), jax.ShapeDtypeStruct((m // tm, 2, f), F32)),
        grid=(m // tm, nj),
        in_specs=[
            pl.BlockSpec((tm, d), lambda i, j: (i, 0)),
            pl.BlockSpec((HALO_ROWS, d), lambda i, j: (jnp.maximum(i * halo_blocks - 1, 0), 0)),
            wspec(0), wspec(nj), cwspec(0), cwspec(nj), cbspec(0), cbspec(nj), hspec(0), hspec(nj),
        ],
        out_specs=(pl.BlockSpec((tm, tn), lambda i, j: (i, j)), tail, tail),
        scratch_shapes=[pltpu.VMEM((tm + 8, tn), F32)],
        compiler_params=_params("arbitrary", "arbitrary"),
        name=name,
    )(x_b, x_b, w_up, w_up, cw, cw, cb, cb, hist8, hist8)
    last = slice(tiles_per_seq - 1, None, tiles_per_seq)
    return act, jnp.concatenate([tail_g[last], tail_v[last]], axis=-1)


def _flash_body(*refs, tq, tk, scale, has_bias, has_mask):
    refs = list(refs)
    q_ref, k_ref, v_ref = refs[:3]
    rest = refs[3:]
    if has_bias:
        rq_ref, rk_ref = rest[:2]
        rest = rest[2:]
    if has_mask:
        mask_ref = rest[0]
        rest = rest[1:]
    o_ref, m_ref, l_ref, acc_ref = rest
    qi = pl.program_id(2)
    q = q_ref[...]
    m_ref[...] = jnp.full(m_ref.shape, NEG_BIG, F32)
    l_ref[...] = jnp.zeros(l_ref.shape, F32)
    acc_ref[...] = jnp.zeros(acc_ref.shape, F32)
    q_pos = qi * tq + lax.broadcasted_iota(jnp.int32, (tq, tk), 0)
    k_iota = lax.broadcasted_iota(jnp.int32, (tq, tk), 1)

    def chunk(c, carry):
        off = pl.multiple_of(c * tk, tk)
        kc = k_ref[pl.ds(off, tk), :].astype(BF16)
        vc = v_ref[pl.ds(off, tk), :].astype(BF16)
        s = lax.dot_general(q, kc, (((1,), (1,)), ((), ())), preferred_element_type=F32) * scale
        if has_bias:
            s = s + rk_ref[c] - rq_ref[...]
        if has_mask:
            ok = mask_ref[c].astype(jnp.int32) != 0
        else:
            ok = (off + k_iota) <= q_pos
        s = jnp.where(ok, s, NEG_BIG)
        m_old = m_ref[...]
        m_new = jnp.maximum(m_old, jnp.max(s, -1, keepdims=True))
        p = jnp.exp(s - m_new)
        a = jnp.exp(m_old - m_new)
        l_ref[...] = a * l_ref[...] + jnp.sum(p, -1, keepdims=True)
        acc_ref[...] = a * acc_ref[...] + jnp.dot(p.astype(BF16), vc, preferred_element_type=F32)
        m_ref[...] = m_new
        return carry

    n_chunks = ((qi + 1) * tq + tk - 1) // tk
    lax.fori_loop(0, n_chunks, chunk, 0)
    o_ref[...] = (acc_ref[...] / l_ref[...]).astype(o_ref.dtype)


def flash_attention(q, k, v, *, r=None, mask=None, tq, tk, name="flash"):
    b, n, hd = q.shape
    h = hd // HEAD_DIM
    assert n % tq == 0 and n % tk == 0
    nkc = n // tk
    in_specs = [
        pl.BlockSpec((None, tq, HEAD_DIM), lambda bi, hi, qi: (bi, qi, hi)),
        pl.BlockSpec((None, n, HEAD_DIM), lambda bi, hi, qi: (bi, 0, hi)),
        pl.BlockSpec((None, n, HEAD_DIM), lambda bi, hi, qi: (bi, 0, hi)),
    ]
    args = [q, k, v]
    if r is not None:
        rt = r.transpose(0, 2, 1)
        in_specs.append(pl.BlockSpec((None, None, tq, 1), lambda bi, hi, qi: (bi, hi, qi, 0)))
        in_specs.append(pl.BlockSpec((None, None, nkc, 1, tk), lambda bi, hi, qi: (bi, hi, 0, 0, 0)))
        args += [rt.reshape(b, h, n, 1), rt.reshape(b, h, nkc, 1, tk)]
    if mask is not None:
        in_specs.append(pl.BlockSpec((None, nkc, tq, tk), lambda bi, hi, qi: (bi, 0, qi, 0)))
        args.append(mask)
    return pl.pallas_call(
        functools.partial(_flash_body, tq=tq, tk=tk, scale=HEAD_DIM ** -0.5,
                          has_bias=r is not None, has_mask=mask is not None),
        out_shape=jax.ShapeDtypeStruct((b, n, hd), BF16),
        grid=(b, h, n // tq),
        in_specs=in_specs,
        out_specs=pl.BlockSpec((None, tq, HEAD_DIM), lambda bi, hi, qi: (bi, qi, hi)),
        scratch_shapes=[pltpu.VMEM((tq, 1), F32), pltpu.VMEM((tq, 1), F32), pltpu.VMEM((tq, HEAD_DIM), F32)],
        compiler_params=_params("parallel", "parallel", "arbitrary"),
        name=name,
    )(*args)


def _dsa_select_body(q_ref, kidx_ref, wh_ref, mask_ref, keys_ref, *, tq, tk, topk, n_keys):
    qi = pl.program_id(1)
    n_valid = ((qi + 1) * tq + tk - 1) // tk
    whs = wh_ref[...] * (IDX_DIM ** -0.5)
    q_pos = qi * tq + lax.broadcasted_iota(jnp.int32, (tq, tk), 0)
    k_iota = lax.broadcasted_iota(jnp.int32, (tq, tk), 1)
    n_heads = wh_ref.shape[-1]
    lanes = 128

    def score_chunk(c, carry):
        off = pl.multiple_of(c * tk, tk)
        kc = kidx_ref[pl.ds(off, tk), :].astype(BF16)
        acc = jnp.zeros((tq, tk), F32)
        for h in range(n_heads):
            d = lax.dot_general(q_ref[:, h * IDX_DIM:(h + 1) * IDX_DIM], kc, (((1,), (1,)), ((), ())),
                                preferred_element_type=F32)
            acc = acc + jnp.maximum(d, 0.0) * whs[:, h:h + 1]
        bits = pltpu.bitcast(acc, jnp.int32)
        key = bits ^ ((bits >> 31) & 0x7FFFFFFF)
        keys_ref[c] = jnp.where((off + k_iota) <= q_pos, key, INT_MIN)
        return carry

    lax.fori_loop(0, n_valid, score_chunk, 0)

    def count(pred):
        def body(c, part):
            ind = jnp.where(pred(keys_ref[c], c * tk), 1.0, 0.0)
            for s in range(tk // lanes):
                part = part + ind[:, s * lanes:(s + 1) * lanes]
            return part
        part = lax.fori_loop(0, n_valid, body, jnp.zeros((tq, lanes), F32))
        return jnp.sum(part, -1, keepdims=True)

    kf = float(topk)

    def value_bit(it, t):
        cand = t + lax.shift_left(jnp.int32(1), 31 - it)
        cnt = count(lambda kk, off: kk >= cand)
        return jnp.where(cnt >= kf, cand, t)

    thr = lax.fori_loop(0, 32, value_bit, jnp.full((tq, 1), INT_MIN, jnp.int32))
    cnt_ge = count(lambda kk, off: kk >= thr)
    cnt_gt = count(lambda kk, off: kk > thr)
    need = kf - cnt_gt
    idx_bits = int(n_keys).bit_length()

    def tie_search(_):
        def index_bit(it, p):
            cand = p + lax.shift_left(jnp.int32(1), idx_bits - 1 - it)
            cnt = count(lambda kk, off: (kk == thr) & ((off + k_iota) < cand))
            return jnp.where(cnt < need, cand, p)
        return lax.fori_loop(0, idx_bits, index_bit, jnp.zeros((tq, 1), jnp.int32))

    excess = jnp.max(cnt_ge - kf) > 0.0
    last_tie = lax.cond(excess, tie_search, lambda _: jnp.full((tq, 1), n_keys, jnp.int32), 0)

    mask_ref[...] = jnp.zeros(mask_ref.shape, mask_ref.dtype)

    def emit(c, carry):
        off = c * tk
        kk = keys_ref[c]
        k_pos = off + k_iota
        sel = (kk > thr) | ((kk == thr) & (k_pos <= last_tie))
        sel = sel & (k_pos <= q_pos)
        mask_ref[c] = jnp.where(sel, 1, 0).astype(mask_ref.dtype)
        return carry

    lax.fori_loop(0, n_valid, emit, 0)


def dsa_select(qidx, kidx, wh, *, tq, tk, topk, name="dsa_select"):
    b, n, _ = qidx.shape
    assert n % tq == 0 and n % tk == 0
    nkc = n // tk
    return pl.pallas_call(
        functools.partial(_dsa_select_body, tq=tq, tk=tk, topk=topk, n_keys=n),
        out_shape=jax.ShapeDtypeStruct((b, nkc, n, tk), jnp.int8),
        grid=(b, n // tq),
        in_specs=[
            pl.BlockSpec((None, tq, qidx.shape[2]), lambda bi, qi: (bi, qi, 0)),
            pl.BlockSpec((None, n, kidx.shape[2]), lambda bi, qi: (bi, 0, 0)),
            pl.BlockSpec((None, tq, wh.shape[2]), lambda bi, qi: (bi, qi, 0)),
        ],
        out_specs=pl.BlockSpec((None, nkc, tq, tk), lambda bi, qi: (bi, 0, qi, 0)),
        scratch_shapes=[pltpu.VMEM((nkc, tq, tk), jnp.int32)],
        compiler_params=_params("parallel", "arbitrary"),
        name=name,
    )(qidx, kidx, wh)


def _decode_attn_body(*refs, n_pages, nq, n_heads, scale, has_bias, has_mask):
    refs = list(refs)
    pt_ref, q_ref, k_ref, v_ref, kn_ref, vn_ref = refs[:6]
    rest = refs[6:]
    if has_bias:
        rq_ref, rk_ref = rest[:2]
        rest = rest[2:]
    if has_mask:
        mask_ref = rest[0]
        rest = rest[1:]
    o_ref, m_s, l_s, acc_s = rest
    p = pl.program_id(1)
    rows = nq * n_heads
    page = k_ref.shape[0]

    @pl.when(p == 0)
    def _():
        m_s[...] = jnp.full(m_s.shape, NEG_BIG, F32)
        l_s[...] = jnp.zeros(l_s.shape, F32)
        acc_s[...] = jnp.zeros(acc_s.shape, F32)

    def attend(k_src, v_src, new_tokens):
        kc = k_src[...].astype(BF16)
        vc = v_src[...].astype(BF16)
        s = _dot_nt(q_ref[...], kc) * scale
        if has_bias:
            s = s + jnp.concatenate([rk_ref[...]] * nq, axis=0) - rq_ref[...]
        ok = None
        if has_mask:
            mk = mask_ref[...]
            ok = jnp.concatenate([jnp.broadcast_to(mk[i:i + 1, :], (n_heads, page)) for i in range(nq)], axis=0) > 0.0
        elif new_tokens:
            q_idx = lax.broadcasted_iota(jnp.int32, (rows, page), 0) // n_heads
            t_idx = lax.broadcasted_iota(jnp.int32, (rows, page), 1)
            ok = (t_idx <= q_idx) & (t_idx < nq)
        if ok is not None:
            s = jnp.where(ok, s, NEG_BIG)
        m_old = m_s[...]
        m_new = jnp.maximum(m_old, jnp.max(s, -1, keepdims=True))
        pr = jnp.exp(s - m_new)
        a = jnp.exp(m_old - m_new)
        l_s[...] = a * l_s[...] + jnp.sum(pr, -1, keepdims=True)
        acc_s[...] = a * acc_s[...] + _dot(pr.astype(BF16), vc)
        m_s[...] = m_new

    @pl.when(p < n_pages)
    def _():
        attend(k_ref, v_ref, False)

    @pl.when(p == n_pages)
    def _():
        attend(kn_ref, vn_ref, True)
        hd = acc_s.shape[1] // n_heads
        o = acc_s[...] / l_s[...]
        r_head = lax.broadcasted_iota(jnp.int32, o.shape, 0) % n_heads
        c_head = lax.broadcasted_iota(jnp.int32, o.shape, 1) // hd
        o = jnp.where(r_head == c_head, o, 0.0)
        o_ref[...] = jnp.sum(o.reshape(nq, n_heads, o.shape[1]), axis=1)


def decode_attention(q, pool_k, pool_v, layer, page_table, k_new, v_new, *, r_q=None, r_k=None, mask=None,
                     name="decode_attn"):
    b, nq, h, d = q.shape
    n_pool, page = pool_k.shape[1], pool_k.shape[2]
    n_pages = page_table.shape[1]
    hd = h * d
    rows = nq * h
    eye = jnp.eye(h, dtype=q.dtype)
    q_exp = (q[:, :, :, None, :] * eye[None, None, :, :, None]).reshape(b, rows, hd).astype(BF16)
    pad = lambda a: jnp.pad(a.reshape(b, nq, hd), ((0, 0), (0, page - nq), (0, 0)))
    pk = pool_k.reshape(pool_k.shape[0], n_pool, page, hd)
    pv = pool_v.reshape(pool_v.shape[0], n_pool, page, hd)
    last = n_pages - 1
    pool_spec = pl.BlockSpec((None, None, page, hd), lambda bi, p, pt: (layer, pt[bi, jnp.minimum(p, last)], 0, 0))
    new_spec = pl.BlockSpec((None, page, hd), lambda bi, p, pt: (bi, 0, 0))
    in_specs = [pl.BlockSpec((None, rows, hd), lambda bi, p, pt: (bi, 0, 0)), pool_spec, pool_spec, new_spec, new_spec]
    args = [q_exp, pk, pv, pad(k_new), pad(v_new)]
    if r_q is not None:
        n_tot = r_k.shape[1]
        rk = jnp.pad(r_k, ((0, 0), (0, (n_pages + 1) * page - n_tot), (0, 0)))
        rk = rk.reshape(b, n_pages + 1, page, h).transpose(0, 1, 3, 2)
        in_specs.append(pl.BlockSpec((None, rows, 1), lambda bi, p, pt: (bi, 0, 0)))
        in_specs.append(pl.BlockSpec((None, None, h, page), lambda bi, p, pt: (bi, p, 0, 0)))
        args += [r_q.reshape(b, rows, 1), rk]
    if mask is not None:
        in_specs.append(pl.BlockSpec((None, None, 8, page), lambda bi, p, pt: (bi, p, 0, 0)))
        args.append(mask)
    return pl.pallas_call(
        functools.partial(_decode_attn_body, n_pages=n_pages, nq=nq, n_heads=h, scale=d ** -0.5,
                          has_bias=r_q is not None, has_mask=mask is not None),
        out_shape=jax.ShapeDtypeStruct((b, nq, hd), F32),
        grid_spec=pltpu.PrefetchScalarGridSpec(
            num_scalar_prefetch=1,
            grid=(b, n_pages + 1),
            in_specs=in_specs,
            out_specs=pl.BlockSpec((None, nq, hd), lambda bi, p, pt: (bi, 0, 0)),
            scratch_shapes=[pltpu.VMEM((rows, 1), F32), pltpu.VMEM((rows, 1), F32), pltpu.VMEM((rows, hd), F32)],
        ),
        compiler_params=_params("parallel", "arbitrary"),
        name=name,
    )(page_table, *args)


def _dsa_decode_select_body(pt_ref, q_ref, wh_ref, kidx_ref, kin_ref, mask_ref, keys_s, *, n_pages, nq, n_heads,
                            topk, n_keys):
    p = pl.program_id(1)
    page = kidx_ref.shape[0]
    lane = lax.broadcasted_iota(jnp.int32, (8, page), 1)
    q_row = lax.broadcasted_iota(jnp.int32, (8, page), 0)

    def score_page(src):
        kc = src[...].astype(BF16)
        d = _dot_nt(q_ref[...], kc)
        rel = jnp.maximum(d, 0.0) * (wh_ref[...] * (IDX_DIM ** -0.5))
        sc = jnp.sum(rel.reshape(nq, n_heads, page), axis=1)
        sc = jnp.concatenate([sc, jnp.zeros((8 - nq, page), F32)], axis=0)
        bits = pltpu.bitcast(sc, jnp.int32)
        return bits ^ ((bits >> 31) & 0x7FFFFFFF)

    @pl.when(p < n_pages)
    def _():
        keys_s[p] = score_page(kidx_ref)

    @pl.when(p == n_pages)
    def _():
        key = score_page(kin_ref)
        keys_s[p] = jnp.where((lane <= q_row) & (lane < nq), key, INT_MIN)

        def count(pred):
            def body(c, part):
                return part + jnp.where(pred(keys_s[c], c * page + lane), 1.0, 0.0)
            part = lax.fori_loop(0, n_pages + 1, body, jnp.zeros((8, page), F32))
            return jnp.sum(part, -1, keepdims=True)

        kf = float(topk)

        def value_bit(it, t):
            cand = t + lax.shift_left(jnp.int32(1), 31 - it)
            return jnp.where(count(lambda kk, pos: kk >= cand) >= kf, cand, t)

        thr = lax.fori_loop(0, 32, value_bit, jnp.full((8, 1), INT_MIN, jnp.int32))
        need = kf - count(lambda kk, pos: kk > thr)
        idx_bits = int(n_keys).bit_length()

        def index_bit(it, pos_max):
            cand = pos_max + lax.shift_left(jnp.int32(1), idx_bits - 1 - it)
            cnt = count(lambda kk, pos: (kk == thr) & (pos < cand))
            return jnp.where(cnt < need, cand, pos_max)

        last_tie = lax.fori_loop(0, idx_bits, index_bit, jnp.zeros((8, 1), jnp.int32))

        def emit(c, carry):
            kk = keys_s[c]
            pos = c * page + lane
            sel = (kk > thr) | ((kk == thr) & (pos <= last_tie))
            sel = sel & (kk != INT_MIN)
            mask_ref[c] = jnp.where(sel, 1.0, 0.0)
            return carry

        lax.fori_loop(0, n_pages + 1, emit, 0)


def dsa_decode_select(qidx, wh, pool_kidx, layer, page_table, kidx_new, *, topk, name="dsa_decode_select"):
    b, nq, hi, di = qidx.shape
    page = pool_kidx.shape[2]
    n_pages = page_table.shape[1]
    rows = nq * hi
    last = n_pages - 1
    kin = jnp.pad(kidx_new, ((0, 0), (0, page - nq), (0, 0)))
    return pl.pallas_call(
        functools.partial(_dsa_decode_select_body, n_pages=n_pages, nq=nq, n_heads=hi, topk=topk,
                          n_keys=n_pages * page + nq),
        out_shape=jax.ShapeDtypeStruct((b, n_pages + 1, 8, page), F32),
        grid_spec=pltpu.PrefetchScalarGridSpec(
            num_scalar_prefetch=1,
            grid=(b, n_pages + 1),
            in_specs=[
                pl.BlockSpec((None, rows, di), lambda bi, p, pt: (bi, 0, 0)),
                pl.BlockSpec((None, rows, 1), lambda bi, p, pt: (bi, 0, 0)),
                pl.BlockSpec((None, None, page, di), lambda bi, p, pt: (layer, pt[bi, jnp.minimum(p, last)], 0, 0)),
                pl.BlockSpec((None, page, di), lambda bi, p, pt: (bi, 0, 0)),
            ],
            out_specs=pl.BlockSpec((None, n_pages + 1, 8, page), lambda bi, p, pt: (bi, 0, 0, 0)),
            scratch_shapes=[pltpu.VMEM((n_pages + 1, 8, page), jnp.int32)],
        ),
        compiler_params=_params("parallel", "arbitrary"),
        name=name,
    )(page_table, qidx.reshape(b, rows, di).astype(BF16), wh.reshape(b, rows, 1), pool_kidx, kin)


GDN_HALO = 8


def _dot(a, b, **kw):
    return jnp.dot(a, b, preferred_element_type=F32, **kw)


def _dot_nt(a, b, **kw):
    return lax.dot_general(a, b, (((1,), (1,)), ((), ())), preferred_element_type=F32, **kw)


def _gdn_body(q_ref, k_ref, v_ref, z_ref, ba_ref, cwq_ref, cwk_ref, cwv_ref, gate_ref, nw_ref, h0_ref, s0_ref,
              o_ref, sout_ref,
              xq_s, xk_s, xv_s, s_s, u_s, wq_s, qkkt_s, gt_s, *, blk, chunk, n_vh, hg):
    jg = pl.program_id(1)
    sb = pl.program_id(2)
    n_sb = pl.num_programs(2)
    nch = blk // chunk
    hd = GDN_HEAD
    exact = dict(precision=lax.Precision.HIGHEST)

    @pl.when(sb == 0)
    def _():
        xq_s[0:GDN_HALO, :] = h0_ref[:, 0:hg * hd]
        xk_s[0:GDN_HALO, :] = h0_ref[:, hg * hd:2 * hg * hd]
        xv_s[0:GDN_HALO, :] = h0_ref[:, 2 * hg * hd:]
        s_s[...] = s0_ref[...]

    xq_s[GDN_HALO:, :] = q_ref[...]
    xk_s[GDN_HALO:, :] = k_ref[...]
    xv_s[GDN_HALO:, :] = v_ref[...]

    row = lax.broadcasted_iota(jnp.int32, (chunk, chunk), 0)
    col = lax.broadcasted_iota(jnp.int32, (chunk, chunk), 1)
    lower = row >= col
    strict = row > col
    ltri = jnp.where(lower, 1.0, 0.0).astype(F32)
    lane = lax.broadcasted_iota(jnp.int32, (chunk, 128), 1)
    sel_row = lax.broadcasted_iota(jnp.int32, (8, 128), 0)
    sel_lane = lax.broadcasted_iota(jnp.int32, (8, 128), 1)
    hv0 = 2 * hg * jg
    pick = jnp.where((sel_row < 2 * hg) & (sel_lane == n_vh + hv0 + sel_row), 1.0, 0.0).astype(F32)
    neg_a = -jnp.exp(gate_ref[0:1, :])
    dt_bias = gate_ref[1:2, :]

    def conv_silu(xs_ref, w_ref, r0):
        x = xs_ref[pl.ds(r0, chunk + GDN_HALO), :]
        w = w_ref[...]
        taps = w.shape[0]
        acc = None
        for t in range(taps):
            sh = taps - 1 - t
            xt = x if sh == 0 else pltpu.roll(x, sh, axis=0)
            term = xt[GDN_HALO:, :] * w[t:t + 1, :]
            acc = term if acc is None else acc + term
        return jax.nn.silu(acc)

    def l2n(x):
        return x * lax.rsqrt(jnp.sum(x * x, -1, keepdims=True) + L2_EPS)

    def prep(c, carry):
        r0 = pl.multiple_of(c * chunk, chunk)
        q_all = conv_silu(xq_s, cwq_ref, r0)
        k_all = conv_silu(xk_s, cwk_ref, r0)
        v_all = conv_silu(xv_s, cwv_ref, r0)
        ba = jnp.where(lane < 2 * n_vh, ba_ref[pl.ds(r0, chunk), :], 0.0)
        beta_all = jax.nn.sigmoid(ba)
        g_all = neg_a * jax.nn.softplus(ba + dt_bias)
        gc_all = _dot(ltri, g_all, **exact)
        gc_rows = _dot_nt(pick, gc_all, **exact)
        heads = range(2 * hg)
        qs = [l2n(q_all[:, h * hd:(h + 1) * hd]) * (hd ** -0.5) for h in range(hg)]
        ks = [l2n(k_all[:, h * hd:(h + 1) * hd]) for h in range(hg)]
        k16 = [k.astype(BF16) for k in ks]
        gram = [_dot_nt(k16[h], k16[h]) for h in range(hg)]
        qk_raw = [_dot_nt(qs[h].astype(BF16), k16[h]) for h in range(hg)]
        beta = [jnp.sum(jnp.where(lane == hv0 + e, beta_all, 0.0), -1, keepdims=True) for e in heads]
        gcol = [jnp.sum(jnp.where(lane == n_vh + hv0 + e, gc_all, 0.0), -1, keepdims=True) for e in heads]
        g_last = [g[chunk - 1:chunk, :] for g in gcol]
        decay = [jnp.where(lower, jnp.exp(jnp.where(lower, gcol[e] - gc_rows[e:e + 1, :], 0.0)), 0.0) for e in heads]
        nmat = [jnp.where(strict, -(beta[e] * gram[e // 2] * decay[e]), 0.0) for e in heads]
        cpow = [m.astype(BF16) for m in nmat]
        for _ in range(max(chunk.bit_length() - 2, 0)):
            cnew = [_dot(cp, cp) for cp in cpow]
            cpow = [cn.astype(BF16) for cn in cnew]
            corr = [_dot(nmat[e].astype(BF16), cpow[e]) for e in heads]
            nmat = [nmat[e] + cnew[e] + corr[e] for e in heads]
        eg = [jnp.exp(g) for g in gcol]
        rhs = [jnp.concatenate([v_all[:, e * hd:(e + 1) * hd] * beta[e], (ks[e // 2] * beta[e]) * eg[e]], axis=1)
               for e in heads]
        sol = [rhs[e] + _dot(nmat[e].astype(BF16), rhs[e].astype(BF16)) for e in heads]
        for e in heads:
            u_s[e, pl.ds(r0, chunk), :] = sol[e][:, :hd]
            wq_s[e, c, 0:chunk, :] = sol[e][:, hd:].astype(BF16)
            wq_s[e, c, chunk:2 * chunk, :] = (qs[e // 2] * eg[e]).astype(BF16)
            qkkt_s[e, c, 0:chunk, :] = jnp.where(lower, qk_raw[e // 2] * decay[e], 0.0).astype(BF16)
            k_tail = ks[e // 2] * jnp.exp(g_last[e] - gcol[e])
            qkkt_s[e, c, chunk:, :] = k_tail.T.astype(BF16)
            gt_s[e, c] = jnp.broadcast_to(jnp.exp(g_last[e]), (8, 128))
        return carry

    lax.fori_loop(0, nch, prep, 0)

    nw = nw_ref[...]

    def recur(c, carry):
        r0 = pl.multiple_of(c * chunk, chunk)
        heads = range(2 * hg)
        s = [s_s[e] for e in heads]
        ws = [_dot(wq_s[e, c], s[e].astype(BF16)) for e in heads]
        v_new = [u_s[e, pl.ds(r0, chunk), :] - ws[e][0:chunk] for e in heads]
        mix = [_dot(qkkt_s[e, c], v_new[e].astype(BF16)) for e in heads]
        for e in heads:
            s_s[e] = s[e] * gt_s[e, c][0:1, 0:1] + mix[e][chunk:]
            o = ws[e][chunk:] + mix[e][0:chunk]
            o = o * lax.rsqrt(jnp.mean(o * o, -1, keepdims=True) + RMS_EPS) * nw
            o = o * jax.nn.silu(z_ref[pl.ds(r0, chunk), e * hd:(e + 1) * hd])
            o_ref[pl.ds(r0, chunk), e * hd:(e + 1) * hd] = o.astype(o_ref.dtype)
        return carry

    lax.fori_loop(0, nch, recur, 0)

    xq_s[0:GDN_HALO, :] = xq_s[blk:blk + GDN_HALO, :]
    xk_s[0:GDN_HALO, :] = xk_s[blk:blk + GDN_HALO, :]
    xv_s[0:GDN_HALO, :] = xv_s[blk:blk + GDN_HALO, :]

    @pl.when(sb == n_sb - 1)
    def _():
        sout_ref[...] = s_s[...]


def gdn_mixer(qkv, z, ba, conv_hist, s0, conv_w, a_log, dt_bias, norm_w, layer, *, blk, hg, chunk=GDN_CHUNK,
              name="gdn"):
    b, n, conv_dim = qkv.shape
    hd = GDN_HEAD
    n_qk, n_vh = GDN_QK_HEADS, GDN_V_HEADS
    assert n_vh == 2 * n_qk and n % blk == 0 and blk % chunk == 0 and 2 * n_vh <= ba.shape[2]
    assert n_qk % hg == 0 and 2 * hg <= 8
    ng = n_qk // hg
    taps = conv_w.shape[1]
    hist = jnp.concatenate([jnp.zeros((b, GDN_HALO - (taps - 1), conv_dim), F32), conv_hist.astype(F32)], axis=1)
    hq = hist[:, :, :n_qk * hd].reshape(b, GDN_HALO, ng, hg * hd)
    hk = hist[:, :, n_qk * hd:2 * n_qk * hd].reshape(b, GDN_HALO, ng, hg * hd)
    hv = hist[:, :, 2 * n_qk * hd:].reshape(b, GDN_HALO, ng, 2 * hg * hd)
    h0 = jnp.concatenate([hq, hk, hv], axis=-1).transpose(0, 2, 1, 3)
    gate = jnp.zeros((8, ba.shape[2]), F32)
    gate = gate.at[0, n_vh:2 * n_vh].set(a_log[layer]).at[1, n_vh:2 * n_vh].set(dt_bias[layer])
    nw = norm_w.reshape(norm_w.shape[0], 1, hd)
    nch = blk // chunk
    wq, wv = hg * hd, 2 * hg * hd
    o, s_out = pl.pallas_call(
        functools.partial(_gdn_body, blk=blk, chunk=chunk, n_vh=n_vh, hg=hg),
        out_shape=(jax.ShapeDtypeStruct((b, n, n_vh * hd), BF16), jax.ShapeDtypeStruct(s0.shape, F32)),
        grid=(b, ng, n // blk),
        in_specs=[
            pl.BlockSpec((None, blk, wq), lambda bi, j, sb: (bi, sb, j)),
            pl.BlockSpec((None, blk, wq), lambda bi, j, sb: (bi, sb, ng + j)),
            pl.BlockSpec((None, blk, wv), lambda bi, j, sb: (bi, sb, ng + j)),
            pl.BlockSpec((None, blk, wv), lambda bi, j, sb: (bi, sb, j)),
            pl.BlockSpec((None, blk, ba.shape[2]), lambda bi, j, sb: (bi, sb, 0)),
            pl.BlockSpec((None, taps, wq), lambda bi, j, sb: (layer, 0, j)),
            pl.BlockSpec((None, taps, wq), lambda bi, j, sb: (layer, 0, ng + j)),
            pl.BlockSpec((None, taps, wv), lambda bi, j, sb: (layer, 0, ng + j)),
            pl.BlockSpec((8, ba.shape[2]), lambda bi, j, sb: (0, 0)),
            pl.BlockSpec((None, 1, hd), lambda bi, j, sb: (layer, 0, 0)),
            pl.BlockSpec((None, None, GDN_HALO, 4 * wq), lambda bi, j, sb: (bi, j, 0, 0)),
            pl.BlockSpec((None, 2 * hg, hd, hd), lambda bi, j, sb: (bi, j, 0, 0)),
        ],
        out_specs=(
            pl.BlockSpec((None, blk, wv), lambda bi, j, sb: (bi, sb, j)),
            pl.BlockSpec((None, 2 * hg, hd, hd), lambda bi, j, sb: (bi, j, 0, 0)),
        ),
        scratch_shapes=[
            pltpu.VMEM((blk + GDN_HALO, wq), F32), pltpu.VMEM((blk + GDN_HALO, wq), F32),
            pltpu.VMEM((blk + GDN_HALO, wv), F32),
            pltpu.VMEM((2 * hg, hd, hd), F32),
            pltpu.VMEM((2 * hg, blk, hd), F32),
            pltpu.VMEM((2 * hg, nch, 2 * chunk, hd), BF16),
            pltpu.VMEM((2 * hg, nch, chunk + hd, chunk), BF16),
            pltpu.VMEM((2 * hg, nch, 8, 128), F32),
        ],
        compiler_params=_params("parallel", "parallel", "arbitrary"),
        name=name,
    )(qkv, qkv, qkv, z, ba, conv_w, conv_w, conv_w, gate, nw, h0, s0)
    return o, s_out


def _l2norm(x):
    return x * lax.rsqrt(jnp.sum(x * x, -1, keepdims=True) + L2_EPS)


def _causal_dwconv(x_ext, w):
    width = w.shape[0]
    n = x_ext.shape[1] - width + 1
    return sum(x_ext[:, j:j + n] * w[j] for j in range(width))


def _chunk_gated_delta(q, k, v, g, beta, s0):
    b, n, h, dk = k.shape
    dv = v.shape[-1]
    c = min(GDN_CHUNK, n)
    pad = (-n) % c
    if pad:
        padf = lambda a: jnp.pad(a, [(0, 0), (0, pad)] + [(0, 0)] * (a.ndim - 2))
        q, k, v, g, beta = padf(q), padf(k), padf(v), padf(g), padf(beta)
    nc = (n + pad) // c
    qh, kh, vh = [a.transpose(0, 2, 1, 3).reshape(b, h, nc, c, a.shape[-1]) for a in (q, k, v)]
    gh, bh = [a.transpose(0, 2, 1).reshape(b, h, nc, c) for a in (g, beta)]
    gc = jnp.cumsum(gh, -1)
    diff = gc[..., :, None] - gc[..., None, :]
    lower = jnp.tril(jnp.ones((c, c), bool))
    strict = jnp.tril(jnp.ones((c, c), bool), -1)
    decay = jnp.where(lower, jnp.exp(jnp.where(lower, diff, 0.0)), 0.0)
    kb = kh * bh[..., None]
    a_mat = jnp.where(strict, jnp.einsum('bhnid,bhnjd->bhnij', kb, kh) * decay, 0.0)
    rhs = jnp.concatenate([vh * bh[..., None], kb * jnp.exp(gc)[..., None]], -1)
    sol = lax.linalg.triangular_solve(a_mat + jnp.eye(c, dtype=a_mat.dtype), rhs,
                                      left_side=True, lower=True, unit_diagonal=True)
    u, w = sol[..., :dv], sol[..., dv:]
    qk = jnp.where(lower, jnp.einsum('bhnid,bhnjd->bhnij', qh, kh) * decay, 0.0)
    q_dec = qh * jnp.exp(gc)[..., None]
    k_tail = kh * jnp.exp(gc[..., -1:] - gc)[..., None]
    g_tot = jnp.exp(gc[..., -1])

    def step(s, xs_n):
        u_n, w_n, qk_n, qd_n, kt_n, gt_n = xs_n
        v_new = u_n - jnp.einsum('bhcd,bhde->bhce', w_n, s)
        o = jnp.einsum('bhcd,bhde->bhce', qd_n, s) + jnp.einsum('bhij,bhje->bhie', qk_n, v_new)
        s = s * gt_n[..., None, None] + jnp.einsum('bhcd,bhce->bhde', kt_n, v_new)
        return s, o

    xs_all = tuple(jnp.moveaxis(a, 2, 0) for a in (u, w, qk, q_dec, k_tail, g_tot))
    s_fin, o = lax.scan(step, s0, xs_all)
    o = jnp.moveaxis(o, 0, 2).reshape(b, h, nc * c, dv)[:, :, :n].transpose(0, 2, 1, 3)
    return o, s_fin


def _gdn_core(qkv, z, bt, at, conv_buf, s0, conv_w, a_log, dt_bias, norm_w):
    b, n, _ = qkv.shape
    key_dim = GDN_QK_HEADS * GDN_HEAD
    ext = jnp.concatenate([conv_buf.astype(qkv.dtype), qkv], axis=1)
    new_buf = ext[:, -(conv_w.shape[0] - 1):]
    qkv = jax.nn.silu(_causal_dwconv(ext, conv_w))
    q, k, v = jnp.split(qkv, [key_dim, 2 * key_dim], axis=-1)
    rep = GDN_V_HEADS // GDN_QK_HEADS
    q = jnp.repeat(_l2norm(q.reshape(b, n, GDN_QK_HEADS, GDN_HEAD)), rep, axis=2) * (GDN_HEAD ** -0.5)
    k = jnp.repeat(_l2norm(k.reshape(b, n, GDN_QK_HEADS, GDN_HEAD)), rep, axis=2)
    v = v.reshape(b, n, GDN_V_HEADS, GDN_HEAD)
    beta = jax.nn.sigmoid(bt)
    g = -jnp.exp(a_log) * jax.nn.softplus(at + dt_bias)
    o, s_fin = _chunk_gated_delta(q, k, v, g, beta, s0)
    o = o * lax.rsqrt(jnp.mean(o * o, -1, keepdims=True) + RMS_EPS) * norm_w
    o = o * jax.nn.silu(z.reshape(b, n, GDN_V_HEADS, GDN_HEAD))
    return o.reshape(b, n, GDN_V_HEADS * GDN_HEAD), new_buf, s_fin


def _suffix_exclusive(logf):
    return lax.cumsum(logf, axis=1, reverse=True) - logf


def _paged_rows(pool, page_table):
    g = pool[page_table]
    return g.reshape((g.shape[0], g.shape[1] * g.shape[2]) + g.shape[3:])


def _take_rows(a, idx):
    return jax.vmap(lambda ab, ib: ab[ib])(a, idx)


def _fox_attention_small(q, k, v, r_q, r_k, q_pos, k_pos):
    d = q.shape[-1]
    rk = r_k.transpose(0, 2, 1)
    s = jnp.einsum('bqhd,bkhd->bhqk', q, k, preferred_element_type=F32) * (d ** -0.5)
    s = s + rk[:, :, None, :] - r_q.transpose(0, 2, 1)[:, :, :, None]
    s = jnp.where(k_pos[None, None, None, :] <= q_pos[None, None, :, None], s, -jnp.inf)
    p = jax.nn.softmax(s, axis=-1)
    return jnp.einsum('bhqk,bkhd->bqhd', p, v)


def _dsa_attention_small(q, qi, wh, q_pos, k_idx, n_keys, gather_kv):
    d = q.shape[-1]
    topk = min(TOPK_MAX, n_keys // 4)
    k_pos = jnp.arange(k_idx.shape[1])
    rel = jax.nn.relu(jnp.einsum('bqhd,bkd->bqhk', qi, k_idx, preferred_element_type=F32) * (IDX_DIM ** -0.5))
    score = jnp.einsum('bqhk,bqh->bqk', rel, wh)
    score = jnp.where(k_pos[None, None, :] <= q_pos[None, :, None], score, -jnp.inf)
    _, sel = lax.top_k(score, topk)
    valid = sel <= q_pos[None, :, None]
    k_sel, v_sel = gather_kv(sel)
    s = jnp.einsum('bqhd,bqkhd->bqhk', q, k_sel, preferred_element_type=F32) * (d ** -0.5)
    s = jnp.where(valid[:, :, None, :], s, -jnp.inf)
    p = jax.nn.softmax(s, axis=-1)
    return jnp.einsum('bqhk,bqkhd->bqhd', p, v_sel)


def _gather_paged_or_new(pool_k, pool_v, page_table, k_new, v_new, sel):
    n_past = page_table.shape[1] * PAGE
    nb = sel.shape[0]
    in_past = (sel < n_past)[..., None, None]
    sp = jnp.minimum(sel, n_past - 1)
    phys = jnp.take_along_axis(page_table, (sp // PAGE).reshape(nb, -1), axis=1).reshape(sel.shape)
    off = sp % PAGE
    sn = jnp.clip(sel - n_past, 0, k_new.shape[1] - 1)
    k_sel = jnp.where(in_past, pool_k[phys, off], _take_rows(k_new, sn))
    v_sel = jnp.where(in_past, pool_v[phys, off], _take_rows(v_new, sn))
    return k_sel, v_sel


def _tile_n(n, col0, cap):
    for t in (1024, 512, 256, 128):
        if t <= cap and n % t == 0 and col0 % t == 0:
            return t
    raise ValueError((n, col0))


def kernel(x_prompt, x_sample, cache_fox_k, cache_fox_v, cache_fox_logf, cache_dsa_k, cache_dsa_v, cache_dsa_kidx, state_gdn, state_gdn_conv, state_ffn_conv, page_table, p_prompt, p_sample, gdn_w_in, gdn_conv_w, gdn_a_log, gdn_dt_bias, gdn_norm_w, gdn_w_out, fox_w_in, fox_b_f, fox_w_out, dsa_w_in, dsa_idx_ln_g, dsa_idx_ln_b, dsa_w_out, ffn_w_up, ffn_conv_w, ffn_conv_b, ffn_w_down, ln_mix_g, ln_mix_b, ln_ffn_g, ln_ffn_b, ple_w_proj, ple_w_gate):
    b, n, d = x_prompt.shape
    db, ns, _ = x_sample.shape
    depth = ffn_w_up.shape[0]
    n_past = page_table.shape[1] * PAGE
    alpha = (2.0 * depth) ** 0.25
    f2 = ffn_w_up.shape[2]
    key_dim = GDN_QK_HEADS * GDN_HEAD
    val_dim = GDN_V_HEADS * GDN_HEAD
    conv_dim = 2 * key_dim + val_dim

    groups = {
        "p": dict(nb=b, n=n, tm=1024, tn_cap=512, xf=x_prompt.reshape(b * n, d)),
        "s": dict(nb=db, n=ns, tm=db * ns, tn_cap=1024, xf=x_sample.reshape(db * ns, d)),
    }
    for gr in groups.values():
        gr["xb"] = gr["xf"].astype(BF16)

    def proj(gr, w, layer, col0, ncols, out_dtype=F32, **kw):
        return matmul(gr["xb"], w, layer, col0=col0, n=ncols, tm=gr["tm"], tn=_tile_n(ncols, col0, gr["tn_cap"]),
                      out_dtype=out_dtype, **kw)

    def out_proj(gr, o_b, w, layer):
        kdim = o_b.shape[1]
        return matmul(o_b, w, layer, tm=gr["tm"], tn=_tile_n(d, 0, gr["tn_cap"]), tk=min(kdim, 2048),
                      res=gr["xf"], alpha=alpha, name="out_proj")

    outs = {k: [] for k in ("gdn_s_p", "gdn_c_p", "gdn_s_s", "gdn_c_s", "fox_k_p", "fox_v_p", "fox_lf_p",
                            "fox_k_s", "fox_v_s", "fox_lf_s", "dsa_k_p", "dsa_v_p", "dsa_ki_p",
                            "dsa_k_s", "dsa_v_s", "dsa_ki_s", "ffn_c_p", "ffn_c_s")}
    pos_sq = n_past + jnp.arange(ns)
    pos_sk = jnp.arange(n_past + ns)

    for i in range(depth):
        kind, j = i % 3, i // 3
        for tag, gr in groups.items():
            nb, nn = gr["nb"], gr["n"]
            if kind == 0:
                qkv = proj(gr, gdn_w_in, j, 0, conv_dim).reshape(nb, nn, conv_dim)
                z = proj(gr, gdn_w_in, j, conv_dim, val_dim).reshape(nb, nn, val_dim)
                ba = proj(gr, gdn_w_in, j, conv_dim + val_dim, 128).reshape(nb, nn, 128)
                bt, at = ba[..., :GDN_V_HEADS], ba[..., GDN_V_HEADS:2 * GDN_V_HEADS]
                if tag == "p":
                    conv_buf = jnp.zeros((nb, gdn_conv_w.shape[1] - 1, conv_dim), F32)
                    s0 = jnp.zeros((nb, GDN_V_HEADS, GDN_HEAD, GDN_HEAD), F32)
                    o, s_new = gdn_mixer(qkv, z, ba, conv_buf, s0, gdn_conv_w, gdn_a_log, gdn_dt_bias, gdn_norm_w, j,
                                         blk=512, hg=4)
                    c_new = qkv[:, nn - (gdn_conv_w.shape[1] - 1):]
                else:
                    conv_buf, s0 = state_gdn_conv[j], state_gdn[j]
                    o, c_new, s_new = _gdn_core(qkv, z, bt, at, conv_buf, s0, gdn_conv_w[j], gdn_a_log[j],
                                                gdn_dt_bias[j], gdn_norm_w[j])
                    o = o.astype(BF16)
                outs["gdn_s_" + tag].append(s_new)
                outs["gdn_c_" + tag].append(c_new)
                r = out_proj(gr, o.reshape(nb * nn, val_dim), gdn_w_out, j)
            elif kind == 1:
                k = proj(gr, fox_w_in, j, d, d)
                v = proj(gr, fox_w_in, j, 2 * d, d)
                f = proj(gr, fox_w_in, j, 3 * d, 128)[:, :ATT_HEADS].reshape(nb, nn, ATT_HEADS)
                lf = jax.nn.log_sigmoid(f + fox_b_f[j])
                k4, v4 = k.reshape(nb, nn, ATT_HEADS, HEAD_DIM), v.reshape(nb, nn, ATT_HEADS, HEAD_DIM)
                outs["fox_k_" + tag].append(k4)
                outs["fox_v_" + tag].append(v4)
                outs["fox_lf_" + tag].append(lf)
                if tag == "p":
                    q = proj(gr, fox_w_in, j, 0, d, out_dtype=BF16)
                    o = flash_attention(q.reshape(nb, nn, d), k.reshape(nb, nn, d), v.reshape(nb, nn, d),
                                        r=_suffix_exclusive(lf), tq=512, tk=512, name="fox_flash")
                    o = o.reshape(nb * nn, d)
                else:
                    q = proj(gr, fox_w_in, j, 0, d).reshape(nb, nn, ATT_HEADS, HEAD_DIM)
                    lf_all = jnp.concatenate([_paged_rows(cache_fox_logf[j], page_table), lf], axis=1)
                    rr = _suffix_exclusive(lf_all)
                    o = decode_attention(q, cache_fox_k, cache_fox_v, j, page_table, k4, v4,
                                         r_q=rr[:, n_past:], r_k=rr, name="fox_decode")
                    o = o.reshape(nb * nn, d).astype(BF16)
                r = out_proj(gr, o, fox_w_out, j)
            else:
                k = proj(gr, dsa_w_in, j, d, d)
                v = proj(gr, dsa_w_in, j, 2 * d, d)
                tail = proj(gr, dsa_w_in, j, 3 * d + IDX_HEADS * IDX_DIM, 256)
                ki, _ = layer_norm(tail[:, :IDX_DIM], dsa_idx_ln_g, dsa_idx_ln_b, j, tm=min(gr["tm"], 512),
                                   name="dsa_ki_ln")
                wh = tail[:, IDX_DIM:IDX_DIM + IDX_HEADS] * (IDX_HEADS ** -0.5)
                k4, v4 = k.reshape(nb, nn, ATT_HEADS, HEAD_DIM), v.reshape(nb, nn, ATT_HEADS, HEAD_DIM)
                ki3 = ki.reshape(nb, nn, IDX_DIM)
                outs["dsa_k_" + tag].append(k4)
                outs["dsa_v_" + tag].append(v4)
                outs["dsa_ki_" + tag].append(ki3)
                if tag == "p":
                    q = proj(gr, dsa_w_in, j, 0, d, out_dtype=BF16)
                    qidx = proj(gr, dsa_w_in, j, 3 * d, IDX_HEADS * IDX_DIM, out_dtype=BF16)
                    mask = dsa_select(qidx.reshape(nb, nn, IDX_HEADS * IDX_DIM), ki3, wh.reshape(nb, nn, IDX_HEADS),
                                      tq=256, tk=512, topk=min(TOPK_MAX, nn // 4))
                    o = flash_attention(q.reshape(nb, nn, d), k.reshape(nb, nn, d), v.reshape(nb, nn, d),
                                        mask=mask, tq=512, tk=512, name="dsa_flash")
                    o = o.reshape(nb * nn, d)
                else:
                    q = proj(gr, dsa_w_in, j, 0, d).reshape(nb, nn, ATT_HEADS, HEAD_DIM)
                    qidx = proj(gr, dsa_w_in, j, 3 * d, IDX_HEADS * IDX_DIM).reshape(nb, nn, IDX_HEADS, IDX_DIM)
                    sel = dsa_decode_select(qidx, wh.reshape(nb, nn, IDX_HEADS), cache_dsa_kidx, j, page_table, ki3,
                                            topk=min(TOPK_MAX, (n_past + nn) // 4))
                    o = decode_attention(q, cache_dsa_k, cache_dsa_v, j, page_table, k4, v4, mask=sel,
                                         name="dsa_decode")
                    o = o.reshape(nb * nn, d).astype(BF16)
                r = out_proj(gr, o, dsa_w_out, j)

            ln_tm = min(gr["tm"], 512)
            gr["xf"], gr["xb"] = layer_norm(r, ln_mix_g, ln_mix_b, i, tm=ln_tm, name="ln_mix")

            if tag == "p":
                act, c_new = ffn_up(gr["xb"], jnp.zeros((nb, 2, f2), F32), ffn_w_up, ffn_conv_w, ffn_conv_b, i,
                                    seq_len=nn, tm=gr["tm"], tn=512)
            else:
                hcur = proj(gr, ffn_w_up, i, 0, f2).reshape(nb, nn, f2)
                ext = jnp.concatenate([state_ffn_conv[i], hcur], axis=1)
                c_new = ext[:, -(ffn_conv_w.shape[1] - 1):]
                hc = _causal_dwconv(ext, ffn_conv_w[i]) + ffn_conv_b[i]
                gate, val = jnp.split(hc, [f2 // 2], axis=-1)
                act = (jax.nn.silu(gate) * val).reshape(nb * nn, f2 // 2).astype(BF16)
            outs["ffn_c_" + tag].append(c_new)
            r = matmul(act, ffn_w_down, i, tm=gr["tm"], tn=_tile_n(d, 0, gr["tn_cap"]), tk=f2 // 4,
                       res=gr["xf"], alpha=alpha, name="ffn_down")
            x2f, x2b = layer_norm(r, ln_ffn_g, ln_ffn_b, i, tm=ln_tm, name="ln_ffn")
            p_in = (p_prompt if tag == "p" else p_sample)[i].reshape(nb * nn, -1).astype(BF16)
            gr["xf"], gr["xb"] = per_layer_embed(x2f, x2b, p_in, ple_w_proj, ple_w_gate, i, tm=gr["tm"],
                                                 tn=_tile_n(d, 0, gr["tn_cap"]))

    st = lambda key: jnp.stack(outs[key])
    return (groups["p"]["xf"].reshape(b, n, d), groups["s"]["xf"].reshape(db, ns, d),
            st("gdn_s_p"), st("gdn_c_p"), st("fox_k_p"), st("fox_v_p"), st("fox_lf_p"),
            st("dsa_k_p"), st("dsa_v_p"), st("dsa_ki_p"), st("ffn_c_p"),
            st("gdn_s_s"), st("gdn_c_s"), st("fox_k_s"), st("fox_v_s"), st("fox_lf_s"),
            st("dsa_k_s"), st("dsa_v_s"), st("dsa_ki_s"), st("ffn_c_s"))
```

```python
import functools
import math

import jax
import jax.numpy as jnp
from jax import lax
from jax.experimental import pallas as pl
from jax.experimental.pallas import tpu as pltpu

F32 = jnp.float32
BF16 = jnp.bfloat16

PAGE = 128
GDN_QK_HEADS = 16
GDN_V_HEADS = 32
GDN_HEAD = 128
GDN_CHUNK = 64
ATT_HEADS = 16
HEAD_DIM = 128
IDX_HEADS = 16
IDX_DIM = 128
TOPK_MAX = 256
LN_EPS = 1e-5
RMS_EPS = 1e-6
L2_EPS = 1e-6

V7X_VMEM_LIMIT_BYTES = 56 * 1024 * 1024
NEG_BIG = -1e30
INT_MIN = -(2 ** 31)


def _params(*sem):
    return pltpu.CompilerParams(dimension_semantics=sem, vmem_limit_bytes=V7X_VMEM_LIMIT_BYTES)


def _mm_body(*refs, nk, has_res, alpha):
    if has_res:
        x_ref, w_ref, r_ref, o_ref, *scratch = refs
    else:
        x_ref, w_ref, o_ref, *scratch = refs
        r_ref = None

    def finish(acc):
        if has_res:
            acc = alpha * r_ref[...] + acc
        o_ref[...] = acc.astype(o_ref.dtype)

    part = jnp.dot(x_ref[...], w_ref[...].astype(BF16), preferred_element_type=F32)
    if nk == 1:
        finish(part)
        return
    acc_ref, = scratch
    k = pl.program_id(2)

    @pl.when(k == 0)
    def _():
        acc_ref[...] = part

    @pl.when(k > 0)
    def _():
        acc_ref[...] += part

    @pl.when(k == nk - 1)
    def _():
        finish(acc_ref[...])


def matmul(x, w, layer, *, col0=0, n=None, tm, tn, tk=None, out_dtype=F32, res=None, alpha=None, name="mm"):
    m, kdim = x.shape
    assert x.dtype == BF16 and w.shape[1] == kdim
    n = w.shape[2] - col0 if n is None else n
    tk = kdim if tk is None else tk
    assert m % tm == 0 and n % tn == 0 and kdim % tk == 0 and col0 % tn == 0
    nk = kdim // tk
    cb0 = col0 // tn
    in_specs = [
        pl.BlockSpec((tm, tk), lambda i, j, k: (i, k)),
        pl.BlockSpec((None, tk, tn), lambda i, j, k: (layer, k, cb0 + j)),
    ]
    args = [x, w]
    if res is not None:
        assert res.shape == (m, n)
        in_specs.append(pl.BlockSpec((tm, tn), lambda i, j, k: (i, j)))
        args.append(res)
    return pl.pallas_call(
        functools.partial(_mm_body, nk=nk, has_res=res is not None, alpha=alpha),
        out_shape=jax.ShapeDtypeStruct((m, n), out_dtype),
        grid=(m // tm, n // tn, nk),
        in_specs=in_specs,
        out_specs=pl.BlockSpec((tm, tn), lambda i, j, k: (i, j)),
        scratch_shapes=[pltpu.VMEM((tm, tn), F32)] if nk > 1 else [],
        compiler_params=_params("parallel", "parallel", "arbitrary"),
        name=name,
    )(*args)


def _ln_body(x_ref, g_ref, b_ref, of_ref, ob_ref, *, eps):
    x = x_ref[...]
    mu = jnp.mean(x, -1, keepdims=True)
    xc = x - mu
    var = jnp.mean(xc * xc, -1, keepdims=True)
    y = xc * lax.rsqrt(var + eps) * g_ref[...] + b_ref[...]
    of_ref[...] = y
    ob_ref[...] = y.astype(BF16)


def layer_norm(x, g, b, layer, *, tm, eps=LN_EPS, name="ln"):
    m, n = x.shape
    assert m % tm == 0
    g3 = g.reshape(g.shape[0], 1, n)
    b3 = b.reshape(b.shape[0], 1, n)
    vec = pl.BlockSpec((None, 1, n), lambda i: (layer, 0, 0))
    row = pl.BlockSpec((tm, n), lambda i: (i, 0))
    return pl.pallas_call(
        functools.partial(_ln_body, eps=eps),
        out_shape=(jax.ShapeDtypeStruct((m, n), F32), jax.ShapeDtypeStruct((m, n), BF16)),
        grid=(m // tm,),
        in_specs=[row, vec, vec],
        out_specs=(row, row),
        compiler_params=_params("parallel"),
        name=name,
    )(x, g3, b3)


def _ple_body(xb_ref, pb_ref, wg_ref, wp_ref, xr_ref, of_ref, ob_ref):
    gate = jnp.dot(xb_ref[...], wg_ref[...].astype(BF16), preferred_element_type=F32)
    proj = jnp.dot(pb_ref[...], wp_ref[...].astype(BF16), preferred_element_type=F32)
    y = xr_ref[...] + proj * jax.nn.sigmoid(gate)
    of_ref[...] = y
    ob_ref[...] = y.astype(BF16)


def per_layer_embed(x_f, x_b, p_b, w_proj, w_gate, layer, *, tm, tn, name="ple"):
    m, d = x_f.shape
    pd = p_b.shape[1]
    assert m % tm == 0 and d % tn == 0
    out = pl.BlockSpec((tm, tn), lambda i, j: (i, j))
    return pl.pallas_call(
        _ple_body,
        out_shape=(jax.ShapeDtypeStruct((m, d), F32), jax.ShapeDtypeStruct((m, d), BF16)),
        grid=(m // tm, d // tn),
        in_specs=[
            pl.BlockSpec((tm, d), lambda i, j: (i, 0)),
            pl.BlockSpec((tm, pd), lambda i, j: (i, 0)),
            pl.BlockSpec((None, d, tn), lambda i, j: (layer, 0, j)),
            pl.BlockSpec((None, pd, tn), lambda i, j: (layer, 0, j)),
            out,
        ],
        out_specs=(out, out),
        compiler_params=_params("parallel", "parallel"),
        name=name,
    )(x_b, p_b, w_gate, w_proj, x_f)


HALO_ROWS = 16


def _ffn_up_body(x_ref, xh_ref, wg_ref, wv_ref, cwg_ref, cwv_ref, cbg_ref, cbv_ref, hg_ref, hv_ref,
                 act_ref, tg_ref, tv_ref, hs_ref, *, tiles_per_seq, tm):
    first = (pl.program_id(0) % tiles_per_seq) == 0
    x = x_ref[...]
    xh = xh_ref[...]
    conv = []
    for w_ref, cw_ref, cb_ref, hist_ref, tail_ref in ((wg_ref, cwg_ref, cbg_ref, hg_ref, tg_ref),
                                                     (wv_ref, cwv_ref, cbv_ref, hv_ref, tv_ref)):
        w = w_ref[...].astype(BF16)
        h = jnp.dot(x, w, preferred_element_type=F32)
        halo = jnp.dot(xh, w, preferred_element_type=F32)
        hs_ref[0:8, :] = jnp.where(first, hist_ref[...], halo[HALO_ROWS - 8:, :])
        hs_ref[8:8 + tm, :] = h
        cw = cw_ref[...]
        c = hs_ref[6:6 + tm, :] * cw[0:1, :] + hs_ref[7:7 + tm, :] * cw[1:2, :] + h * cw[2:3, :] + cb_ref[...]
        conv.append(c)
        tail_ref[...] = hs_ref[8 + tm - 2:8 + tm, :]
    act_ref[...] = (jax.nn.silu(conv[0]) * conv[1]).astype(act_ref.dtype)


def ffn_up(x_b, hist, w_up, conv_w, conv_b, layer, *, seq_len, tm, tn, name="ffn_up"):
    m, d = x_b.shape
    f2 = w_up.shape[2]
    f = f2 // 2
    nb = m // seq_len
    assert seq_len % tm == 0 and f % tn == 0 and tm % HALO_ROWS == 0
    tiles_per_seq = seq_len // tm
    nj = f // tn
    hist8 = jnp.concatenate([jnp.zeros((nb, 6, f2), F32), hist.astype(F32)], axis=1)
    cw = conv_w
    cb = conv_b.reshape(conv_b.shape[0], 1, f2)
    halo_blocks = tm // HALO_ROWS

    def wspec(off):
        return pl.BlockSpec((None, d, tn), lambda i, j: (layer, 0, off + j))

    def cwspec(off):
        return pl.BlockSpec((None, cw.shape[1], tn), lambda i, j: (layer, 0, off + j))

    def cbspec(off):
        return pl.BlockSpec((None, 1, tn), lambda i, j: (layer, 0, off + j))

    def hspec(off):
        return pl.BlockSpec((None, 8, tn), lambda i, j: (i // tiles_per_seq, 0, off + j))

    tail = pl.BlockSpec((None, 2, tn), lambda i, j: (i, 0, j))
    act, tail_g, tail_v = pl.pallas_call(
        functools.partial(_ffn_up_body, tiles_per_seq=tiles_per_seq, tm=tm),
        out_shape=(jax.ShapeDtypeStruct((m, f), BF16),
                   jax.ShapeDtypeStruct((m // tm, 2, f), F32), jax.ShapeDtypeStruct((m // tm, 2, f), F32)),
        grid=(m // tm, nj),
        in_specs=[
            pl.BlockSpec((tm, d), lambda i, j: (i, 0)),
            pl.BlockSpec((HALO_ROWS, d), lambda i, j: (jnp.maximum(i * halo_blocks - 1, 0), 0)),
            wspec(0), wspec(nj), cwspec(0), cwspec(nj), cbspec(0), cbspec(nj), hspec(0), hspec(nj),
        ],
        out_specs=(pl.BlockSpec((tm, tn), lambda i, j: (i, j)), tail, tail),
        scratch_shapes=[pltpu.VMEM((tm + 8, tn), F32)],
        compiler_params=_params("arbitrary", "arbitrary"),
        name=name,
    )(x_b, x_b, w_up, w_up, cw, cw, cb, cb, hist8, hist8)
    last = slice(tiles_per_seq - 1, None, tiles_per_seq)
    return act, jnp.concatenate([tail_g[last], tail_v[last]], axis=-1)


def _flash_body(*refs, tq, tk, scale, has_bias, has_mask):
    refs = list(refs)
    q_ref, k_ref, v_ref = refs[:3]
    rest = refs[3:]
    if has_bias:
        rq_ref, rk_ref = rest[:2]
        rest = rest[2:]
    if has_mask:
        mask_ref = rest[0]
        rest = rest[1:]
    if has_bias:
        o_ref, m_ref, l_ref, acc_ref, rqb_ref = rest
    else:
        o_ref, m_ref, l_ref, acc_ref = rest
    qi = pl.program_id(2)
    lanes = m_ref.shape[1]
    nblk = tk // lanes
    m_ref[...] = jnp.full(m_ref.shape, NEG_BIG, F32)
    l_ref[...] = jnp.zeros(l_ref.shape, F32)
    acc_ref[...] = jnp.zeros(acc_ref.shape, F32)
    if has_bias:
        rqb_ref[...] = jnp.broadcast_to(rq_ref[...], rqb_ref.shape)
    tri = lax.broadcasted_iota(jnp.int32, (tq, lanes), 1) - lax.broadcasted_iota(jnp.int32, (tq, lanes), 0)

    def chunk(c, diagonal):
        off = pl.multiple_of(c * tk, tk)
        kc = k_ref[pl.ds(off, tk), :].astype(BF16)
        vc = v_ref[pl.ds(off, tk), :].astype(BF16)
        s = lax.dot_general(q_ref[...], kc, (((1,), (1,)), ((), ())), preferred_element_type=F32) * scale
        blocks = [s[:, j * lanes:(j + 1) * lanes] for j in range(nblk)]
        if has_bias:
            rk = rk_ref[c]
            rqb = rqb_ref[...]
            blocks = [blocks[j] + rk[:, j * lanes:(j + 1) * lanes] - rqb for j in range(nblk)]
        if has_mask:
            keep = mask_ref[c].astype(jnp.int32)
            blocks = [jnp.where(keep[:, j * lanes:(j + 1) * lanes] != 0, blocks[j], NEG_BIG) for j in range(nblk)]
        elif diagonal:
            blocks = [jnp.where(tri + j * lanes <= 0, blocks[j], NEG_BIG) for j in range(nblk)]
        bmax = blocks[0]
        for blk in blocks[1:]:
            bmax = jnp.maximum(bmax, blk)
        m_old = m_ref[...]
        m_new = jnp.maximum(m_old, jnp.max(bmax, -1, keepdims=True))
        a = jnp.exp(m_old - m_new)
        ps = [jnp.exp(blk - m_new) for blk in blocks]
        psum = ps[0]
        for pj in ps[1:]:
            psum = psum + pj
        l_ref[...] = a * l_ref[...] + psum
        p = jnp.concatenate(ps, axis=1).astype(BF16)
        acc_ref[...] = a * acc_ref[...] + jnp.dot(p, vc, preferred_element_type=F32)
        m_ref[...] = m_new

    def full_chunk(c, carry):
        chunk(c, False)
        return carry

    if has_mask:
        lax.fori_loop(0, qi + 1, full_chunk, 0)
    else:
        lax.fori_loop(0, qi, full_chunk, 0)
        chunk(qi, True)
    o_ref[...] = (acc_ref[...] / jnp.sum(l_ref[...], -1, keepdims=True)).astype(o_ref.dtype)


def flash_attention(q, k, v, *, r=None, mask=None, tq, tk, name="flash"):
    b, n, hd = q.shape
    h = hd // HEAD_DIM
    assert n % tq == 0 and tq == tk and tk % HEAD_DIM == 0
    nkc = n // tk
    in_specs = [
        pl.BlockSpec((None, tq, HEAD_DIM), lambda bi, hi, qi: (bi, qi, hi)),
        pl.BlockSpec((None, n, HEAD_DIM), lambda bi, hi, qi: (bi, 0, hi)),
        pl.BlockSpec((None, n, HEAD_DIM), lambda bi, hi, qi: (bi, 0, hi)),
    ]
    args = [q, k, v]
    if r is not None:
        rt = r.transpose(0, 2, 1)
        in_specs.append(pl.BlockSpec((None, None, tq, 1), lambda bi, hi, qi: (bi, hi, qi, 0)))
        in_specs.append(pl.BlockSpec((None, None, nkc, 1, tk), lambda bi, hi, qi: (bi, hi, 0, 0, 0)))
        args += [rt.reshape(b, h, n, 1), rt.reshape(b, h, nkc, 1, tk)]
    if mask is not None:
        in_specs.append(pl.BlockSpec((None, nkc, tq, tk), lambda bi, hi, qi: (bi, 0, qi, 0)))
        args.append(mask)
    return pl.pallas_call(
        functools.partial(_flash_body, tq=tq, tk=tk, scale=HEAD_DIM ** -0.5,
                          has_bias=r is not None, has_mask=mask is not None),
        out_shape=jax.ShapeDtypeStruct((b, n, hd), BF16),
        grid=(b, h, n // tq),
        in_specs=in_specs,
        out_specs=pl.BlockSpec((None, tq, HEAD_DIM), lambda bi, hi, qi: (bi, qi, hi)),
        scratch_shapes=[pltpu.VMEM((tq, HEAD_DIM), F32)] * (4 if r is not None else 3),
        compiler_params=_params("parallel", "parallel", "arbitrary"),
        name=name,
    )(*args)


def _dsa_select_body(q_ref, kidx_ref, wh_ref, mask_ref, keys_ref, *, tq, tk, topk, n_keys):
    qi = pl.program_id(1)
    n_valid = ((qi + 1) * tq + tk - 1) // tk
    whs = wh_ref[...] * (IDX_DIM ** -0.5)
    q_pos = qi * tq + lax.broadcasted_iota(jnp.int32, (tq, tk), 0)
    k_iota = lax.broadcasted_iota(jnp.int32, (tq, tk), 1)
    n_heads = wh_ref.shape[-1]
    lanes = 128

    def score_chunk(c, carry):
        off = pl.multiple_of(c * tk, tk)
        kc = kidx_ref[pl.ds(off, tk), :].astype(BF16)
        acc = jnp.zeros((tq, tk), F32)
        for h in range(n_heads):
            d = lax.dot_general(q_ref[:, h * IDX_DIM:(h + 1) * IDX_DIM], kc, (((1,), (1,)), ((), ())),
                                preferred_element_type=F32)
            acc = acc + jnp.maximum(d, 0.0) * whs[:, h:h + 1]
        bits = pltpu.bitcast(acc, jnp.int32)
        key = bits ^ ((bits >> 31) & 0x7FFFFFFF)
        keys_ref[c] = jnp.where((off + k_iota) <= q_pos, key, INT_MIN)
        return carry

    lax.fori_loop(0, n_valid, score_chunk, 0)

    def count(pred):
        def body(c, part):
            ind = jnp.where(pred(keys_ref[c], c * tk), 1.0, 0.0)
            for s in range(tk // lanes):
                part = part + ind[:, s * lanes:(s + 1) * lanes]
            return part
        part = lax.fori_loop(0, n_valid, body, jnp.zeros((tq, lanes), F32))
        return jnp.sum(part, -1, keepdims=True)

    kf = float(topk)

    def value_bit(it, t):
        cand = t + lax.shift_left(jnp.int32(1), 31 - it)
        cnt = count(lambda kk, off: kk >= cand)
        return jnp.where(cnt >= kf, cand, t)

    thr = lax.fori_loop(0, 32, value_bit, jnp.full((tq, 1), INT_MIN, jnp.int32))
    cnt_ge = count(lambda kk, off: kk >= thr)
    cnt_gt = count(lambda kk, off: kk > thr)
    need = kf - cnt_gt
    idx_bits = int(n_keys).bit_length()

    def tie_search(_):
        def index_bit(it, p):
            cand = p + lax.shift_left(jnp.int32(1), idx_bits - 1 - it)
            cnt = count(lambda kk, off: (kk == thr) & ((off + k_iota) < cand))
            return jnp.where(cnt < need, cand, p)
        return lax.fori_loop(0, idx_bits, index_bit, jnp.zeros((tq, 1), jnp.int32))

    excess = jnp.max(cnt_ge - kf) > 0.0
    last_tie = lax.cond(excess, tie_search, lambda _: jnp.full((tq, 1), n_keys, jnp.int32), 0)

    mask_ref[...] = jnp.zeros(mask_ref.shape, mask_ref.dtype)

    def emit(c, carry):
        off = c * tk
        kk = keys_ref[c]
        k_pos = off + k_iota
        sel = (kk > thr) | ((kk == thr) & (k_pos <= last_tie))
        sel = sel & (k_pos <= q_pos)
        mask_ref[c] = jnp.where(sel, 1, 0).astype(mask_ref.dtype)
        return carry

    lax.fori_loop(0, n_valid, emit, 0)


def dsa_select(qidx, kidx, wh, *, tq, tk, topk, name="dsa_select"):
    b, n, _ = qidx.shape
    assert n % tq == 0 and n % tk == 0
    nkc = n // tk
    return pl.pallas_call(
        functools.partial(_dsa_select_body, tq=tq, tk=tk, topk=topk, n_keys=n),
        out_shape=jax.ShapeDtypeStruct((b, nkc, n, tk), jnp.int8),
        grid=(b, n // tq),
        in_specs=[
            pl.BlockSpec((None, tq, qidx.shape[2]), lambda bi, qi: (bi, qi, 0)),
            pl.BlockSpec((None, n, kidx.shape[2]), lambda bi, qi: (bi, 0, 0)),
            pl.BlockSpec((None, tq, wh.shape[2]), lambda bi, qi: (bi, qi, 0)),
        ],
        out_specs=pl.BlockSpec((None, nkc, tq, tk), lambda bi, qi: (bi, 0, qi, 0)),
        scratch_shapes=[pltpu.VMEM((nkc, tq, tk), jnp.int32)],
        compiler_params=_params("parallel", "arbitrary"),
        name=name,
    )(qidx, kidx, wh)


def _decode_attn_body(*refs, n_pages, nq, n_heads, scale, has_bias, has_mask):
    refs = list(refs)
    pt_ref, q_ref, k_ref, v_ref, kn_ref, vn_ref = refs[:6]
    rest = refs[6:]
    if has_bias:
        rq_ref, rk_ref = rest[:2]
        rest = rest[2:]
    if has_mask:
        mask_ref = rest[0]
        rest = rest[1:]
    o_ref, m_s, l_s, acc_s = rest
    p = pl.program_id(1)
    rows = nq * n_heads
    page = k_ref.shape[0]

    @pl.when(p == 0)
    def _():
        m_s[...] = jnp.full(m_s.shape, NEG_BIG, F32)
        l_s[...] = jnp.zeros(l_s.shape, F32)
        acc_s[...] = jnp.zeros(acc_s.shape, F32)

    def attend(k_src, v_src, new_tokens):
        kc = k_src[...].astype(BF16)
        vc = v_src[...].astype(BF16)
        s = _dot_nt(q_ref[...], kc) * scale
        if has_bias:
            s = s + jnp.concatenate([rk_ref[...]] * nq, axis=0) - rq_ref[...]
        ok = None
        if has_mask:
            mk = mask_ref[...]
            ok = jnp.concatenate([jnp.broadcast_to(mk[i:i + 1, :], (n_heads, page)) for i in range(nq)], axis=0) > 0.0
        elif new_tokens:
            q_idx = lax.broadcasted_iota(jnp.int32, (rows, page), 0) // n_heads
            t_idx = lax.broadcasted_iota(jnp.int32, (rows, page), 1)
            ok = (t_idx <= q_idx) & (t_idx < nq)
        if ok is not None:
            s = jnp.where(ok, s, NEG_BIG)
        m_old = m_s[...]
        m_new = jnp.maximum(m_old, jnp.max(s, -1, keepdims=True))
        pr = jnp.exp(s - m_new)
        a = jnp.exp(m_old - m_new)
        l_s[...] = a * l_s[...] + jnp.sum(pr, -1, keepdims=True)
        acc_s[...] = a * acc_s[...] + _dot(pr.astype(BF16), vc)
        m_s[...] = m_new

    @pl.when(p < n_pages)
    def _():
        attend(k_ref, v_ref, False)

    @pl.when(p == n_pages)
    def _():
        attend(kn_ref, vn_ref, True)
        hd = acc_s.shape[1] // n_heads
        o = acc_s[...] / l_s[...]
        r_head = lax.broadcasted_iota(jnp.int32, o.shape, 0) % n_heads
        c_head = lax.broadcasted_iota(jnp.int32, o.shape, 1) // hd
        o = jnp.where(r_head == c_head, o, 0.0)
        o_ref[...] = jnp.sum(o.reshape(nq, n_heads, o.shape[1]), axis=1)


def decode_attention(q, pool_k, pool_v, layer, page_table, k_new, v_new, *, r_q=None, r_k=None, mask=None,
                     name="decode_attn"):
    b, nq, h, d = q.shape
    n_pool, page = pool_k.shape[1], pool_k.shape[2]
    n_pages = page_table.shape[1]
    hd = h * d
    rows = nq * h
    eye = jnp.eye(h, dtype=q.dtype)
    q_exp = (q[:, :, :, None, :] * eye[None, None, :, :, None]).reshape(b, rows, hd).astype(BF16)
    pad = lambda a: jnp.pad(a.reshape(b, nq, hd), ((0, 0), (0, page - nq), (0, 0)))
    pk = pool_k.reshape(pool_k.shape[0], n_pool, page, hd)
    pv = pool_v.reshape(pool_v.shape[0], n_pool, page, hd)
    last = n_pages - 1
    pool_spec = pl.BlockSpec((None, None, page, hd), lambda bi, p, pt: (layer, pt[bi, jnp.minimum(p, last)], 0, 0))
    new_spec = pl.BlockSpec((None, page, hd), lambda bi, p, pt: (bi, 0, 0))
    in_specs = [pl.BlockSpec((None, rows, hd), lambda bi, p, pt: (bi, 0, 0)), pool_spec, pool_spec, new_spec, new_spec]
    args = [q_exp, pk, pv, pad(k_new), pad(v_new)]
    if r_q is not None:
        n_tot = r_k.shape[1]
        rk = jnp.pad(r_k, ((0, 0), (0, (n_pages + 1) * page - n_tot), (0, 0)))
        rk = rk.reshape(b, n_pages + 1, page, h).transpose(0, 1, 3, 2)
        in_specs.append(pl.BlockSpec((None, rows, 1), lambda bi, p, pt: (bi, 0, 0)))
        in_specs.append(pl.BlockSpec((None, None, h, page), lambda bi, p, pt: (bi, p, 0, 0)))
        args += [r_q.reshape(b, rows, 1), rk]
    if mask is not None:
        in_specs.append(pl.BlockSpec((None, None, 8, page), lambda bi, p, pt: (bi, p, 0, 0)))
        args.append(mask)
    return pl.pallas_call(
        functools.partial(_decode_attn_body, n_pages=n_pages, nq=nq, n_heads=h, scale=d ** -0.5,
                          has_bias=r_q is not None, has_mask=mask is not None),
        out_shape=jax.ShapeDtypeStruct((b, nq, hd), F32),
        grid_spec=pltpu.PrefetchScalarGridSpec(
            num_scalar_prefetch=1,
            grid=(b, n_pages + 1),
            in_specs=in_specs,
            out_specs=pl.BlockSpec((None, nq, hd), lambda bi, p, pt: (bi, 0, 0)),
            scratch_shapes=[pltpu.VMEM((rows, 1), F32), pltpu.VMEM((rows, 1), F32), pltpu.VMEM((rows, hd), F32)],
        ),
        compiler_params=_params("parallel", "arbitrary"),
        name=name,
    )(page_table, *args)


def _dsa_decode_select_body(pt_ref, q_ref, wh_ref, kidx_ref, kin_ref, mask_ref, keys_s, *, n_pages, nq, n_heads,
                            topk, n_keys):
    p = pl.program_id(1)
    page = kidx_ref.shape[0]
    lane = lax.broadcasted_iota(jnp.int32, (8, page), 1)
    q_row = lax.broadcasted_iota(jnp.int32, (8, page), 0)

    def score_page(src):
        kc = src[...].astype(BF16)
        d = _dot_nt(q_ref[...], kc)
        rel = jnp.maximum(d, 0.0) * (wh_ref[...] * (IDX_DIM ** -0.5))
        sc = jnp.sum(rel.reshape(nq, n_heads, page), axis=1)
        sc = jnp.concatenate([sc, jnp.zeros((8 - nq, page), F32)], axis=0)
        bits = pltpu.bitcast(sc, jnp.int32)
        return bits ^ ((bits >> 31) & 0x7FFFFFFF)

    @pl.when(p < n_pages)
    def _():
        keys_s[p] = score_page(kidx_ref)

    @pl.when(p == n_pages)
    def _():
        key = score_page(kin_ref)
        keys_s[p] = jnp.where((lane <= q_row) & (lane < nq), key, INT_MIN)

        def count(pred):
            def body(c, part):
                return part + jnp.where(pred(keys_s[c], c * page + lane), 1.0, 0.0)
            part = lax.fori_loop(0, n_pages + 1, body, jnp.zeros((8, page), F32))
            return jnp.sum(part, -1, keepdims=True)

        kf = float(topk)

        def value_bit(it, t):
            cand = t + lax.shift_left(jnp.int32(1), 31 - it)
            return jnp.where(count(lambda kk, pos: kk >= cand) >= kf, cand, t)

        thr = lax.fori_loop(0, 32, value_bit, jnp.full((8, 1), INT_MIN, jnp.int32))
        need = kf - count(lambda kk, pos: kk > thr)
        idx_bits = int(n_keys).bit_length()

        def index_bit(it, pos_max):
            cand = pos_max + lax.shift_left(jnp.int32(1), idx_bits - 1 - it)
            cnt = count(lambda kk, pos: (kk == thr) & (pos < cand))
            return jnp.where(cnt < need, cand, pos_max)

        last_tie = lax.fori_loop(0, idx_bits, index_bit, jnp.zeros((8, 1), jnp.int32))

        def emit(c, carry):
            kk = keys_s[c]
            pos = c * page + lane
            sel = (kk > thr) | ((kk == thr) & (pos <= last_tie))
            sel = sel & (kk != INT_MIN)
            mask_ref[c] = jnp.where(sel, 1.0, 0.0)
            return carry

        lax.fori_loop(0, n_pages + 1, emit, 0)


def dsa_decode_select(qidx, wh, pool_kidx, layer, page_table, kidx_new, *, topk, name="dsa_decode_select"):
    b, nq, hi, di = qidx.shape
    page = pool_kidx.shape[2]
    n_pages = page_table.shape[1]
    rows = nq * hi
    last = n_pages - 1
    kin = jnp.pad(kidx_new, ((0, 0), (0, page - nq), (0, 0)))
    return pl.pallas_call(
        functools.partial(_dsa_decode_select_body, n_pages=n_pages, nq=nq, n_heads=hi, topk=topk,
                          n_keys=n_pages * page + nq),
        out_shape=jax.ShapeDtypeStruct((b, n_pages + 1, 8, page), F32),
        grid_spec=pltpu.PrefetchScalarGridSpec(
            num_scalar_prefetch=1,
            grid=(b, n_pages + 1),
            in_specs=[
                pl.BlockSpec((None, rows, di), lambda bi, p, pt: (bi, 0, 0)),
                pl.BlockSpec((None, rows, 1), lambda bi, p, pt: (bi, 0, 0)),
                pl.BlockSpec((None, None, page, di), lambda bi, p, pt: (layer, pt[bi, jnp.minimum(p, last)], 0, 0)),
                pl.BlockSpec((None, page, di), lambda bi, p, pt: (bi, 0, 0)),
            ],
            out_specs=pl.BlockSpec((None, n_pages + 1, 8, page), lambda bi, p, pt: (bi, 0, 0, 0)),
            scratch_shapes=[pltpu.VMEM((n_pages + 1, 8, page), jnp.int32)],
        ),
        compiler_params=_params("parallel", "arbitrary"),
        name=name,
    )(page_table, qidx.reshape(b, rows, di).astype(BF16), wh.reshape(b, rows, 1), pool_kidx, kin)


def _paged_attn_body(*refs, n_steps, pg, nq, n_heads, scale, has_bias, has_mask):
    refs = list(refs)
    pt_ref, q_ref, hm_ref = refs[:3]
    k_refs = refs[3:3 + pg]
    v_refs = refs[3 + pg:3 + 2 * pg]
    kn_ref, vn_ref = refs[3 + 2 * pg:5 + 2 * pg]
    rest = refs[5 + 2 * pg:]
    if has_bias:
        rq_ref, rk_ref = rest[:2]
        rest = rest[2:]
    if has_mask:
        mask_ref, expand_ref = rest[:2]
        rest = rest[2:]
    o_ref, m_s, l_s, acc_s = rest
    p = pl.program_id(1)
    rows = nq * n_heads
    cols = k_refs[0].shape[0]

    @pl.when(p == 0)
    def _():
        m_s[...] = jnp.full(m_s.shape, NEG_BIG, F32)
        l_s[...] = jnp.zeros(l_s.shape, F32)
        acc_s[...] = jnp.zeros(acc_s.shape, F32)

    def attend(k_src, v_src, slot, new_tokens):
        kc = k_src[...].astype(BF16)
        vc = v_src[...].astype(BF16)
        s = _dot_nt(q_ref[...], kc) * scale
        if has_bias:
            s = s + rk_ref[slot] - rq_ref[...]
        if has_mask:
            sel = _dot(mask_ref[slot].astype(BF16), expand_ref[...])
            sel = jnp.concatenate([jnp.broadcast_to(sel[i:i + 1, :], (n_heads, cols)) for i in range(nq)], axis=0)
            s = jnp.where(sel > 0.5, s, NEG_BIG)
        elif new_tokens:
            q_idx = lax.broadcasted_iota(jnp.int32, (rows, cols), 0) // n_heads
            t_idx = lax.broadcasted_iota(jnp.int32, (rows, cols), 1) // n_heads
            s = jnp.where((t_idx <= q_idx) & (t_idx < nq), s, NEG_BIG)
        s = s + hm_ref[...]
        m_old = m_s[...]
        m_new = jnp.maximum(m_old, jnp.max(s, -1, keepdims=True))
        pr = jnp.exp(s - m_new)
        a = jnp.exp(m_old - m_new)
        l_s[...] = a * l_s[...] + jnp.sum(pr, -1, keepdims=True)
        acc_s[...] = a * acc_s[...] + _dot(pr.astype(BF16), vc)
        m_s[...] = m_new

    @pl.when(p < n_steps - 1)
    def _():
        for i in range(pg):
            attend(k_refs[i], v_refs[i], i, False)

    @pl.when(p == n_steps - 1)
    def _():
        attend(kn_ref, vn_ref, 0, True)
        o_ref[...] = acc_s[...] / l_s[...]


def paged_attention(q, pool_k, pool_v, layer, page_table, k_new, v_new, *, pg, r_q=None, r_k=None, mask=None,
                    name="paged_attn"):
    b, nq, h, d = q.shape
    n_pool, page = pool_k.shape[1], pool_k.shape[2]
    n_pages = page_table.shape[1]
    assert n_pages % pg == 0 and nq <= 8
    n_steps = n_pages // pg + 1
    rows, cols = nq * h, page * h
    pk = pool_k.reshape(pool_k.shape[0], n_pool, cols, d)
    pv = pool_v.reshape(pool_v.shape[0], n_pool, cols, d)
    pad = lambda a: jnp.pad(a, ((0, 0), (0, page - nq), (0, 0), (0, 0))).reshape(b, cols, d)
    ri = lax.broadcasted_iota(jnp.int32, (rows, cols), 0) % h
    ci = lax.broadcasted_iota(jnp.int32, (rows, cols), 1) % h
    head_match = jnp.where(ri == ci, 0.0, NEG_BIG).astype(F32)
    last = n_pages - 1

    def pool_spec(i):
        return pl.BlockSpec((None, None, cols, d),
                            lambda bi, p, pt: (layer, pt[bi, jnp.minimum(p * pg + i, last)], 0, 0))

    const2 = lambda shape: pl.BlockSpec(shape, lambda bi, p, pt: (0, 0))
    per_b = lambda shape: pl.BlockSpec((None,) + shape, lambda bi, p, pt: (bi,) + (0,) * len(shape))
    in_specs = ([per_b((rows, d)), const2((rows, cols))] + [pool_spec(i) for i in range(pg)] * 2
                + [per_b((cols, d)), per_b((cols, d))])
    args = [q.reshape(b, rows, d).astype(BF16), head_match] + [pk] * pg + [pv] * pg + [pad(k_new), pad(v_new)]
    if r_q is not None:
        n_tot = r_k.shape[1]
        rk = jnp.pad(r_k, ((0, 0), (0, n_pages * page + pg * page - n_tot), (0, 0)))
        rk = rk.reshape(b, n_steps, pg, 1, cols)
        in_specs += [per_b((rows, 1)), pl.BlockSpec((None, None, pg, 1, cols), lambda bi, p, pt: (bi, p, 0, 0, 0))]
        args += [r_q.reshape(b, rows, 1), rk]
    if mask is not None:
        expand = (lax.broadcasted_iota(jnp.int32, (page, cols), 1) // h
                  == lax.broadcasted_iota(jnp.int32, (page, cols), 0)).astype(BF16)
        in_specs += [pl.BlockSpec((None, None, pg, 8, page), lambda bi, p, pt: (bi, p, 0, 0, 0)), const2((page, cols))]
        assert mask.shape[1] >= n_steps * pg
        args += [mask[:, :n_steps * pg].reshape(b, n_steps, pg, 8, page), expand]
    out = pl.pallas_call(
        functools.partial(_paged_attn_body, n_steps=n_steps, pg=pg, nq=nq, n_heads=h, scale=d ** -0.5,
                          has_bias=r_q is not None, has_mask=mask is not None),
        out_shape=jax.ShapeDtypeStruct((b, rows, d), F32),
        grid_spec=pltpu.PrefetchScalarGridSpec(
            num_scalar_prefetch=1,
            grid=(b, n_steps),
            in_specs=in_specs,
            out_specs=per_b((rows, d)),
            scratch_shapes=[pltpu.VMEM((rows, 1), F32), pltpu.VMEM((rows, 1), F32), pltpu.VMEM((rows, d), F32)],
        ),
        compiler_params=_params("parallel", "arbitrary"),
        name=name,
    )(page_table, *args)
    return out.reshape(b, nq, h * d)


def _paged_select_body(*refs, n_steps, pg, nq, n_heads, topk, n_keys):
    pt_ref, q_ref, wh_ref = refs[:3]
    kidx_refs = refs[3:3 + pg]
    kin_ref, mask_ref, keys_s = refs[3 + pg:]
    p = pl.program_id(1)
    page = kin_ref.shape[0]
    lane = lax.broadcasted_iota(jnp.int32, (8, page), 1)
    q_row = lax.broadcasted_iota(jnp.int32, (8, page), 0)
    n_slots = n_steps * pg

    def score_page(src):
        d = _dot_nt(q_ref[...], src[...].astype(BF16))
        rel = jnp.maximum(d, 0.0) * (wh_ref[...] * (IDX_DIM ** -0.5))
        sc = jnp.sum(rel.reshape(nq, n_heads, page), axis=1)
        sc = jnp.concatenate([sc, jnp.zeros((8 - nq, page), F32)], axis=0)
        bits = pltpu.bitcast(sc, jnp.int32)
        return bits ^ ((bits >> 31) & 0x7FFFFFFF)

    @pl.when(p < n_steps - 1)
    def _():
        for i in range(pg):
            keys_s[p * pg + i] = score_page(kidx_refs[i])

    @pl.when(p == n_steps - 1)
    def _():
        first_new = (n_steps - 1) * pg
        keys_s[first_new] = jnp.where((lane <= q_row) & (lane < nq), score_page(kin_ref), INT_MIN)
        for i in range(1, pg):
            keys_s[first_new + i] = jnp.full((8, page), INT_MIN, jnp.int32)

        def count(pred):
            def body(c, part):
                return part + jnp.where(pred(keys_s[c], c * page + lane), 1.0, 0.0)
            part = lax.fori_loop(0, n_slots, body, jnp.zeros((8, page), F32))
            return jnp.sum(part, -1, keepdims=True)

        kf = float(topk)

        def value_bit(it, t):
            cand = t + lax.shift_left(jnp.int32(1), 31 - it)
            return jnp.where(count(lambda kk, pos: kk >= cand) >= kf, cand, t)

        thr = lax.fori_loop(0, 32, value_bit, jnp.full((8, 1), INT_MIN, jnp.int32))
        need = kf - count(lambda kk, pos: kk > thr)
        idx_bits = int(n_keys).bit_length()

        def index_bit(it, pos_max):
            cand = pos_max + lax.shift_left(jnp.int32(1), idx_bits - 1 - it)
            cnt = count(lambda kk, pos: (kk == thr) & (pos < cand))
            return jnp.where(cnt < need, cand, pos_max)

        last_tie = lax.fori_loop(0, idx_bits, index_bit, jnp.zeros((8, 1), jnp.int32))

        def emit(c, carry):
            kk = keys_s[c]
            sel = (kk > thr) | ((kk == thr) & (c * page + lane <= last_tie))
            mask_ref[c] = jnp.where(sel & (kk != INT_MIN), 1.0, 0.0)
            return carry

        lax.fori_loop(0, n_slots, emit, 0)


def paged_select(qidx, wh, pool_kidx, layer, page_table, kidx_new, *, topk, pg, name="paged_select"):
    b, nq, hi, di = qidx.shape
    page = pool_kidx.shape[2]
    n_pages = page_table.shape[1]
    assert n_pages % pg == 0 and nq <= 8
    n_steps = n_pages // pg + 1
    rows = nq * hi
    last = n_pages - 1
    kin = jnp.pad(kidx_new, ((0, 0), (0, page - nq), (0, 0)))

    def pool_spec(i):
        return pl.BlockSpec((None, None, page, di),
                            lambda bi, p, pt: (layer, pt[bi, jnp.minimum(p * pg + i, last)], 0, 0))

    per_b = lambda shape: pl.BlockSpec((None,) + shape, lambda bi, p, pt: (bi,) + (0,) * len(shape))
    mask = pl.pallas_call(
        functools.partial(_paged_select_body, n_steps=n_steps, pg=pg, nq=nq, n_heads=hi, topk=topk,
                          n_keys=n_pages * page + nq),
        out_shape=jax.ShapeDtypeStruct((b, n_steps * pg, 8, page), F32),
        grid_spec=pltpu.PrefetchScalarGridSpec(
            num_scalar_prefetch=1,
            grid=(b, n_steps),
            in_specs=[per_b((rows, di)), per_b((rows, 1))] + [pool_spec(i) for i in range(pg)] + [per_b((page, di))],
            out_specs=per_b((n_steps * pg, 8, page)),
            scratch_shapes=[pltpu.VMEM((n_steps * pg, 8, page), jnp.int32)],
        ),
        compiler_params=_params("parallel", "arbitrary"),
        name=name,
    )(page_table, qidx.reshape(b, rows, di).astype(BF16), wh.reshape(b, rows, 1), *([pool_kidx] * pg), kin)
    return mask


GDN_HALO = 8


def _dot(a, b, **kw):
    return jnp.dot(a, b, preferred_element_type=F32, **kw)


def _dot_nt(a, b, **kw):
    return lax.dot_general(a, b, (((1,), (1,)), ((), ())), preferred_element_type=F32, **kw)


def _gdn_body(q_ref, k_ref, v_ref, z_ref, ba_ref, cwq_ref, cwk_ref, cwv_ref, gate_ref, nw_ref, h0_ref, s0_ref,
              o_ref, sout_ref,
              xq_s, xk_s, xv_s, s_s, u_s, wq_s, qkkt_s, gt_s, *, blk, chunk, n_vh, hg):
    jg = pl.program_id(1)
    sb = pl.program_id(2)
    n_sb = pl.num_programs(2)
    nch = blk // chunk
    hd = GDN_HEAD
    exact = dict(precision=lax.Precision.HIGHEST)

    @pl.when(sb == 0)
    def _():
        xq_s[0:GDN_HALO, :] = h0_ref[:, 0:hg * hd]
        xk_s[0:GDN_HALO, :] = h0_ref[:, hg * hd:2 * hg * hd]
        xv_s[0:GDN_HALO, :] = h0_ref[:, 2 * hg * hd:]
        s_s[...] = s0_ref[...]

    xq_s[GDN_HALO:, :] = q_ref[...]
    xk_s[GDN_HALO:, :] = k_ref[...]
    xv_s[GDN_HALO:, :] = v_ref[...]

    row = lax.broadcasted_iota(jnp.int32, (chunk, chunk), 0)
    col = lax.broadcasted_iota(jnp.int32, (chunk, chunk), 1)
    lower = row >= col
    strict = row > col
    ltri = jnp.where(lower, 1.0, 0.0).astype(F32)
    lane = lax.broadcasted_iota(jnp.int32, (chunk, 128), 1)
    sel_row = lax.broadcasted_iota(jnp.int32, (8, 128), 0)
    sel_lane = lax.broadcasted_iota(jnp.int32, (8, 128), 1)
    hv0 = 2 * hg * jg
    pick = jnp.where((sel_row < 2 * hg) & (sel_lane == n_vh + hv0 + sel_row), 1.0, 0.0).astype(F32)
    neg_a = -jnp.exp(gate_ref[0:1, :])
    dt_bias = gate_ref[1:2, :]

    def conv_silu(xs_ref, w_ref, r0):
        x = xs_ref[pl.ds(r0, chunk + GDN_HALO), :]
        w = w_ref[...]
        taps = w.shape[0]
        acc = None
        for t in range(taps):
            sh = taps - 1 - t
            xt = x if sh == 0 else pltpu.roll(x, sh, axis=0)
            term = xt[GDN_HALO:, :] * w[t:t + 1, :]
            acc = term if acc is None else acc + term
        return jax.nn.silu(acc)

    def l2n(x):
        return x * lax.rsqrt(jnp.sum(x * x, -1, keepdims=True) + L2_EPS)

    def prep(c, carry):
        r0 = pl.multiple_of(c * chunk, chunk)
        q_all = conv_silu(xq_s, cwq_ref, r0)
        k_all = conv_silu(xk_s, cwk_ref, r0)
        v_all = conv_silu(xv_s, cwv_ref, r0)
        ba = ba_ref[pl.ds(r0, chunk), :]
        beta_all = jax.nn.sigmoid(ba)
        g_all = neg_a * jax.nn.softplus(ba + dt_bias)
        gc_all = _dot(ltri, g_all, **exact)
        gc_rows = _dot_nt(pick, gc_all, **exact)
        heads = range(2 * hg)
        qs = [l2n(q_all[:, h * hd:(h + 1) * hd]) * (hd ** -0.5) for h in range(hg)]
        ks = [l2n(k_all[:, h * hd:(h + 1) * hd]) for h in range(hg)]
        k16 = [k.astype(BF16) for k in ks]
        gram = [_dot_nt(k16[h], k16[h]) for h in range(hg)]
        qk_raw = [_dot_nt(qs[h].astype(BF16), k16[h]) for h in range(hg)]
        beta = [jnp.sum(jnp.where(lane == hv0 + e, beta_all, 0.0), -1, keepdims=True) for e in heads]
        gcol = [jnp.sum(jnp.where(lane == n_vh + hv0 + e, gc_all, 0.0), -1, keepdims=True) for e in heads]
        g_last = [g[chunk - 1:chunk, :] for g in gcol]
        decay = [jnp.where(lower, jnp.exp(jnp.where(lower, gcol[e] - gc_rows[e:e + 1, :], 0.0)), 0.0) for e in heads]
        nmat = [jnp.where(strict, -(beta[e] * gram[e // 2] * decay[e]), 0.0) for e in heads]
        cpow = [m.astype(BF16) for m in nmat]
        for _ in range(max(chunk.bit_length() - 2, 0)):
            cnew = [_dot(cp, cp) for cp in cpow]
            cpow = [cn.astype(BF16) for cn in cnew]
            corr = [_dot(nmat[e].astype(BF16), cpow[e]) for e in heads]
            nmat = [nmat[e] + cnew[e] + corr[e] for e in heads]
        eg = [jnp.exp(g) for g in gcol]
        rhs = [jnp.concatenate([v_all[:, e * hd:(e + 1) * hd] * beta[e], (ks[e // 2] * beta[e]) * eg[e]], axis=1)
               for e in heads]
        sol = [rhs[e] + _dot(nmat[e].astype(BF16), rhs[e].astype(BF16)) for e in heads]
        for e in heads:
            u_s[e, pl.ds(r0, chunk), :] = sol[e][:, :hd]
            wq_s[e, c, 0:chunk, :] = sol[e][:, hd:].astype(BF16)
            wq_s[e, c, chunk:2 * chunk, :] = (qs[e // 2] * eg[e]).astype(BF16)
            qkkt_s[e, c, 0:chunk, :] = jnp.where(lower, qk_raw[e // 2] * decay[e], 0.0).astype(BF16)
            k_tail = ks[e // 2] * jnp.exp(g_last[e] - gcol[e])
            qkkt_s[e, c, chunk:, :] = k_tail.T.astype(BF16)
            gt_s[e, c] = jnp.broadcast_to(jnp.exp(g_last[e]), (8, 128))
        return carry

    lax.fori_loop(0, nch, prep, 0)

    nw = nw_ref[...]

    def recur(c, carry):
        r0 = pl.multiple_of(c * chunk, chunk)
        heads = range(2 * hg)
        s = [s_s[e] for e in heads]
        ws = [_dot(wq_s[e, c], s[e].astype(BF16)) for e in heads]
        v_new = [u_s[e, pl.ds(r0, chunk), :] - ws[e][0:chunk] for e in heads]
        mix = [_dot(qkkt_s[e, c], v_new[e].astype(BF16)) for e in heads]
        for e in heads:
            s_s[e] = s[e] * gt_s[e, c][0:1, 0:1] + mix[e][chunk:]
            o = ws[e][chunk:] + mix[e][0:chunk]
            o = o * lax.rsqrt(jnp.mean(o * o, -1, keepdims=True) + RMS_EPS) * nw
            o = o * jax.nn.silu(z_ref[pl.ds(r0, chunk), e * hd:(e + 1) * hd])
            o_ref[pl.ds(r0, chunk), e * hd:(e + 1) * hd] = o.astype(o_ref.dtype)
        return carry

    lax.fori_loop(0, nch, recur, 0)

    xq_s[0:GDN_HALO, :] = xq_s[blk:blk + GDN_HALO, :]
    xk_s[0:GDN_HALO, :] = xk_s[blk:blk + GDN_HALO, :]
    xv_s[0:GDN_HALO, :] = xv_s[blk:blk + GDN_HALO, :]

    @pl.when(sb == n_sb - 1)
    def _():
        sout_ref[...] = s_s[...]


def gdn_mixer(qkv, z, ba, conv_hist, s0, conv_w, a_log, dt_bias, norm_w, layer, *, blk, hg, chunk=GDN_CHUNK,
              name="gdn"):
    b, n, conv_dim = qkv.shape
    hd = GDN_HEAD
    n_qk, n_vh = GDN_QK_HEADS, GDN_V_HEADS
    assert n_vh == 2 * n_qk and n % blk == 0 and blk % chunk == 0 and 2 * n_vh <= ba.shape[2]
    assert n_qk % hg == 0 and 2 * hg <= 8
    ng = n_qk // hg
    taps = conv_w.shape[1]
    hist = jnp.concatenate([jnp.zeros((b, GDN_HALO - (taps - 1), conv_dim), F32), conv_hist.astype(F32)], axis=1)
    hq = hist[:, :, :n_qk * hd].reshape(b, GDN_HALO, ng, hg * hd)
    hk = hist[:, :, n_qk * hd:2 * n_qk * hd].reshape(b, GDN_HALO, ng, hg * hd)
    hv = hist[:, :, 2 * n_qk * hd:].reshape(b, GDN_HALO, ng, 2 * hg * hd)
    h0 = jnp.concatenate([hq, hk, hv], axis=-1).transpose(0, 2, 1, 3)
    gate = jnp.zeros((8, ba.shape[2]), F32)
    gate = gate.at[0, n_vh:2 * n_vh].set(a_log[layer]).at[1, n_vh:2 * n_vh].set(dt_bias[layer])
    nw = norm_w.reshape(norm_w.shape[0], 1, hd)
    nch = blk // chunk
    wq, wv = hg * hd, 2 * hg * hd
    o, s_out = pl.pallas_call(
        functools.partial(_gdn_body, blk=blk, chunk=chunk, n_vh=n_vh, hg=hg),
        out_shape=(jax.ShapeDtypeStruct((b, n, n_vh * hd), BF16), jax.ShapeDtypeStruct(s0.shape, F32)),
        grid=(b, ng, n // blk),
        in_specs=[
            pl.BlockSpec((None, blk, wq), lambda bi, j, sb: (bi, sb, j)),
            pl.BlockSpec((None, blk, wq), lambda bi, j, sb: (bi, sb, ng + j)),
            pl.BlockSpec((None, blk, wv), lambda bi, j, sb: (bi, sb, ng + j)),
            pl.BlockSpec((None, blk, wv), lambda bi, j, sb: (bi, sb, j)),
            pl.BlockSpec((None, blk, ba.shape[2]), lambda bi, j, sb: (bi, sb, 0)),
            pl.BlockSpec((None, taps, wq), lambda bi, j, sb: (layer, 0, j)),
            pl.BlockSpec((None, taps, wq), lambda bi, j, sb: (layer, 0, ng + j)),
            pl.BlockSpec((None, taps, wv), lambda bi, j, sb: (layer, 0, ng + j)),
            pl.BlockSpec((8, ba.shape[2]), lambda bi, j, sb: (0, 0)),
            pl.BlockSpec((None, 1, hd), lambda bi, j, sb: (layer, 0, 0)),
            pl.BlockSpec((None, None, GDN_HALO, 4 * wq), lambda bi, j, sb: (bi, j, 0, 0)),
            pl.BlockSpec((None, 2 * hg, hd, hd), lambda bi, j, sb: (bi, j, 0, 0)),
        ],
        out_specs=(
            pl.BlockSpec((None, blk, wv), lambda bi, j, sb: (bi, sb, j)),
            pl.BlockSpec((None, 2 * hg, hd, hd), lambda bi, j, sb: (bi, j, 0, 0)),
        ),
        scratch_shapes=[
            pltpu.VMEM((blk + GDN_HALO, wq), F32), pltpu.VMEM((blk + GDN_HALO, wq), F32),
            pltpu.VMEM((blk + GDN_HALO, wv), F32),
            pltpu.VMEM((2 * hg, hd, hd), F32),
            pltpu.VMEM((2 * hg, blk, hd), F32),
            pltpu.VMEM((2 * hg, nch, 2 * chunk, hd), BF16),
            pltpu.VMEM((2 * hg, nch, chunk + hd, chunk), BF16),
            pltpu.VMEM((2 * hg, nch, 8, 128), F32),
        ],
        compiler_params=_params("parallel", "parallel", "arbitrary"),
        name=name,
    )(qkv, qkv, qkv, z, ba, conv_w, conv_w, conv_w, gate, nw, h0, s0)
    return o, s_out


def _l2norm(x):
    return x * lax.rsqrt(jnp.sum(x * x, -1, keepdims=True) + L2_EPS)


def _causal_dwconv(x_ext, w):
    width = w.shape[0]
    n = x_ext.shape[1] - width + 1
    return sum(x_ext[:, j:j + n] * w[j] for j in range(width))


def _chunk_gated_delta(q, k, v, g, beta, s0):
    b, n, h, dk = k.shape
    dv = v.shape[-1]
    c = min(GDN_CHUNK, n)
    pad = (-n) % c
    if pad:
        padf = lambda a: jnp.pad(a, [(0, 0), (0, pad)] + [(0, 0)] * (a.ndim - 2))
        q, k, v, g, beta = padf(q), padf(k), padf(v), padf(g), padf(beta)
    nc = (n + pad) // c
    qh, kh, vh = [a.transpose(0, 2, 1, 3).reshape(b, h, nc, c, a.shape[-1]) for a in (q, k, v)]
    gh, bh = [a.transpose(0, 2, 1).reshape(b, h, nc, c) for a in (g, beta)]
    gc = jnp.cumsum(gh, -1)
    diff = gc[..., :, None] - gc[..., None, :]
    lower = jnp.tril(jnp.ones((c, c), bool))
    strict = jnp.tril(jnp.ones((c, c), bool), -1)
    decay = jnp.where(lower, jnp.exp(jnp.where(lower, diff, 0.0)), 0.0)
    kb = kh * bh[..., None]
    a_mat = jnp.where(strict, jnp.einsum('bhnid,bhnjd->bhnij', kb, kh) * decay, 0.0)
    rhs = jnp.concatenate([vh * bh[..., None], kb * jnp.exp(gc)[..., None]], -1)
    sol = lax.linalg.triangular_solve(a_mat + jnp.eye(c, dtype=a_mat.dtype), rhs,
                                      left_side=True, lower=True, unit_diagonal=True)
    u, w = sol[..., :dv], sol[..., dv:]
    qk = jnp.where(lower, jnp.einsum('bhnid,bhnjd->bhnij', qh, kh) * decay, 0.0)
    q_dec = qh * jnp.exp(gc)[..., None]
    k_tail = kh * jnp.exp(gc[..., -1:] - gc)[..., None]
    g_tot = jnp.exp(gc[..., -1])

    def step(s, xs_n):
        u_n, w_n, qk_n, qd_n, kt_n, gt_n = xs_n
        v_new = u_n - jnp.einsum('bhcd,bhde->bhce', w_n, s)
        o = jnp.einsum('bhcd,bhde->bhce', qd_n, s) + jnp.einsum('bhij,bhje->bhie', qk_n, v_new)
        s = s * gt_n[..., None, None] + jnp.einsum('bhcd,bhce->bhde', kt_n, v_new)
        return s, o

    xs_all = tuple(jnp.moveaxis(a, 2, 0) for a in (u, w, qk, q_dec, k_tail, g_tot))
    s_fin, o = lax.scan(step, s0, xs_all)
    o = jnp.moveaxis(o, 0, 2).reshape(b, h, nc * c, dv)[:, :, :n].transpose(0, 2, 1, 3)
    return o, s_fin


def _gdn_core(qkv, z, bt, at, conv_buf, s0, conv_w, a_log, dt_bias, norm_w):
    b, n, _ = qkv.shape
    key_dim = GDN_QK_HEADS * GDN_HEAD
    ext = jnp.concatenate([conv_buf.astype(qkv.dtype), qkv], axis=1)
    new_buf = ext[:, -(conv_w.shape[0] - 1):]
    qkv = jax.nn.silu(_causal_dwconv(ext, conv_w))
    q, k, v = jnp.split(qkv, [key_dim, 2 * key_dim], axis=-1)
    rep = GDN_V_HEADS // GDN_QK_HEADS
    q = jnp.repeat(_l2norm(q.reshape(b, n, GDN_QK_HEADS, GDN_HEAD)), rep, axis=2) * (GDN_HEAD ** -0.5)
    k = jnp.repeat(_l2norm(k.reshape(b, n, GDN_QK_HEADS, GDN_HEAD)), rep, axis=2)
    v = v.reshape(b, n, GDN_V_HEADS, GDN_HEAD)
    beta = jax.nn.sigmoid(bt)
    g = -jnp.exp(a_log) * jax.nn.softplus(at + dt_bias)
    o, s_fin = _chunk_gated_delta(q, k, v, g, beta, s0)
    o = o * lax.rsqrt(jnp.mean(o * o, -1, keepdims=True) + RMS_EPS) * norm_w
    o = o * jax.nn.silu(z.reshape(b, n, GDN_V_HEADS, GDN_HEAD))
    return o.reshape(b, n, GDN_V_HEADS * GDN_HEAD), new_buf, s_fin


def _suffix_exclusive(logf):
    return lax.cumsum(logf, axis=1, reverse=True) - logf


def _paged_rows(pool, page_table):
    g = pool[page_table]
    return g.reshape((g.shape[0], g.shape[1] * g.shape[2]) + g.shape[3:])


def _take_rows(a, idx):
    return jax.vmap(lambda ab, ib: ab[ib])(a, idx)


def _fox_attention_small(q, k, v, r_q, r_k, q_pos, k_pos):
    d = q.shape[-1]
    rk = r_k.transpose(0, 2, 1)
    s = jnp.einsum('bqhd,bkhd->bhqk', q, k, preferred_element_type=F32) * (d ** -0.5)
    s = s + rk[:, :, None, :] - r_q.transpose(0, 2, 1)[:, :, :, None]
    s = jnp.where(k_pos[None, None, None, :] <= q_pos[None, None, :, None], s, -jnp.inf)
    p = jax.nn.softmax(s, axis=-1)
    return jnp.einsum('bhqk,bkhd->bqhd', p, v)


def _dsa_attention_small(q, qi, wh, q_pos, k_idx, n_keys, gather_kv):
    d = q.shape[-1]
    topk = min(TOPK_MAX, n_keys // 4)
    k_pos = jnp.arange(k_idx.shape[1])
    rel = jax.nn.relu(jnp.einsum('bqhd,bkd->bqhk', qi, k_idx, preferred_element_type=F32) * (IDX_DIM ** -0.5))
    score = jnp.einsum('bqhk,bqh->bqk', rel, wh)
    score = jnp.where(k_pos[None, None, :] <= q_pos[None, :, None], score, -jnp.inf)
    _, sel = lax.top_k(score, topk)
    valid = sel <= q_pos[None, :, None]
    k_sel, v_sel = gather_kv(sel)
    s = jnp.einsum('bqhd,bqkhd->bqhk', q, k_sel, preferred_element_type=F32) * (d ** -0.5)
    s = jnp.where(valid[:, :, None, :], s, -jnp.inf)
    p = jax.nn.softmax(s, axis=-1)
    return jnp.einsum('bqhk,bqkhd->bqhd', p, v_sel)


def _gather_paged_or_new(pool_k, pool_v, page_table, k_new, v_new, sel):
    n_past = page_table.shape[1] * PAGE
    nb = sel.shape[0]
    in_past = (sel < n_past)[..., None, None]
    sp = jnp.minimum(sel, n_past - 1)
    phys = jnp.take_along_axis(page_table, (sp // PAGE).reshape(nb, -1), axis=1).reshape(sel.shape)
    off = sp % PAGE
    sn = jnp.clip(sel - n_past, 0, k_new.shape[1] - 1)
    k_sel = jnp.where(in_past, pool_k[phys, off], _take_rows(k_new, sn))
    v_sel = jnp.where(in_past, pool_v[phys, off], _take_rows(v_new, sn))
    return k_sel, v_sel


def _tile_n(n, col0, cap):
    for t in (1024, 512, 256, 128):
        if t <= cap and n % t == 0 and col0 % t == 0:
            return t
    raise ValueError((n, col0))


def kernel(x_prompt, x_sample, cache_fox_k, cache_fox_v, cache_fox_logf, cache_dsa_k, cache_dsa_v, cache_dsa_kidx, state_gdn, state_gdn_conv, state_ffn_conv, page_table, p_prompt, p_sample, gdn_w_in, gdn_conv_w, gdn_a_log, gdn_dt_bias, gdn_norm_w, gdn_w_out, fox_w_in, fox_b_f, fox_w_out, dsa_w_in, dsa_idx_ln_g, dsa_idx_ln_b, dsa_w_out, ffn_w_up, ffn_conv_w, ffn_conv_b, ffn_w_down, ln_mix_g, ln_mix_b, ln_ffn_g, ln_ffn_b, ple_w_proj, ple_w_gate):
    b, n, d = x_prompt.shape
    db, ns, _ = x_sample.shape
    depth = ffn_w_up.shape[0]
    n_past = page_table.shape[1] * PAGE
    alpha = (2.0 * depth) ** 0.25
    f2 = ffn_w_up.shape[2]
    key_dim = GDN_QK_HEADS * GDN_HEAD
    val_dim = GDN_V_HEADS * GDN_HEAD
    conv_dim = 2 * key_dim + val_dim

    groups = {
        "p": dict(nb=b, n=n, tm=1024, tn_cap=512, xf=x_prompt.reshape(b * n, d)),
        "s": dict(nb=db, n=ns, tm=db * ns, tn_cap=1024, xf=x_sample.reshape(db * ns, d)),
    }
    for gr in groups.values():
        gr["xb"] = gr["xf"].astype(BF16)

    def proj(gr, w, layer, col0, ncols, out_dtype=F32, **kw):
        return matmul(gr["xb"], w, layer, col0=col0, n=ncols, tm=gr["tm"], tn=_tile_n(ncols, col0, gr["tn_cap"]),
                      out_dtype=out_dtype, **kw)

    def tail_proj(gr, w, layer, col0, padded):
        wt = jnp.pad(w[layer, :, col0:], ((0, 0), (0, padded - (w.shape[2] - col0))))[None]
        return matmul(gr["xb"], wt, 0, tm=gr["tm"], tn=padded, name="mm_tail")

    def out_proj(gr, o_b, w, layer):
        kdim = o_b.shape[1]
        return matmul(o_b, w, layer, tm=gr["tm"], tn=_tile_n(d, 0, gr["tn_cap"]), tk=min(kdim, 2048),
                      res=gr["xf"], alpha=alpha, name="out_proj")

    outs = {k: [] for k in ("gdn_s_p", "gdn_c_p", "gdn_s_s", "gdn_c_s", "fox_k_p", "fox_v_p", "fox_lf_p",
                            "fox_k_s", "fox_v_s", "fox_lf_s", "dsa_k_p", "dsa_v_p", "dsa_ki_p",
                            "dsa_k_s", "dsa_v_s", "dsa_ki_s", "ffn_c_p", "ffn_c_s")}
    pos_sq = n_past + jnp.arange(ns)
    pos_sk = jnp.arange(n_past + ns)

    for i in range(depth):
        kind, j = i % 3, i // 3
        for tag, gr in groups.items():
            nb, nn = gr["nb"], gr["n"]
            if kind == 0:
                qkv = proj(gr, gdn_w_in, j, 0, conv_dim).reshape(nb, nn, conv_dim)
                z = proj(gr, gdn_w_in, j, conv_dim, val_dim).reshape(nb, nn, val_dim)
                ba = tail_proj(gr, gdn_w_in, j, conv_dim + val_dim, 128).reshape(nb, nn, 128)
                bt, at = ba[..., :GDN_V_HEADS], ba[..., GDN_V_HEADS:2 * GDN_V_HEADS]
                if tag == "p":
                    conv_buf = jnp.zeros((nb, gdn_conv_w.shape[1] - 1, conv_dim), F32)
                    s0 = jnp.zeros((nb, GDN_V_HEADS, GDN_HEAD, GDN_HEAD), F32)
                    o, s_new = gdn_mixer(qkv, z, ba, conv_buf, s0, gdn_conv_w, gdn_a_log, gdn_dt_bias, gdn_norm_w, j,
                                         blk=512, hg=4)
                    c_new = qkv[:, nn - (gdn_conv_w.shape[1] - 1):]
                else:
                    conv_buf, s0 = state_gdn_conv[j], state_gdn[j]
                    o, c_new, s_new = _gdn_core(qkv, z, bt, at, conv_buf, s0, gdn_conv_w[j], gdn_a_log[j],
                                                gdn_dt_bias[j], gdn_norm_w[j])
                    o = o.astype(BF16)
                outs["gdn_s_" + tag].append(s_new)
                outs["gdn_c_" + tag].append(c_new)
                r = out_proj(gr, o.reshape(nb * nn, val_dim), gdn_w_out, j)
            elif kind == 1:
                k = proj(gr, fox_w_in, j, d, d)
                v = proj(gr, fox_w_in, j, 2 * d, d)
                f = tail_proj(gr, fox_w_in, j, 3 * d, 128)[:, :ATT_HEADS].reshape(nb, nn, ATT_HEADS)
                lf = jax.nn.log_sigmoid(f + fox_b_f[j])
                k4, v4 = k.reshape(nb, nn, ATT_HEADS, HEAD_DIM), v.reshape(nb, nn, ATT_HEADS, HEAD_DIM)
                outs["fox_k_" + tag].append(k4)
                outs["fox_v_" + tag].append(v4)
                outs["fox_lf_" + tag].append(lf)
                if tag == "p":
                    q = proj(gr, fox_w_in, j, 0, d, out_dtype=BF16)
                    o = flash_attention(q.reshape(nb, nn, d), k.reshape(nb, nn, d), v.reshape(nb, nn, d),
                                        r=_suffix_exclusive(lf), tq=512, tk=512, name="fox_flash")
                    o = o.reshape(nb * nn, d)
                else:
                    q = proj(gr, fox_w_in, j, 0, d).reshape(nb, nn, ATT_HEADS, HEAD_DIM)
                    lf_all = jnp.concatenate([_paged_rows(cache_fox_logf[j], page_table), lf], axis=1)
                    rr = _suffix_exclusive(lf_all)
                    o = paged_attention(q, cache_fox_k, cache_fox_v, j, page_table, k4, v4, pg=2,
                                        r_q=rr[:, n_past:], r_k=rr, name="fox_decode")
                    o = o.reshape(nb * nn, d).astype(BF16)
                r = out_proj(gr, o, fox_w_out, j)
            else:
                k = proj(gr, dsa_w_in, j, d, d)
                v = proj(gr, dsa_w_in, j, 2 * d, d)
                tail = tail_proj(gr, dsa_w_in, j, 3 * d + IDX_HEADS * IDX_DIM, 256)
                ki, _ = layer_norm(tail[:, :IDX_DIM], dsa_idx_ln_g, dsa_idx_ln_b, j, tm=min(gr["tm"], 512),
                                   name="dsa_ki_ln")
                wh = tail[:, IDX_DIM:IDX_DIM + IDX_HEADS] * (IDX_HEADS ** -0.5)
                k4, v4 = k.reshape(nb, nn, ATT_HEADS, HEAD_DIM), v.reshape(nb, nn, ATT_HEADS, HEAD_DIM)
                ki3 = ki.reshape(nb, nn, IDX_DIM)
                outs["dsa_k_" + tag].append(k4)
                outs["dsa_v_" + tag].append(v4)
                outs["dsa_ki_" + tag].append(ki3)
                if tag == "p":
                    q = proj(gr, dsa_w_in, j, 0, d, out_dtype=BF16)
                    qidx = proj(gr, dsa_w_in, j, 3 * d, IDX_HEADS * IDX_DIM, out_dtype=BF16)
                    mask = dsa_select(qidx.reshape(nb, nn, IDX_HEADS * IDX_DIM), ki3, wh.reshape(nb, nn, IDX_HEADS),
                                      tq=256, tk=512, topk=min(TOPK_MAX, nn // 4))
                    o = flash_attention(q.reshape(nb, nn, d), k.reshape(nb, nn, d), v.reshape(nb, nn, d),
                                        mask=mask, tq=512, tk=512, name="dsa_flash")
                    o = o.reshape(nb * nn, d)
                else:
                    q = proj(gr, dsa_w_in, j, 0, d).reshape(nb, nn, ATT_HEADS, HEAD_DIM)
                    qidx = proj(gr, dsa_w_in, j, 3 * d, IDX_HEADS * IDX_DIM).reshape(nb, nn, IDX_HEADS, IDX_DIM)
                    sel = paged_select(qidx, wh.reshape(nb, nn, IDX_HEADS), cache_dsa_kidx, j, page_table, ki3,
                                       topk=min(TOPK_MAX, (n_past + nn) // 4), pg=8, name="dsa_decode_select")
                    o = paged_attention(q, cache_dsa_k, cache_dsa_v, j, page_table, k4, v4, pg=2, mask=sel,
                                        name="dsa_decode")
                    o = o.reshape(nb * nn, d).astype(BF16)
                r = out_proj(gr, o, dsa_w_out, j)

            ln_tm = min(gr["tm"], 512)
            gr["xf"], gr["xb"] = layer_norm(r, ln_mix_g, ln_mix_b, i, tm=ln_tm, name="ln_mix")

            if tag == "p":
                act, c_new = ffn_up(gr["xb"], jnp.zeros((nb, 2, f2), F32), ffn_w_up, ffn_conv_w, ffn_conv_b, i,
                                    seq_len=nn, tm=gr["tm"], tn=512)
            else:
                hcur = proj(gr, ffn_w_up, i, 0, f2).reshape(nb, nn, f2)
                ext = jnp.concatenate([state_ffn_conv[i], hcur], axis=1)
                c_new = ext[:, -(ffn_conv_w.shape[1] - 1):]
                hc = _causal_dwconv(ext, ffn_conv_w[i]) + ffn_conv_b[i]
                gate, val = jnp.split(hc, [f2 // 2], axis=-1)
                act = (jax.nn.silu(gate) * val).reshape(nb * nn, f2 // 2).astype(BF16)
            outs["ffn_c_" + tag].append(c_new)
            r = matmul(act, ffn_w_down, i, tm=gr["tm"], tn=_tile_n(d, 0, gr["tn_cap"]), tk=f2 // 4,
                       res=gr["xf"], alpha=alpha, name="ffn_down")
            x2f, x2b = layer_norm(r, ln_ffn_g, ln_ffn_b, i, tm=ln_tm, name="ln_ffn")
            p_in = (p_prompt if tag == "p" else p_sample)[i].reshape(nb * nn, -1).astype(BF16)
            gr["xf"], gr["xb"] = per_layer_embed(x2f, x2b, p_in, ple_w_proj, ple_w_gate, i, tm=gr["tm"],
                                                 tn=_tile_n(d, 0, gr["tn_cap"]))

    st = lambda key: jnp.stack(outs[key])
    return (groups["p"]["xf"].reshape(b, n, d), groups["s"]["xf"].reshape(db, ns, d),
            st("gdn_s_p"), st("gdn_c_p"), st("fox_k_p"), st("fox_v_p"), st("fox_lf_p"),
            st("dsa_k_p"), st("dsa_v_p"), st("dsa_ki_p"), st("ffn_c_p"),
            st("gdn_s_s"), st("gdn_c_s"), st("fox_k_s"), st("fox_v_s"), st("fox_lf_s"),
            st("dsa_k_s"), st("dsa_v_s"), st("dsa_ki_s"), st("ffn_c_s"))
```

```python
import functools
import math

import jax
import jax.numpy as jnp
from jax import lax
from jax.experimental import pallas as pl
from jax.experimental.pallas import tpu as pltpu

F32 = jnp.float32
BF16 = jnp.bfloat16

PAGE = 128
GDN_QK_HEADS = 16
GDN_V_HEADS = 32
GDN_HEAD = 128
GDN_CHUNK = 64
ATT_HEADS = 16
HEAD_DIM = 128
IDX_HEADS = 16
IDX_DIM = 128
TOPK_MAX = 256
LN_EPS = 1e-5
RMS_EPS = 1e-6
L2_EPS = 1e-6

V7X_VMEM_LIMIT_BYTES = 56 * 1024 * 1024
NEG_BIG = -1e30
INT_MIN = -(2 ** 31)


def _params(*sem):
    return pltpu.CompilerParams(dimension_semantics=sem, vmem_limit_bytes=V7X_VMEM_LIMIT_BYTES)


def _mm_body(*refs, nk, has_res, alpha):
    if has_res:
        x_ref, w_ref, r_ref, o_ref, *scratch = refs
    else:
        x_ref, w_ref, o_ref, *scratch = refs
        r_ref = None

    def finish(acc):
        if has_res:
            acc = alpha * r_ref[...] + acc
        o_ref[...] = acc.astype(o_ref.dtype)

    part = jnp.dot(x_ref[...], w_ref[...].astype(BF16), preferred_element_type=F32)
    if nk == 1:
        finish(part)
        return
    acc_ref, = scratch
    k = pl.program_id(2)

    @pl.when(k == 0)
    def _():
        acc_ref[...] = part

    @pl.when(k > 0)
    def _():
        acc_ref[...] += part

    @pl.when(k == nk - 1)
    def _():
        finish(acc_ref[...])


def matmul(x, w, layer, *, col0=0, n=None, tm, tn, tk=None, out_dtype=F32, res=None, alpha=None, name="mm"):
    m, kdim = x.shape
    assert x.dtype == BF16 and w.shape[1] == kdim
    n = w.shape[2] - col0 if n is None else n
    tk = kdim if tk is None else tk
    assert m % tm == 0 and n % tn == 0 and kdim % tk == 0 and col0 % tn == 0
    nk = kdim // tk
    cb0 = col0 // tn
    in_specs = [
        pl.BlockSpec((tm, tk), lambda i, j, k: (i, k)),
        pl.BlockSpec((None, tk, tn), lambda i, j, k: (layer, k, cb0 + j)),
    ]
    args = [x, w]
    if res is not None:
        assert res.shape == (m, n)
        in_specs.append(pl.BlockSpec((tm, tn), lambda i, j, k: (i, j)))
        args.append(res)
    return pl.pallas_call(
        functools.partial(_mm_body, nk=nk, has_res=res is not None, alpha=alpha),
        out_shape=jax.ShapeDtypeStruct((m, n), out_dtype),
        grid=(m // tm, n // tn, nk),
        in_specs=in_specs,
        out_specs=pl.BlockSpec((tm, tn), lambda i, j, k: (i, j)),
        scratch_shapes=[pltpu.VMEM((tm, tn), F32)] if nk > 1 else [],
        compiler_params=_params("parallel", "parallel", "arbitrary"),
        name=name,
    )(*args)


def _ln_body(x_ref, g_ref, b_ref, of_ref, ob_ref, *, eps):
    x = x_ref[...]
    mu = jnp.mean(x, -1, keepdims=True)
    xc = x - mu
    var = jnp.mean(xc * xc, -1, keepdims=True)
    y = xc * lax.rsqrt(var + eps) * g_ref[...] + b_ref[...]
    of_ref[...] = y
    ob_ref[...] = y.astype(BF16)


def layer_norm(x, g, b, layer, *, tm, eps=LN_EPS, name="ln"):
    m, n = x.shape
    assert m % tm == 0
    g3 = g.reshape(g.shape[0], 1, n)
    b3 = b.reshape(b.shape[0], 1, n)
    vec = pl.BlockSpec((None, 1, n), lambda i: (layer, 0, 0))
    row = pl.BlockSpec((tm, n), lambda i: (i, 0))
    return pl.pallas_call(
        functools.partial(_ln_body, eps=eps),
        out_shape=(jax.ShapeDtypeStruct((m, n), F32), jax.ShapeDtypeStruct((m, n), BF16)),
        grid=(m // tm,),
        in_specs=[row, vec, vec],
        out_specs=(row, row),
        compiler_params=_params("parallel"),
        name=name,
    )(x, g3, b3)


def _ple_body(xb_ref, pb_ref, wg_ref, wp_ref, xr_ref, of_ref, ob_ref):
    gate = jnp.dot(xb_ref[...], wg_ref[...].astype(BF16), preferred_element_type=F32)
    proj = jnp.dot(pb_ref[...], wp_ref[...].astype(BF16), preferred_element_type=F32)
    y = xr_ref[...] + proj * jax.nn.sigmoid(gate)
    of_ref[...] = y
    ob_ref[...] = y.astype(BF16)


def per_layer_embed(x_f, x_b, p_b, w_proj, w_gate, layer, *, tm, tn, name="ple"):
    m, d = x_f.shape
    pd = p_b.shape[1]
    assert m % tm == 0 and d % tn == 0
    out = pl.BlockSpec((tm, tn), lambda i, j: (i, j))
    return pl.pallas_call(
        _ple_body,
        out_shape=(jax.ShapeDtypeStruct((m, d), F32), jax.ShapeDtypeStruct((m, d), BF16)),
        grid=(m // tm, d // tn),
        in_specs=[
            pl.BlockSpec((tm, d), lambda i, j: (i, 0)),
            pl.BlockSpec((tm, pd), lambda i, j: (i, 0)),
            pl.BlockSpec((None, d, tn), lambda i, j: (layer, 0, j)),
            pl.BlockSpec((None, pd, tn), lambda i, j: (layer, 0, j)),
            out,
        ],
        out_specs=(out, out),
        compiler_params=_params("parallel", "parallel"),
        name=name,
    )(x_b, p_b, w_gate, w_proj, x_f)


HALO_ROWS = 16


def _ffn_up_body(x_ref, xh_ref, wg_ref, wv_ref, cwg_ref, cwv_ref, cbg_ref, cbv_ref, hg_ref, hv_ref,
                 act_ref, tg_ref, tv_ref, hs_ref, *, tiles_per_seq, tm):
    first = (pl.program_id(0) % tiles_per_seq) == 0
    x = x_ref[...]
    xh = xh_ref[...]
    conv = []
    for w_ref, cw_ref, cb_ref, hist_ref, tail_ref in ((wg_ref, cwg_ref, cbg_ref, hg_ref, tg_ref),
                                                     (wv_ref, cwv_ref, cbv_ref, hv_ref, tv_ref)):
        w = w_ref[...].astype(BF16)
        h = jnp.dot(x, w, preferred_element_type=F32)
        halo = jnp.dot(xh, w, preferred_element_type=F32)
        hs_ref[0:8, :] = jnp.where(first, hist_ref[...], halo[HALO_ROWS - 8:, :])
        hs_ref[8:8 + tm, :] = h
        cw = cw_ref[...]
        c = hs_ref[6:6 + tm, :] * cw[0:1, :] + hs_ref[7:7 + tm, :] * cw[1:2, :] + h * cw[2:3, :] + cb_ref[...]
        conv.append(c)
        tail_ref[...] = hs_ref[8 + tm - 2:8 + tm, :]
    act_ref[...] = (jax.nn.silu(conv[0]) * conv[1]).astype(act_ref.dtype)


def ffn_up(x_b, hist, w_up, conv_w, conv_b, layer, *, seq_len, tm, tn, name="ffn_up"):
    m, d = x_b.shape
    f2 = w_up.shape[2]
    f = f2 // 2
    nb = m // seq_len
    assert seq_len % tm == 0 and f % tn == 0 and tm % HALO_ROWS == 0
    tiles_per_seq = seq_len // tm
    nj = f // tn
    hist8 = jnp.concatenate([jnp.zeros((nb, 6, f2), F32), hist.astype(F32)], axis=1)
    cw = conv_w
    cb = conv_b.reshape(conv_b.shape[0], 1, f2)
    halo_blocks = tm // HALO_ROWS

    def wspec(off):
        return pl.BlockSpec((None, d, tn), lambda i, j: (layer, 0, off + j))

    def cwspec(off):
        return pl.BlockSpec((None, cw.shape[1], tn), lambda i, j: (layer, 0, off + j))

    def cbspec(off):
        return pl.BlockSpec((None, 1, tn), lambda i, j: (layer, 0, off + j))

    def hspec(off):
        return pl.BlockSpec((None, 8, tn), lambda i, j: (i // tiles_per_seq, 0, off + j))

    tail = pl.BlockSpec((None, 2, tn), lambda i, j: (i, 0, j))
    act, tail_g, tail_v = pl.pallas_call(
        functools.partial(_ffn_up_body, tiles_per_seq=tiles_per_seq, tm=tm),
        out_shape=(jax.ShapeDtypeStruct((m, f), BF16),
                   jax.ShapeDtypeStruct((m // tm, 2, f), F32), jax.ShapeDtypeStruct((m // tm, 2, f), F32)),
        grid=(m // tm, nj),
        in_specs=[
            pl.BlockSpec((tm, d), lambda i, j: (i, 0)),
            pl.BlockSpec((HALO_ROWS, d), lambda i, j: (jnp.maximum(i * halo_blocks - 1, 0), 0)),
            wspec(0), wspec(nj), cwspec(0), cwspec(nj), cbspec(0), cbspec(nj), hspec(0), hspec(nj),
        ],
        out_specs=(pl.BlockSpec((tm, tn), lambda i, j: (i, j)), tail, tail),
        scratch_shapes=[pltpu.VMEM((tm + 8, tn), F32)],
        compiler_params=_params("arbitrary", "arbitrary"),
        name=name,
    )(x_b, x_b, w_up, w_up, cw, cw, cb, cb, hist8, hist8)
    last = slice(tiles_per_seq - 1, None, tiles_per_seq)
    return act, jnp.concatenate([tail_g[last], tail_v[last]], axis=-1)


def _flash_body(*refs, tq, tk, scale, has_bias, has_mask):
    refs = list(refs)
    q_ref, k_ref, v_ref = refs[:3]
    rest = refs[3:]
    if has_bias:
        rq_ref, rk_ref = rest[:2]
        rest = rest[2:]
    if has_mask:
        mask_ref = rest[0]
        rest = rest[1:]
    if has_bias:
        o_ref, m_ref, l_ref, acc_ref, rqb_ref = rest
    else:
        o_ref, m_ref, l_ref, acc_ref = rest
    qi = pl.program_id(2)
    lanes = m_ref.shape[1]
    nblk = tk // lanes
    m_ref[...] = jnp.full(m_ref.shape, NEG_BIG, F32)
    l_ref[...] = jnp.zeros(l_ref.shape, F32)
    acc_ref[...] = jnp.zeros(acc_ref.shape, F32)
    if has_bias:
        rqb_ref[...] = jnp.broadcast_to(rq_ref[...], rqb_ref.shape)
    tri = lax.broadcasted_iota(jnp.int32, (tq, lanes), 1) - lax.broadcasted_iota(jnp.int32, (tq, lanes), 0)

    def chunk(c, diagonal):
        off = pl.multiple_of(c * tk, tk)
        kc = k_ref[pl.ds(off, tk), :].astype(BF16)
        vc = v_ref[pl.ds(off, tk), :].astype(BF16)
        s = lax.dot_general(q_ref[...], kc, (((1,), (1,)), ((), ())), preferred_element_type=F32) * scale
        blocks = [s[:, j * lanes:(j + 1) * lanes] for j in range(nblk)]
        if has_bias:
            rk = rk_ref[c]
            rqb = rqb_ref[...]
            blocks = [blocks[j] + rk[:, j * lanes:(j + 1) * lanes] - rqb for j in range(nblk)]
        if has_mask:
            keep = mask_ref[c].astype(jnp.int32)
            blocks = [jnp.where(keep[:, j * lanes:(j + 1) * lanes] != 0, blocks[j], NEG_BIG) for j in range(nblk)]
        elif diagonal:
            blocks = [jnp.where(tri + j * lanes <= 0, blocks[j], NEG_BIG) for j in range(nblk)]
        bmax = blocks[0]
        for blk in blocks[1:]:
            bmax = jnp.maximum(bmax, blk)
        m_old = m_ref[...]
        m_new = jnp.maximum(m_old, jnp.max(bmax, -1, keepdims=True))
        a = jnp.exp(m_old - m_new)
        ps = [jnp.exp(blk - m_new) for blk in blocks]
        psum = ps[0]
        for pj in ps[1:]:
            psum = psum + pj
        l_ref[...] = a * l_ref[...] + psum
        p = jnp.concatenate(ps, axis=1).astype(BF16)
        acc_ref[...] = a * acc_ref[...] + jnp.dot(p, vc, preferred_element_type=F32)
        m_ref[...] = m_new

    def full_chunk(c, carry):
        chunk(c, False)
        return carry

    if has_mask:
        lax.fori_loop(0, qi + 1, full_chunk, 0)
    else:
        lax.fori_loop(0, qi, full_chunk, 0)
        chunk(qi, True)
    o_ref[...] = (acc_ref[...] / jnp.sum(l_ref[...], -1, keepdims=True)).astype(o_ref.dtype)


def flash_attention(q, k, v, *, r=None, mask=None, tq, tk, name="flash"):
    b, n, hd = q.shape
    h = hd // HEAD_DIM
    assert n % tq == 0 and tq == tk and tk % HEAD_DIM == 0
    nkc = n // tk
    in_specs = [
        pl.BlockSpec((None, tq, HEAD_DIM), lambda bi, hi, qi: (bi, qi, hi)),
        pl.BlockSpec((None, n, HEAD_DIM), lambda bi, hi, qi: (bi, 0, hi)),
        pl.BlockSpec((None, n, HEAD_DIM), lambda bi, hi, qi: (bi, 0, hi)),
    ]
    args = [q, k, v]
    if r is not None:
        rt = r.transpose(0, 2, 1)
        in_specs.append(pl.BlockSpec((None, None, tq, 1), lambda bi, hi, qi: (bi, hi, qi, 0)))
        in_specs.append(pl.BlockSpec((None, None, nkc, 1, tk), lambda bi, hi, qi: (bi, hi, 0, 0, 0)))
        args += [rt.reshape(b, h, n, 1), rt.reshape(b, h, nkc, 1, tk)]
    if mask is not None:
        in_specs.append(pl.BlockSpec((None, nkc, tq, tk), lambda bi, hi, qi: (bi, 0, qi, 0)))
        args.append(mask)
    return pl.pallas_call(
        functools.partial(_flash_body, tq=tq, tk=tk, scale=HEAD_DIM ** -0.5,
                          has_bias=r is not None, has_mask=mask is not None),
        out_shape=jax.ShapeDtypeStruct((b, n, hd), BF16),
        grid=(b, h, n // tq),
        in_specs=in_specs,
        out_specs=pl.BlockSpec((None, tq, HEAD_DIM), lambda bi, hi, qi: (bi, qi, hi)),
        scratch_shapes=[pltpu.VMEM((tq, HEAD_DIM), F32)] * (4 if r is not None else 3),
        compiler_params=_params("parallel", "parallel", "arbitrary"),
        name=name,
    )(*args)


def _dsa_select_body(q_ref, kidx_ref, wh_ref, mask_ref, keys_ref, *, tq, tk, topk, n_keys):
    qi = pl.program_id(1)
    n_valid = ((qi + 1) * tq + tk - 1) // tk
    whs = wh_ref[...] * (IDX_DIM ** -0.5)
    q_pos = qi * tq + lax.broadcasted_iota(jnp.int32, (tq, tk), 0)
    k_iota = lax.broadcasted_iota(jnp.int32, (tq, tk), 1)
    n_heads = wh_ref.shape[-1]
    lanes = 128

    def score_chunk(c, carry):
        off = pl.multiple_of(c * tk, tk)
        kc = kidx_ref[pl.ds(off, tk), :].astype(BF16)
        acc = jnp.zeros((tq, tk), F32)
        for h in range(n_heads):
            d = lax.dot_general(q_ref[:, h * IDX_DIM:(h + 1) * IDX_DIM], kc, (((1,), (1,)), ((), ())),
                                preferred_element_type=F32)
            acc = acc + jnp.maximum(d, 0.0) * whs[:, h:h + 1]
        bits = pltpu.bitcast(acc, jnp.int32)
        key = bits ^ ((bits >> 31) & 0x7FFFFFFF)
        keys_ref[c] = jnp.where((off + k_iota) <= q_pos, key, INT_MIN)
        return carry

    lax.fori_loop(0, n_valid, score_chunk, 0)

    def count(pred):
        def body(c, part):
            ind = jnp.where(pred(keys_ref[c], c * tk), 1.0, 0.0)
            for s in range(tk // lanes):
                part = part + ind[:, s * lanes:(s + 1) * lanes]
            return part
        part = lax.fori_loop(0, n_valid, body, jnp.zeros((tq, lanes), F32))
        return jnp.sum(part, -1, keepdims=True)

    kf = float(topk)

    def value_bit(it, t):
        cand = t + lax.shift_left(jnp.int32(1), 31 - it)
        cnt = count(lambda kk, off: kk >= cand)
        return jnp.where(cnt >= kf, cand, t)

    thr = lax.fori_loop(0, 32, value_bit, jnp.full((tq, 1), INT_MIN, jnp.int32))
    cnt_ge = count(lambda kk, off: kk >= thr)
    cnt_gt = count(lambda kk, off: kk > thr)
    need = kf - cnt_gt
    idx_bits = int(n_keys).bit_length()

    def tie_search(_):
        def index_bit(it, p):
            cand = p + lax.shift_left(jnp.int32(1), idx_bits - 1 - it)
            cnt = count(lambda kk, off: (kk == thr) & ((off + k_iota) < cand))
            return jnp.where(cnt < need, cand, p)
        return lax.fori_loop(0, idx_bits, index_bit, jnp.zeros((tq, 1), jnp.int32))

    excess = jnp.max(cnt_ge - kf) > 0.0
    last_tie = lax.cond(excess, tie_search, lambda _: jnp.full((tq, 1), n_keys, jnp.int32), 0)

    mask_ref[...] = jnp.zeros(mask_ref.shape, mask_ref.dtype)

    def emit(c, carry):
        off = c * tk
        kk = keys_ref[c]
        k_pos = off + k_iota
        sel = (kk > thr) | ((kk == thr) & (k_pos <= last_tie))
        sel = sel & (k_pos <= q_pos)
        mask_ref[c] = jnp.where(sel, 1, 0).astype(mask_ref.dtype)
        return carry

    lax.fori_loop(0, n_valid, emit, 0)


def dsa_select(qidx, kidx, wh, *, tq, tk, topk, name="dsa_select"):
    b, n, _ = qidx.shape
    assert n % tq == 0 and n % tk == 0
    nkc = n // tk
    return pl.pallas_call(
        functools.partial(_dsa_select_body, tq=tq, tk=tk, topk=topk, n_keys=n),
        out_shape=jax.ShapeDtypeStruct((b, nkc, n, tk), jnp.int8),
        grid=(b, n // tq),
        in_specs=[
            pl.BlockSpec((None, tq, qidx.shape[2]), lambda bi, qi: (bi, qi, 0)),
            pl.BlockSpec((None, n, kidx.shape[2]), lambda bi, qi: (bi, 0, 0)),
            pl.BlockSpec((None, tq, wh.shape[2]), lambda bi, qi: (bi, qi, 0)),
        ],
        out_specs=pl.BlockSpec((None, nkc, tq, tk), lambda bi, qi: (bi, 0, qi, 0)),
        scratch_shapes=[pltpu.VMEM((nkc, tq, tk), jnp.int32)],
        compiler_params=_params("parallel", "arbitrary"),
        name=name,
    )(qidx, kidx, wh)


def _decode_attn_body(*refs, n_pages, nq, n_heads, scale, has_bias, has_mask):
    refs = list(refs)
    pt_ref, q_ref, k_ref, v_ref, kn_ref, vn_ref = refs[:6]
    rest = refs[6:]
    if has_bias:
        rq_ref, rk_ref = rest[:2]
        rest = rest[2:]
    if has_mask:
        mask_ref = rest[0]
        rest = rest[1:]
    o_ref, m_s, l_s, acc_s = rest
    p = pl.program_id(1)
    rows = nq * n_heads
    page = k_ref.shape[0]

    @pl.when(p == 0)
    def _():
        m_s[...] = jnp.full(m_s.shape, NEG_BIG, F32)
        l_s[...] = jnp.zeros(l_s.shape, F32)
        acc_s[...] = jnp.zeros(acc_s.shape, F32)

    def attend(k_src, v_src, new_tokens):
        kc = k_src[...].astype(BF16)
        vc = v_src[...].astype(BF16)
        s = _dot_nt(q_ref[...], kc) * scale
        if has_bias:
            s = s + jnp.concatenate([rk_ref[...]] * nq, axis=0) - rq_ref[...]
        ok = None
        if has_mask:
            mk = mask_ref[...]
            ok = jnp.concatenate([jnp.broadcast_to(mk[i:i + 1, :], (n_heads, page)) for i in range(nq)], axis=0) > 0.0
        elif new_tokens:
            q_idx = lax.broadcasted_iota(jnp.int32, (rows, page), 0) // n_heads
            t_idx = lax.broadcasted_iota(jnp.int32, (rows, page), 1)
            ok = (t_idx <= q_idx) & (t_idx < nq)
        if ok is not None:
            s = jnp.where(ok, s, NEG_BIG)
        m_old = m_s[...]
        m_new = jnp.maximum(m_old, jnp.max(s, -1, keepdims=True))
        pr = jnp.exp(s - m_new)
        a = jnp.exp(m_old - m_new)
        l_s[...] = a * l_s[...] + jnp.sum(pr, -1, keepdims=True)
        acc_s[...] = a * acc_s[...] + _dot(pr.astype(BF16), vc)
        m_s[...] = m_new

    @pl.when(p < n_pages)
    def _():
        attend(k_ref, v_ref, False)

    @pl.when(p == n_pages)
    def _():
        attend(kn_ref, vn_ref, True)
        hd = acc_s.shape[1] // n_heads
        o = acc_s[...] / l_s[...]
        r_head = lax.broadcasted_iota(jnp.int32, o.shape, 0) % n_heads
        c_head = lax.broadcasted_iota(jnp.int32, o.shape, 1) // hd
        o = jnp.where(r_head == c_head, o, 0.0)
        o_ref[...] = jnp.sum(o.reshape(nq, n_heads, o.shape[1]), axis=1)


def decode_attention(q, pool_k, pool_v, layer, page_table, k_new, v_new, *, r_q=None, r_k=None, mask=None,
                     name="decode_attn"):
    b, nq, h, d = q.shape
    n_pool, page = pool_k.shape[1], pool_k.shape[2]
    n_pages = page_table.shape[1]
    hd = h * d
    rows = nq * h
    eye = jnp.eye(h, dtype=q.dtype)
    q_exp = (q[:, :, :, None, :] * eye[None, None, :, :, None]).reshape(b, rows, hd).astype(BF16)
    pad = lambda a: jnp.pad(a.reshape(b, nq, hd), ((0, 0), (0, page - nq), (0, 0)))
    pk = pool_k.reshape(pool_k.shape[0], n_pool, page, hd)
    pv = pool_v.reshape(pool_v.shape[0], n_pool, page, hd)
    last = n_pages - 1
    pool_spec = pl.BlockSpec((None, None, page, hd), lambda bi, p, pt: (layer, pt[bi, jnp.minimum(p, last)], 0, 0))
    new_spec = pl.BlockSpec((None, page, hd), lambda bi, p, pt: (bi, 0, 0))
    in_specs = [pl.BlockSpec((None, rows, hd), lambda bi, p, pt: (bi, 0, 0)), pool_spec, pool_spec, new_spec, new_spec]
    args = [q_exp, pk, pv, pad(k_new), pad(v_new)]
    if r_q is not None:
        n_tot = r_k.shape[1]
        rk = jnp.pad(r_k, ((0, 0), (0, (n_pages + 1) * page - n_tot), (0, 0)))
        rk = rk.reshape(b, n_pages + 1, page, h).transpose(0, 1, 3, 2)
        in_specs.append(pl.BlockSpec((None, rows, 1), lambda bi, p, pt: (bi, 0, 0)))
        in_specs.append(pl.BlockSpec((None, None, h, page), lambda bi, p, pt: (bi, p, 0, 0)))
        args += [r_q.reshape(b, rows, 1), rk]
    if mask is not None:
        in_specs.append(pl.BlockSpec((None, None, 8, page), lambda bi, p, pt: (bi, p, 0, 0)))
        args.append(mask)
    return pl.pallas_call(
        functools.partial(_decode_attn_body, n_pages=n_pages, nq=nq, n_heads=h, scale=d ** -0.5,
                          has_bias=r_q is not None, has_mask=mask is not None),
        out_shape=jax.ShapeDtypeStruct((b, nq, hd), F32),
        grid_spec=pltpu.PrefetchScalarGridSpec(
            num_scalar_prefetch=1,
            grid=(b, n_pages + 1),
            in_specs=in_specs,
            out_specs=pl.BlockSpec((None, nq, hd), lambda bi, p, pt: (bi, 0, 0)),
            scratch_shapes=[pltpu.VMEM((rows, 1), F32), pltpu.VMEM((rows, 1), F32), pltpu.VMEM((rows, hd), F32)],
        ),
        compiler_params=_params("parallel", "arbitrary"),
        name=name,
    )(page_table, *args)


def _dsa_decode_select_body(pt_ref, q_ref, wh_ref, kidx_ref, kin_ref, mask_ref, keys_s, *, n_pages, nq, n_heads,
                            topk, n_keys):
    p = pl.program_id(1)
    page = kidx_ref.shape[0]
    lane = lax.broadcasted_iota(jnp.int32, (8, page), 1)
    q_row = lax.broadcasted_iota(jnp.int32, (8, page), 0)

    def score_page(src):
        kc = src[...].astype(BF16)
        d = _dot_nt(q_ref[...], kc)
        rel = jnp.maximum(d, 0.0) * (wh_ref[...] * (IDX_DIM ** -0.5))
        sc = jnp.sum(rel.reshape(nq, n_heads, page), axis=1)
        sc = jnp.concatenate([sc, jnp.zeros((8 - nq, page), F32)], axis=0)
        bits = pltpu.bitcast(sc, jnp.int32)
        return bits ^ ((bits >> 31) & 0x7FFFFFFF)

    @pl.when(p < n_pages)
    def _():
        keys_s[p] = score_page(kidx_ref)

    @pl.when(p == n_pages)
    def _():
        key = score_page(kin_ref)
        keys_s[p] = jnp.where((lane <= q_row) & (lane < nq), key, INT_MIN)

        def count(pred):
            def body(c, part):
                return part + jnp.where(pred(keys_s[c], c * page + lane), 1.0, 0.0)
            part = lax.fori_loop(0, n_pages + 1, body, jnp.zeros((8, page), F32))
            return jnp.sum(part, -1, keepdims=True)

        kf = float(topk)

        def value_bit(it, t):
            cand = t + lax.shift_left(jnp.int32(1), 31 - it)
            return jnp.where(count(lambda kk, pos: kk >= cand) >= kf, cand, t)

        thr = lax.fori_loop(0, 32, value_bit, jnp.full((8, 1), INT_MIN, jnp.int32))
        need = kf - count(lambda kk, pos: kk > thr)
        idx_bits = int(n_keys).bit_length()

        def index_bit(it, pos_max):
            cand = pos_max + lax.shift_left(jnp.int32(1), idx_bits - 1 - it)
            cnt = count(lambda kk, pos: (kk == thr) & (pos < cand))
            return jnp.where(cnt < need, cand, pos_max)

        last_tie = lax.fori_loop(0, idx_bits, index_bit, jnp.zeros((8, 1), jnp.int32))

        def emit(c, carry):
            kk = keys_s[c]
            pos = c * page + lane
            sel = (kk > thr) | ((kk == thr) & (pos <= last_tie))
            sel = sel & (kk != INT_MIN)
            mask_ref[c] = jnp.where(sel, 1.0, 0.0)
            return carry

        lax.fori_loop(0, n_pages + 1, emit, 0)


def dsa_decode_select(qidx, wh, pool_kidx, layer, page_table, kidx_new, *, topk, name="dsa_decode_select"):
    b, nq, hi, di = qidx.shape
    page = pool_kidx.shape[2]
    n_pages = page_table.shape[1]
    rows = nq * hi
    last = n_pages - 1
    kin = jnp.pad(kidx_new, ((0, 0), (0, page - nq), (0, 0)))
    return pl.pallas_call(
        functools.partial(_dsa_decode_select_body, n_pages=n_pages, nq=nq, n_heads=hi, topk=topk,
                          n_keys=n_pages * page + nq),
        out_shape=jax.ShapeDtypeStruct((b, n_pages + 1, 8, page), F32),
        grid_spec=pltpu.PrefetchScalarGridSpec(
            num_scalar_prefetch=1,
            grid=(b, n_pages + 1),
            in_specs=[
                pl.BlockSpec((None, rows, di), lambda bi, p, pt: (bi, 0, 0)),
                pl.BlockSpec((None, rows, 1), lambda bi, p, pt: (bi, 0, 0)),
                pl.BlockSpec((None, None, page, di), lambda bi, p, pt: (layer, pt[bi, jnp.minimum(p, last)], 0, 0)),
                pl.BlockSpec((None, page, di), lambda bi, p, pt: (bi, 0, 0)),
            ],
            out_specs=pl.BlockSpec((None, n_pages + 1, 8, page), lambda bi, p, pt: (bi, 0, 0, 0)),
            scratch_shapes=[pltpu.VMEM((n_pages + 1, 8, page), jnp.int32)],
        ),
        compiler_params=_params("parallel", "arbitrary"),
        name=name,
    )(page_table, qidx.reshape(b, rows, di).astype(BF16), wh.reshape(b, rows, 1), pool_kidx, kin)


def _paged_attn_body(*refs, n_steps, pg, nq, n_heads, scale, has_bias, has_mask):
    refs = list(refs)
    pt_ref, q_ref, hm_ref = refs[:3]
    k_refs = refs[3:3 + pg]
    v_refs = refs[3 + pg:3 + 2 * pg]
    kn_ref, vn_ref = refs[3 + 2 * pg:5 + 2 * pg]
    rest = refs[5 + 2 * pg:]
    if has_bias:
        rq_ref, rk_ref = rest[:2]
        rest = rest[2:]
    if has_mask:
        mask_ref, expand_ref = rest[:2]
        rest = rest[2:]
    if has_bias:
        o_ref, m_s, l_s, acc_s, rqb_s = rest
    else:
        o_ref, m_s, l_s, acc_s = rest
    p = pl.program_id(1)
    rows = nq * n_heads
    cols = k_refs[0].shape[0]
    lanes = m_s.shape[1]
    nblk = cols // lanes
    assert lanes % n_heads == 0

    @pl.when(p == 0)
    def _():
        m_s[...] = jnp.full(m_s.shape, NEG_BIG, F32)
        l_s[...] = jnp.zeros(l_s.shape, F32)
        acc_s[...] = jnp.zeros(acc_s.shape, F32)
        if has_bias:
            rqb_s[...] = jnp.broadcast_to(rq_ref[...], rqb_s.shape)

    def attend(k_srcs, v_srcs, slots, new_tokens):
        q = q_ref[...]
        hm = hm_ref[...]
        blocks = []
        for k_src, slot in zip(k_srcs, slots):
            s = _dot_nt(q, k_src[...].astype(BF16)) * scale
            if has_mask:
                mk = mask_ref[slot]
                mk = jnp.concatenate([jnp.broadcast_to(mk[i:i + 1, :], (n_heads, mk.shape[1])) for i in range(nq)], axis=0)
                sel = _dot(mk.astype(BF16), expand_ref[...])
            for j in range(nblk):
                blk = s[:, j * lanes:(j + 1) * lanes]
                if has_bias:
                    blk = blk + rk_ref[slot][:, j * lanes:(j + 1) * lanes] - rqb_s[...]
                if has_mask:
                    blk = jnp.where(sel[:, j * lanes:(j + 1) * lanes] > 0.5, blk, NEG_BIG)
                elif new_tokens:
                    q_idx = lax.broadcasted_iota(jnp.int32, (rows, lanes), 0) // n_heads
                    t_idx = (j * lanes + lax.broadcasted_iota(jnp.int32, (rows, lanes), 1)) // n_heads
                    blk = jnp.where((t_idx <= q_idx) & (t_idx < nq), blk, NEG_BIG)
                blocks.append(blk + hm)
        bmax = blocks[0]
        for blk in blocks[1:]:
            bmax = jnp.maximum(bmax, blk)
        m_old = m_s[...]
        m_new = jnp.maximum(m_old, jnp.max(bmax, -1, keepdims=True))
        a = jnp.exp(m_old - m_new)
        ps = [jnp.exp(blk - m_new) for blk in blocks]
        psum = ps[0]
        for pj in ps[1:]:
            psum = psum + pj
        l_s[...] = a * l_s[...] + psum
        acc = a * acc_s[...]
        for i, v_src in enumerate(v_srcs):
            pr = jnp.concatenate(ps[i * nblk:(i + 1) * nblk], axis=1).astype(BF16)
            acc = acc + _dot(pr, v_src[...].astype(BF16))
        acc_s[...] = acc
        m_s[...] = m_new

    @pl.when(p < n_steps - 1)
    def _():
        attend(k_refs, v_refs, list(range(pg)), False)

    @pl.when(p == n_steps - 1)
    def _():
        attend([kn_ref], [vn_ref], [0], True)
        o_ref[...] = acc_s[...] / jnp.sum(l_s[...], -1, keepdims=True)


def paged_attention(q, pool_k, pool_v, layer, page_table, k_new, v_new, *, pg, r_q=None, r_k=None, mask=None,
                    name="paged_attn"):
    b, nq, h, d = q.shape
    n_pool, page = pool_k.shape[1], pool_k.shape[2]
    n_pages = page_table.shape[1]
    assert n_pages % pg == 0 and nq <= 8
    n_steps = n_pages // pg + 1
    rows, cols = nq * h, page * h
    pk = pool_k.reshape(pool_k.shape[0], n_pool, cols, d)
    pv = pool_v.reshape(pool_v.shape[0], n_pool, cols, d)
    pad = lambda a: jnp.pad(a, ((0, 0), (0, page - nq), (0, 0), (0, 0))).reshape(b, cols, d)
    ri = lax.broadcasted_iota(jnp.int32, (rows, d), 0) % h
    ci = lax.broadcasted_iota(jnp.int32, (rows, d), 1) % h
    head_match = jnp.where(ri == ci, 0.0, NEG_BIG).astype(F32)
    last = n_pages - 1

    def pool_spec(i):
        return pl.BlockSpec((None, None, cols, d),
                            lambda bi, p, pt: (layer, pt[bi, jnp.minimum(p * pg + i, last)], 0, 0))

    const2 = lambda shape: pl.BlockSpec(shape, lambda bi, p, pt: (0, 0))
    per_b = lambda shape: pl.BlockSpec((None,) + shape, lambda bi, p, pt: (bi,) + (0,) * len(shape))
    in_specs = ([per_b((rows, d)), const2((rows, d))] + [pool_spec(i) for i in range(pg)] * 2
                + [per_b((cols, d)), per_b((cols, d))])
    args = [q.reshape(b, rows, d).astype(BF16), head_match] + [pk] * pg + [pv] * pg + [pad(k_new), pad(v_new)]
    if r_q is not None:
        n_tot = r_k.shape[1]
        rk = jnp.pad(r_k, ((0, 0), (0, n_pages * page + pg * page - n_tot), (0, 0)))
        rk = rk.reshape(b, n_steps, pg, 1, cols)
        in_specs += [per_b((rows, 1)), pl.BlockSpec((None, None, pg, 1, cols), lambda bi, p, pt: (bi, p, 0, 0, 0))]
        args += [r_q.reshape(b, rows, 1), rk]
    if mask is not None:
        expand = (lax.broadcasted_iota(jnp.int32, (page, cols), 1) // h
                  == lax.broadcasted_iota(jnp.int32, (page, cols), 0)).astype(BF16)
        in_specs += [pl.BlockSpec((None, None, pg, 8, page), lambda bi, p, pt: (bi, p, 0, 0, 0)), const2((page, cols))]
        assert mask.shape[1] >= n_steps * pg
        args += [mask[:, :n_steps * pg].reshape(b, n_steps, pg, 8, page), expand]
    out = pl.pallas_call(
        functools.partial(_paged_attn_body, n_steps=n_steps, pg=pg, nq=nq, n_heads=h, scale=d ** -0.5,
                          has_bias=r_q is not None, has_mask=mask is not None),
        out_shape=jax.ShapeDtypeStruct((b, rows, d), F32),
        grid_spec=pltpu.PrefetchScalarGridSpec(
            num_scalar_prefetch=1,
            grid=(b, n_steps),
            in_specs=in_specs,
            out_specs=per_b((rows, d)),
            scratch_shapes=[pltpu.VMEM((rows, d), F32)] * (4 if r_q is not None else 3),
        ),
        compiler_params=_params("parallel", "arbitrary"),
        name=name,
    )(page_table, *args)
    return out.reshape(b, nq, h * d)


def _paged_select_body(*refs, n_steps, pg, nq, n_heads, topk, n_keys):
    pt_ref, q_ref, wh_ref = refs[:3]
    kidx_refs = refs[3:3 + pg]
    kin_ref, mask_ref, keys_s = refs[3 + pg:]
    p = pl.program_id(1)
    page = kin_ref.shape[0]
    lane = lax.broadcasted_iota(jnp.int32, (8, page), 1)
    q_row = lax.broadcasted_iota(jnp.int32, (8, page), 0)
    n_slots = n_steps * pg

    def score_page(src):
        d = _dot_nt(q_ref[...], src[...].astype(BF16))
        rel = jnp.maximum(d, 0.0) * (wh_ref[...] * (IDX_DIM ** -0.5))
        sc = jnp.sum(rel.reshape(nq, n_heads, page), axis=1)
        sc = jnp.concatenate([sc, jnp.zeros((8 - nq, page), F32)], axis=0)
        bits = pltpu.bitcast(sc, jnp.int32)
        return bits ^ ((bits >> 31) & 0x7FFFFFFF)

    @pl.when(p < n_steps - 1)
    def _():
        for i in range(pg):
            keys_s[p * pg + i] = score_page(kidx_refs[i])

    @pl.when(p == n_steps - 1)
    def _():
        first_new = (n_steps - 1) * pg
        keys_s[first_new] = jnp.where((lane <= q_row) & (lane < nq), score_page(kin_ref), INT_MIN)
        for i in range(1, pg):
            keys_s[first_new + i] = jnp.full((8, page), INT_MIN, jnp.int32)

        keys = keys_s[...]
        pos = (lax.broadcasted_iota(jnp.int32, keys.shape, 0) * page
               + lax.broadcasted_iota(jnp.int32, keys.shape, 2))

        def count(pred):
            return jnp.sum(jnp.sum(jnp.where(pred, 1.0, 0.0), axis=0), -1, keepdims=True)

        kf = float(topk)

        def value_bit(it, t):
            cand = t + lax.shift_left(jnp.int32(1), 31 - it)
            return jnp.where(count(keys >= cand) >= kf, cand, t)

        thr = lax.fori_loop(0, 32, value_bit, jnp.full((8, 1), INT_MIN, jnp.int32))
        need = kf - count(keys > thr)
        idx_bits = int(n_keys).bit_length()

        def index_bit(it, pos_max):
            cand = pos_max + lax.shift_left(jnp.int32(1), idx_bits - 1 - it)
            return jnp.where(count((keys == thr) & (pos < cand)) < need, cand, pos_max)

        last_tie = lax.fori_loop(0, idx_bits, index_bit, jnp.zeros((8, 1), jnp.int32))
        sel = (keys > thr) | ((keys == thr) & (pos <= last_tie))
        mask_ref[...] = jnp.where(sel & (keys != INT_MIN), 1.0, 0.0)


def paged_select(qidx, wh, pool_kidx, layer, page_table, kidx_new, *, topk, pg, name="paged_select"):
    b, nq, hi, di = qidx.shape
    page = pool_kidx.shape[2]
    n_pages = page_table.shape[1]
    assert n_pages % pg == 0 and nq <= 8
    n_steps = n_pages // pg + 1
    rows = nq * hi
    last = n_pages - 1
    kin = jnp.pad(kidx_new, ((0, 0), (0, page - nq), (0, 0)))

    def pool_spec(i):
        return pl.BlockSpec((None, None, page, di),
                            lambda bi, p, pt: (layer, pt[bi, jnp.minimum(p * pg + i, last)], 0, 0))

    per_b = lambda shape: pl.BlockSpec((None,) + shape, lambda bi, p, pt: (bi,) + (0,) * len(shape))
    mask = pl.pallas_call(
        functools.partial(_paged_select_body, n_steps=n_steps, pg=pg, nq=nq, n_heads=hi, topk=topk,
                          n_keys=n_pages * page + nq),
        out_shape=jax.ShapeDtypeStruct((b, n_steps * pg, 8, page), F32),
        grid_spec=pltpu.PrefetchScalarGridSpec(
            num_scalar_prefetch=1,
            grid=(b, n_steps),
            in_specs=[per_b((rows, di)), per_b((rows, 1))] + [pool_spec(i) for i in range(pg)] + [per_b((page, di))],
            out_specs=per_b((n_steps * pg, 8, page)),
            scratch_shapes=[pltpu.VMEM((n_steps * pg, 8, page), jnp.int32)],
        ),
        compiler_params=_params("parallel", "arbitrary"),
        name=name,
    )(page_table, qidx.reshape(b, rows, di).astype(BF16), wh.reshape(b, rows, 1), *([pool_kidx] * pg), kin)
    return mask


GDN_HALO = 8


def _dot(a, b, **kw):
    return jnp.dot(a, b, preferred_element_type=F32, **kw)


def _dot_nt(a, b, **kw):
    return lax.dot_general(a, b, (((1,), (1,)), ((), ())), preferred_element_type=F32, **kw)


def _gdn_body(q_ref, k_ref, v_ref, z_ref, ba_ref, cwq_ref, cwk_ref, cwv_ref, gate_ref, nw_ref, h0_ref, s0_ref,
              o_ref, sout_ref,
              xq_s, xk_s, xv_s, s_s, u_s, wq_s, qkkt_s, gt_s, *, blk, chunk, n_vh, hg, cpi):
    jg = pl.program_id(1)
    sb = pl.program_id(2)
    n_sb = pl.num_programs(2)
    nch = blk // chunk
    hd = GDN_HEAD
    exact = dict(precision=lax.Precision.HIGHEST)

    @pl.when(sb == 0)
    def _():
        xq_s[0:GDN_HALO, :] = h0_ref[:, 0:hg * hd]
        xk_s[0:GDN_HALO, :] = h0_ref[:, hg * hd:2 * hg * hd]
        xv_s[0:GDN_HALO, :] = h0_ref[:, 2 * hg * hd:]
        s_s[...] = s0_ref[...]

    xq_s[GDN_HALO:, :] = q_ref[...]
    xk_s[GDN_HALO:, :] = k_ref[...]
    xv_s[GDN_HALO:, :] = v_ref[...]

    row = lax.broadcasted_iota(jnp.int32, (chunk, chunk), 0)
    col = lax.broadcasted_iota(jnp.int32, (chunk, chunk), 1)
    lower = row >= col
    strict = row > col
    ltri = jnp.where(lower, 1.0, 0.0).astype(F32)
    lane = lax.broadcasted_iota(jnp.int32, (chunk, 128), 1)
    sel_row = lax.broadcasted_iota(jnp.int32, (8, 128), 0)
    sel_lane = lax.broadcasted_iota(jnp.int32, (8, 128), 1)
    hv0 = 2 * hg * jg
    pick = jnp.where((sel_row < 2 * hg) & (sel_lane == n_vh + hv0 + sel_row), 1.0, 0.0).astype(F32)
    neg_a = -jnp.exp(gate_ref[0:1, :])
    dt_bias = gate_ref[1:2, :]

    def conv_silu(xs_ref, w_ref, r0):
        x = xs_ref[pl.ds(r0, chunk + GDN_HALO), :]
        w = w_ref[...]
        taps = w.shape[0]
        acc = None
        for t in range(taps):
            sh = taps - 1 - t
            xt = x if sh == 0 else pltpu.roll(x, sh, axis=0)
            term = xt[GDN_HALO:, :] * w[t:t + 1, :]
            acc = term if acc is None else acc + term
        return jax.nn.silu(acc)

    def l2n(x):
        return x * lax.rsqrt(jnp.sum(x * x, -1, keepdims=True) + L2_EPS)

    def prep(ci, carry):
        cs = [ci * cpi + u for u in range(cpi)]
        r0s = [pl.multiple_of(c * chunk, chunk) for c in cs]
        q_all = [conv_silu(xq_s, cwq_ref, r0) for r0 in r0s]
        k_all = [conv_silu(xk_s, cwk_ref, r0) for r0 in r0s]
        v_all = [conv_silu(xv_s, cwv_ref, r0) for r0 in r0s]
        ba = [ba_ref[pl.ds(r0, chunk), :] for r0 in r0s]
        beta_all = [jax.nn.sigmoid(x) for x in ba]
        g_all = [neg_a * jax.nn.softplus(x + dt_bias) for x in ba]
        gc_all = [_dot(ltri, g, **exact) for g in g_all]
        gc_rows = [_dot_nt(pick, g, **exact) for g in gc_all]
        pairs = [(u, h) for u in range(cpi) for h in range(hg)]
        units = [(u, e) for u in range(cpi) for e in range(2 * hg)]
        qs = {(u, h): l2n(q_all[u][:, h * hd:(h + 1) * hd]) * (hd ** -0.5) for u, h in pairs}
        ks = {(u, h): l2n(k_all[u][:, h * hd:(h + 1) * hd]) for u, h in pairs}
        k16 = {p: ks[p].astype(BF16) for p in pairs}
        gram = {p: _dot_nt(k16[p], k16[p]) for p in pairs}
        qk_raw = {p: _dot_nt(qs[p].astype(BF16), k16[p]) for p in pairs}
        beta = {(u, e): jnp.sum(jnp.where(lane == hv0 + e, beta_all[u], 0.0), -1, keepdims=True) for u, e in units}
        gcol = {(u, e): jnp.sum(jnp.where(lane == n_vh + hv0 + e, gc_all[u], 0.0), -1, keepdims=True) for u, e in units}
        g_last = {t: gcol[t][chunk - 1:chunk, :] for t in units}
        decay = {(u, e): jnp.where(lower, jnp.exp(jnp.where(lower, gcol[u, e] - gc_rows[u][e:e + 1, :], 0.0)), 0.0)
                 for u, e in units}
        nmat = {(u, e): jnp.where(strict, -(beta[u, e] * gram[u, e // 2] * decay[u, e]), 0.0) for u, e in units}
        cpow = {t: nmat[t].astype(BF16) for t in units}
        for _ in range(max(chunk.bit_length() - 2, 0)):
            cnew = {t: _dot(cpow[t], cpow[t]) for t in units}
            cpow = {t: cnew[t].astype(BF16) for t in units}
            corr = {t: _dot(nmat[t].astype(BF16), cpow[t]) for t in units}
            nmat = {t: nmat[t] + cnew[t] + corr[t] for t in units}
        eg = {t: jnp.exp(gcol[t]) for t in units}
        rhs = {(u, e): jnp.concatenate([v_all[u][:, e * hd:(e + 1) * hd] * beta[u, e],
                                        (ks[u, e // 2] * beta[u, e]) * eg[u, e]], axis=1) for u, e in units}
        sol = {t: rhs[t] + _dot(nmat[t].astype(BF16), rhs[t].astype(BF16)) for t in units}
        for u, e in units:
            c, t = cs[u], (u, e)
            u_s[e, pl.ds(r0s[u], chunk), :] = sol[t][:, :hd]
            wq_s[e, c, 0:chunk, :] = sol[t][:, hd:].astype(BF16)
            wq_s[e, c, chunk:2 * chunk, :] = (qs[u, e // 2] * eg[t]).astype(BF16)
            qkkt_s[e, c, 0:chunk, :] = jnp.where(lower, qk_raw[u, e // 2] * decay[t], 0.0).astype(BF16)
            k_tail = ks[u, e // 2] * jnp.exp(g_last[t] - gcol[t])
            qkkt_s[e, c, chunk:, :] = k_tail.T.astype(BF16)
            gt_s[e, c] = jnp.broadcast_to(jnp.exp(g_last[t]), (8, 128))
        return carry

    lax.fori_loop(0, nch // cpi, prep, 0)

    nw = nw_ref[...]

    def recur(c, carry):
        r0 = pl.multiple_of(c * chunk, chunk)
        heads = range(2 * hg)
        s = [s_s[e] for e in heads]
        ws = [_dot(wq_s[e, c], s[e].astype(BF16)) for e in heads]
        v_new = [u_s[e, pl.ds(r0, chunk), :] - ws[e][0:chunk] for e in heads]
        mix = [_dot(qkkt_s[e, c], v_new[e].astype(BF16)) for e in heads]
        for e in heads:
            s_s[e] = s[e] * gt_s[e, c][0:1, 0:1] + mix[e][chunk:]
            o = ws[e][chunk:] + mix[e][0:chunk]
            o = o * lax.rsqrt(jnp.mean(o * o, -1, keepdims=True) + RMS_EPS) * nw
            o = o * jax.nn.silu(z_ref[pl.ds(r0, chunk), e * hd:(e + 1) * hd])
            o_ref[pl.ds(r0, chunk), e * hd:(e + 1) * hd] = o.astype(o_ref.dtype)
        return carry

    lax.fori_loop(0, nch, recur, 0)

    xq_s[0:GDN_HALO, :] = xq_s[blk:blk + GDN_HALO, :]
    xk_s[0:GDN_HALO, :] = xk_s[blk:blk + GDN_HALO, :]
    xv_s[0:GDN_HALO, :] = xv_s[blk:blk + GDN_HALO, :]

    @pl.when(sb == n_sb - 1)
    def _():
        sout_ref[...] = s_s[...]


def gdn_mixer(qkv, z, ba, conv_hist, s0, conv_w, a_log, dt_bias, norm_w, layer, *, blk, hg, cpi, chunk=GDN_CHUNK,
              name="gdn"):
    b, n, conv_dim = qkv.shape
    hd = GDN_HEAD
    n_qk, n_vh = GDN_QK_HEADS, GDN_V_HEADS
    assert n_vh == 2 * n_qk and n % blk == 0 and blk % chunk == 0 and 2 * n_vh <= ba.shape[2]
    assert n_qk % hg == 0 and 2 * hg <= 8 and (blk // chunk) % cpi == 0
    ng = n_qk // hg
    taps = conv_w.shape[1]
    hist = jnp.concatenate([jnp.zeros((b, GDN_HALO - (taps - 1), conv_dim), F32), conv_hist.astype(F32)], axis=1)
    hq = hist[:, :, :n_qk * hd].reshape(b, GDN_HALO, ng, hg * hd)
    hk = hist[:, :, n_qk * hd:2 * n_qk * hd].reshape(b, GDN_HALO, ng, hg * hd)
    hv = hist[:, :, 2 * n_qk * hd:].reshape(b, GDN_HALO, ng, 2 * hg * hd)
    h0 = jnp.concatenate([hq, hk, hv], axis=-1).transpose(0, 2, 1, 3)
    gate = jnp.zeros((8, ba.shape[2]), F32)
    gate = gate.at[0, n_vh:2 * n_vh].set(a_log[layer]).at[1, n_vh:2 * n_vh].set(dt_bias[layer])
    nw = norm_w.reshape(norm_w.shape[0], 1, hd)
    nch = blk // chunk
    wq, wv = hg * hd, 2 * hg * hd
    o, s_out = pl.pallas_call(
        functools.partial(_gdn_body, blk=blk, chunk=chunk, n_vh=n_vh, hg=hg, cpi=cpi),
        out_shape=(jax.ShapeDtypeStruct((b, n, n_vh * hd), BF16), jax.ShapeDtypeStruct(s0.shape, F32)),
        grid=(b, ng, n // blk),
        in_specs=[
            pl.BlockSpec((None, blk, wq), lambda bi, j, sb: (bi, sb, j)),
            pl.BlockSpec((None, blk, wq), lambda bi, j, sb: (bi, sb, ng + j)),
            pl.BlockSpec((None, blk, wv), lambda bi, j, sb: (bi, sb, ng + j)),
            pl.BlockSpec((None, blk, wv), lambda bi, j, sb: (bi, sb, j)),
            pl.BlockSpec((None, blk, ba.shape[2]), lambda bi, j, sb: (bi, sb, 0)),
            pl.BlockSpec((None, taps, wq), lambda bi, j, sb: (layer, 0, j)),
            pl.BlockSpec((None, taps, wq), lambda bi, j, sb: (layer, 0, ng + j)),
            pl.BlockSpec((None, taps, wv), lambda bi, j, sb: (layer, 0, ng + j)),
            pl.BlockSpec((8, ba.shape[2]), lambda bi, j, sb: (0, 0)),
            pl.BlockSpec((None, 1, hd), lambda bi, j, sb: (layer, 0, 0)),
            pl.BlockSpec((None, None, GDN_HALO, 4 * wq), lambda bi, j, sb: (bi, j, 0, 0)),
            pl.BlockSpec((None, 2 * hg, hd, hd), lambda bi, j, sb: (bi, j, 0, 0)),
        ],
        out_specs=(
            pl.BlockSpec((None, blk, wv), lambda bi, j, sb: (bi, sb, j)),
            pl.BlockSpec((None, 2 * hg, hd, hd), lambda bi, j, sb: (bi, j, 0, 0)),
        ),
        scratch_shapes=[
            pltpu.VMEM((blk + GDN_HALO, wq), F32), pltpu.VMEM((blk + GDN_HALO, wq), F32),
            pltpu.VMEM((blk + GDN_HALO, wv), F32),
            pltpu.VMEM((2 * hg, hd, hd), F32),
            pltpu.VMEM((2 * hg, blk, hd), F32),
            pltpu.VMEM((2 * hg, nch, 2 * chunk, hd), BF16),
            pltpu.VMEM((2 * hg, nch, chunk + hd, chunk), BF16),
            pltpu.VMEM((2 * hg, nch, 8, 128), F32),
        ],
        compiler_params=_params("parallel", "parallel", "arbitrary"),
        name=name,
    )(qkv, qkv, qkv, z, ba, conv_w, conv_w, conv_w, gate, nw, h0, s0)
    return o, s_out


def _l2norm(x):
    return x * lax.rsqrt(jnp.sum(x * x, -1, keepdims=True) + L2_EPS)


def _causal_dwconv(x_ext, w):
    width = w.shape[0]
    n = x_ext.shape[1] - width + 1
    return sum(x_ext[:, j:j + n] * w[j] for j in range(width))


def _chunk_gated_delta(q, k, v, g, beta, s0):
    b, n, h, dk = k.shape
    dv = v.shape[-1]
    c = min(GDN_CHUNK, n)
    pad = (-n) % c
    if pad:
        padf = lambda a: jnp.pad(a, [(0, 0), (0, pad)] + [(0, 0)] * (a.ndim - 2))
        q, k, v, g, beta = padf(q), padf(k), padf(v), padf(g), padf(beta)
    nc = (n + pad) // c
    qh, kh, vh = [a.transpose(0, 2, 1, 3).reshape(b, h, nc, c, a.shape[-1]) for a in (q, k, v)]
    gh, bh = [a.transpose(0, 2, 1).reshape(b, h, nc, c) for a in (g, beta)]
    gc = jnp.cumsum(gh, -1)
    diff = gc[..., :, None] - gc[..., None, :]
    lower = jnp.tril(jnp.ones((c, c), bool))
    strict = jnp.tril(jnp.ones((c, c), bool), -1)
    decay = jnp.where(lower, jnp.exp(jnp.where(lower, diff, 0.0)), 0.0)
    kb = kh * bh[..., None]
    a_mat = jnp.where(strict, jnp.einsum('bhnid,bhnjd->bhnij', kb, kh) * decay, 0.0)
    rhs = jnp.concatenate([vh * bh[..., None], kb * jnp.exp(gc)[..., None]], -1)
    sol = lax.linalg.triangular_solve(a_mat + jnp.eye(c, dtype=a_mat.dtype), rhs,
                                      left_side=True, lower=True, unit_diagonal=True)
    u, w = sol[..., :dv], sol[..., dv:]
    qk = jnp.where(lower, jnp.einsum('bhnid,bhnjd->bhnij', qh, kh) * decay, 0.0)
    q_dec = qh * jnp.exp(gc)[..., None]
    k_tail = kh * jnp.exp(gc[..., -1:] - gc)[..., None]
    g_tot = jnp.exp(gc[..., -1])

    def step(s, xs_n):
        u_n, w_n, qk_n, qd_n, kt_n, gt_n = xs_n
        v_new = u_n - jnp.einsum('bhcd,bhde->bhce', w_n, s)
        o = jnp.einsum('bhcd,bhde->bhce', qd_n, s) + jnp.einsum('bhij,bhje->bhie', qk_n, v_new)
        s = s * gt_n[..., None, None] + jnp.einsum('bhcd,bhce->bhde', kt_n, v_new)
        return s, o

    xs_all = tuple(jnp.moveaxis(a, 2, 0) for a in (u, w, qk, q_dec, k_tail, g_tot))
    s_fin, o = lax.scan(step, s0, xs_all)
    o = jnp.moveaxis(o, 0, 2).reshape(b, h, nc * c, dv)[:, :, :n].transpose(0, 2, 1, 3)
    return o, s_fin


def _gdn_core(qkv, z, bt, at, conv_buf, s0, conv_w, a_log, dt_bias, norm_w):
    b, n, _ = qkv.shape
    key_dim = GDN_QK_HEADS * GDN_HEAD
    ext = jnp.concatenate([conv_buf.astype(qkv.dtype), qkv], axis=1)
    new_buf = ext[:, -(conv_w.shape[0] - 1):]
    qkv = jax.nn.silu(_causal_dwconv(ext, conv_w))
    q, k, v = jnp.split(qkv, [key_dim, 2 * key_dim], axis=-1)
    rep = GDN_V_HEADS // GDN_QK_HEADS
    q = jnp.repeat(_l2norm(q.reshape(b, n, GDN_QK_HEADS, GDN_HEAD)), rep, axis=2) * (GDN_HEAD ** -0.5)
    k = jnp.repeat(_l2norm(k.reshape(b, n, GDN_QK_HEADS, GDN_HEAD)), rep, axis=2)
    v = v.reshape(b, n, GDN_V_HEADS, GDN_HEAD)
    beta = jax.nn.sigmoid(bt)
    g = -jnp.exp(a_log) * jax.nn.softplus(at + dt_bias)
    o, s_fin = _chunk_gated_delta(q, k, v, g, beta, s0)
    o = o * lax.rsqrt(jnp.mean(o * o, -1, keepdims=True) + RMS_EPS) * norm_w
    o = o * jax.nn.silu(z.reshape(b, n, GDN_V_HEADS, GDN_HEAD))
    return o.reshape(b, n, GDN_V_HEADS * GDN_HEAD), new_buf, s_fin


def _suffix_exclusive(logf):
    return lax.cumsum(logf, axis=1, reverse=True) - logf


def _paged_rows(pool, page_table):
    g = pool[page_table]
    return g.reshape((g.shape[0], g.shape[1] * g.shape[2]) + g.shape[3:])


def _take_rows(a, idx):
    return jax.vmap(lambda ab, ib: ab[ib])(a, idx)


def _fox_attention_small(q, k, v, r_q, r_k, q_pos, k_pos):
    d = q.shape[-1]
    rk = r_k.transpose(0, 2, 1)
    s = jnp.einsum('bqhd,bkhd->bhqk', q, k, preferred_element_type=F32) * (d ** -0.5)
    s = s + rk[:, :, None, :] - r_q.transpose(0, 2, 1)[:, :, :, None]
    s = jnp.where(k_pos[None, None, None, :] <= q_pos[None, None, :, None], s, -jnp.inf)
    p = jax.nn.softmax(s, axis=-1)
    return jnp.einsum('bhqk,bkhd->bqhd', p, v)


def _dsa_attention_small(q, qi, wh, q_pos, k_idx, n_keys, gather_kv):
    d = q.shape[-1]
    topk = min(TOPK_MAX, n_keys // 4)
    k_pos = jnp.arange(k_idx.shape[1])
    rel = jax.nn.relu(jnp.einsum('bqhd,bkd->bqhk', qi, k_idx, preferred_element_type=F32) * (IDX_DIM ** -0.5))
    score = jnp.einsum('bqhk,bqh->bqk', rel, wh)
    score = jnp.where(k_pos[None, None, :] <= q_pos[None, :, None], score, -jnp.inf)
    _, sel = lax.top_k(score, topk)
    valid = sel <= q_pos[None, :, None]
    k_sel, v_sel = gather_kv(sel)
    s = jnp.einsum('bqhd,bqkhd->bqhk', q, k_sel, preferred_element_type=F32) * (d ** -0.5)
    s = jnp.where(valid[:, :, None, :], s, -jnp.inf)
    p = jax.nn.softmax(s, axis=-1)
    return jnp.einsum('bqhk,bqkhd->bqhd', p, v_sel)


def _gather_paged_or_new(pool_k, pool_v, page_table, k_new, v_new, sel):
    n_past = page_table.shape[1] * PAGE
    nb = sel.shape[0]
    in_past = (sel < n_past)[..., None, None]
    sp = jnp.minimum(sel, n_past - 1)
    phys = jnp.take_along_axis(page_table, (sp // PAGE).reshape(nb, -1), axis=1).reshape(sel.shape)
    off = sp % PAGE
    sn = jnp.clip(sel - n_past, 0, k_new.shape[1] - 1)
    k_sel = jnp.where(in_past, pool_k[phys, off], _take_rows(k_new, sn))
    v_sel = jnp.where(in_past, pool_v[phys, off], _take_rows(v_new, sn))
    return k_sel, v_sel


def _tile_n(n, col0, cap):
    for t in (1024, 512, 256, 128):
        if t <= cap and n % t == 0 and col0 % t == 0:
            return t
    raise ValueError((n, col0))


MM_VMEM_BUDGET_BYTES = 44 * 1024 * 1024


def _mm_tiles(tm, kdim, n, col0, has_res):
    def footprint(tn, tk):
        blocks = 2 * (tm * tk * 2 + tk * tn * 4 + tm * tn * 4 * (2 if has_res else 1))
        temps = tk * tn * 2 + tm * tn * 4 + (tm * tn * 4 if tk < kdim else 0)
        return blocks + temps
    for tk in (kdim, kdim // 2, kdim // 4):
        if tk % 128:
            continue
        for tn in (1024, 512, 256, 128):
            if n % tn == 0 and col0 % tn == 0 and footprint(tn, tk) <= MM_VMEM_BUDGET_BYTES:
                return tn, tk
    raise ValueError((tm, kdim, n, col0))


def kernel(x_prompt, x_sample, cache_fox_k, cache_fox_v, cache_fox_logf, cache_dsa_k, cache_dsa_v, cache_dsa_kidx, state_gdn, state_gdn_conv, state_ffn_conv, page_table, p_prompt, p_sample, gdn_w_in, gdn_conv_w, gdn_a_log, gdn_dt_bias, gdn_norm_w, gdn_w_out, fox_w_in, fox_b_f, fox_w_out, dsa_w_in, dsa_idx_ln_g, dsa_idx_ln_b, dsa_w_out, ffn_w_up, ffn_conv_w, ffn_conv_b, ffn_w_down, ln_mix_g, ln_mix_b, ln_ffn_g, ln_ffn_b, ple_w_proj, ple_w_gate):
    b, n, d = x_prompt.shape
    db, ns, _ = x_sample.shape
    depth = ffn_w_up.shape[0]
    n_past = page_table.shape[1] * PAGE
    alpha = (2.0 * depth) ** 0.25
    f2 = ffn_w_up.shape[2]
    key_dim = GDN_QK_HEADS * GDN_HEAD
    val_dim = GDN_V_HEADS * GDN_HEAD
    conv_dim = 2 * key_dim + val_dim

    groups = {
        "p": dict(nb=b, n=n, tm=1024, tn_cap=512, xf=x_prompt.reshape(b * n, d)),
        "s": dict(nb=db, n=ns, tm=db * ns, tn_cap=1024, xf=x_sample.reshape(db * ns, d)),
    }
    for gr in groups.values():
        gr["xb"] = gr["xf"].astype(BF16)

    def proj(gr, w, layer, col0, ncols, out_dtype=F32, **kw):
        tn, tk = _mm_tiles(gr["tm"], w.shape[1], ncols, col0, has_res=False)
        return matmul(gr["xb"], w, layer, col0=col0, n=ncols, tm=gr["tm"], tn=tn, tk=tk, out_dtype=out_dtype, **kw)

    def tail_proj(gr, w, layer, col0, padded):
        wt = jnp.pad(w[layer, :, col0:], ((0, 0), (0, padded - (w.shape[2] - col0))))[None]
        return matmul(gr["xb"], wt, 0, tm=gr["tm"], tn=padded, name="mm_tail")

    def out_proj(gr, o_b, w, layer, name="out_proj"):
        tn, tk = _mm_tiles(gr["tm"], o_b.shape[1], d, 0, has_res=True)
        return matmul(o_b, w, layer, tm=gr["tm"], tn=tn, tk=tk, res=gr["xf"], alpha=alpha, name=name)

    outs = {k: [] for k in ("gdn_s_p", "gdn_c_p", "gdn_s_s", "gdn_c_s", "fox_k_p", "fox_v_p", "fox_lf_p",
                            "fox_k_s", "fox_v_s", "fox_lf_s", "dsa_k_p", "dsa_v_p", "dsa_ki_p",
                            "dsa_k_s", "dsa_v_s", "dsa_ki_s", "ffn_c_p", "ffn_c_s")}
    pos_sq = n_past + jnp.arange(ns)
    pos_sk = jnp.arange(n_past + ns)

    for i in range(depth):
        kind, j = i % 3, i // 3
        for tag, gr in groups.items():
            nb, nn = gr["nb"], gr["n"]
            if kind == 0:
                qkv = proj(gr, gdn_w_in, j, 0, conv_dim).reshape(nb, nn, conv_dim)
                z = proj(gr, gdn_w_in, j, conv_dim, val_dim).reshape(nb, nn, val_dim)
                ba = tail_proj(gr, gdn_w_in, j, conv_dim + val_dim, 128).reshape(nb, nn, 128)
                bt, at = ba[..., :GDN_V_HEADS], ba[..., GDN_V_HEADS:2 * GDN_V_HEADS]
                if tag == "p":
                    conv_buf = jnp.zeros((nb, gdn_conv_w.shape[1] - 1, conv_dim), F32)
                    s0 = jnp.zeros((nb, GDN_V_HEADS, GDN_HEAD, GDN_HEAD), F32)
                    o, s_new = gdn_mixer(qkv, z, ba, conv_buf, s0, gdn_conv_w, gdn_a_log, gdn_dt_bias, gdn_norm_w, j,
                                         blk=512, hg=4, cpi=4)
                    c_new = qkv[:, nn - (gdn_conv_w.shape[1] - 1):]
                else:
                    conv_buf, s0 = state_gdn_conv[j], state_gdn[j]
                    o, c_new, s_new = _gdn_core(qkv, z, bt, at, conv_buf, s0, gdn_conv_w[j], gdn_a_log[j],
                                                gdn_dt_bias[j], gdn_norm_w[j])
                    o = o.astype(BF16)
                outs["gdn_s_" + tag].append(s_new)
                outs["gdn_c_" + tag].append(c_new)
                r = out_proj(gr, o.reshape(nb * nn, val_dim), gdn_w_out, j)
            elif kind == 1:
                k = proj(gr, fox_w_in, j, d, d)
                v = proj(gr, fox_w_in, j, 2 * d, d)
                f = tail_proj(gr, fox_w_in, j, 3 * d, 128)[:, :ATT_HEADS].reshape(nb, nn, ATT_HEADS)
                lf = jax.nn.log_sigmoid(f + fox_b_f[j])
                k4, v4 = k.reshape(nb, nn, ATT_HEADS, HEAD_DIM), v.reshape(nb, nn, ATT_HEADS, HEAD_DIM)
                outs["fox_k_" + tag].append(k4)
                outs["fox_v_" + tag].append(v4)
                outs["fox_lf_" + tag].append(lf)
                if tag == "p":
                    q = proj(gr, fox_w_in, j, 0, d, out_dtype=BF16)
                    o = flash_attention(q.reshape(nb, nn, d), k.reshape(nb, nn, d), v.reshape(nb, nn, d),
                                        r=_suffix_exclusive(lf), tq=512, tk=512, name="fox_flash")
                    o = o.reshape(nb * nn, d)
                else:
                    q = proj(gr, fox_w_in, j, 0, d).reshape(nb, nn, ATT_HEADS, HEAD_DIM)
                    lf_all = jnp.concatenate([_paged_rows(cache_fox_logf[j], page_table), lf], axis=1)
                    rr = _suffix_exclusive(lf_all)
                    o = paged_attention(q, cache_fox_k, cache_fox_v, j, page_table, k4, v4, pg=4,
                                        r_q=rr[:, n_past:], r_k=rr, name="fox_decode")
                    o = o.reshape(nb * nn, d).astype(BF16)
                r = out_proj(gr, o, fox_w_out, j)
            else:
                k = proj(gr, dsa_w_in, j, d, d)
                v = proj(gr, dsa_w_in, j, 2 * d, d)
                tail = tail_proj(gr, dsa_w_in, j, 3 * d + IDX_HEADS * IDX_DIM, 256)
                ki, _ = layer_norm(tail[:, :IDX_DIM], dsa_idx_ln_g, dsa_idx_ln_b, j, tm=min(gr["tm"], 512),
                                   name="dsa_ki_ln")
                wh = tail[:, IDX_DIM:IDX_DIM + IDX_HEADS] * (IDX_HEADS ** -0.5)
                k4, v4 = k.reshape(nb, nn, ATT_HEADS, HEAD_DIM), v.reshape(nb, nn, ATT_HEADS, HEAD_DIM)
                ki3 = ki.reshape(nb, nn, IDX_DIM)
                outs["dsa_k_" + tag].append(k4)
                outs["dsa_v_" + tag].append(v4)
                outs["dsa_ki_" + tag].append(ki3)
                if tag == "p":
                    q = proj(gr, dsa_w_in, j, 0, d, out_dtype=BF16)
                    qidx = proj(gr, dsa_w_in, j, 3 * d, IDX_HEADS * IDX_DIM, out_dtype=BF16)
                    mask = dsa_select(qidx.reshape(nb, nn, IDX_HEADS * IDX_DIM), ki3, wh.reshape(nb, nn, IDX_HEADS),
                                      tq=256, tk=512, topk=min(TOPK_MAX, nn // 4))
                    o = flash_attention(q.reshape(nb, nn, d), k.reshape(nb, nn, d), v.reshape(nb, nn, d),
                                        mask=mask, tq=512, tk=512, name="dsa_flash")
                    o = o.reshape(nb * nn, d)
                else:
                    q = proj(gr, dsa_w_in, j, 0, d).reshape(nb, nn, ATT_HEADS, HEAD_DIM)
                    qidx = proj(gr, dsa_w_in, j, 3 * d, IDX_HEADS * IDX_DIM).reshape(nb, nn, IDX_HEADS, IDX_DIM)
                    sel = paged_select(qidx, wh.reshape(nb, nn, IDX_HEADS), cache_dsa_kidx, j, page_table, ki3,
                                       topk=min(TOPK_MAX, (n_past + nn) // 4), pg=8, name="dsa_decode_select")
                    o = paged_attention(q, cache_dsa_k, cache_dsa_v, j, page_table, k4, v4, pg=4, mask=sel,
                                        name="dsa_decode")
                    o = o.reshape(nb * nn, d).astype(BF16)
                r = out_proj(gr, o, dsa_w_out, j)

            ln_tm = min(gr["tm"], 512)
            gr["xf"], gr["xb"] = layer_norm(r, ln_mix_g, ln_mix_b, i, tm=ln_tm, name="ln_mix")

            if tag == "p":
                act, c_new = ffn_up(gr["xb"], jnp.zeros((nb, 2, f2), F32), ffn_w_up, ffn_conv_w, ffn_conv_b, i,
                                    seq_len=nn, tm=gr["tm"], tn=512)
            else:
                hcur = proj(gr, ffn_w_up, i, 0, f2).reshape(nb, nn, f2)
                ext = jnp.concatenate([state_ffn_conv[i], hcur], axis=1)
                c_new = ext[:, -(ffn_conv_w.shape[1] - 1):]
                hc = _causal_dwconv(ext, ffn_conv_w[i]) + ffn_conv_b[i]
                gate, val = jnp.split(hc, [f2 // 2], axis=-1)
                act = (jax.nn.silu(gate) * val).reshape(nb * nn, f2 // 2).astype(BF16)
            outs["ffn_c_" + tag].append(c_new)
            r = out_proj(gr, act, ffn_w_down, i, name="ffn_down")
            x2f, x2b = layer_norm(r, ln_ffn_g, ln_ffn_b, i, tm=ln_tm, name="ln_ffn")
            p_in = (p_prompt if tag == "p" else p_sample)[i].reshape(nb * nn, -1).astype(BF16)
            gr["xf"], gr["xb"] = per_layer_embed(x2f, x2b, p_in, ple_w_proj, ple_w_gate, i, tm=gr["tm"],
                                                 tn=_tile_n(d, 0, gr["tn_cap"]))

    st = lambda key: jnp.stack(outs[key])
    return (groups["p"]["xf"].reshape(b, n, d), groups["s"]["xf"].reshape(db, ns, d),
            st("gdn_s_p"), st("gdn_c_p"), st("fox_k_p"), st("fox_v_p"), st("fox_lf_p"),
            st("dsa_k_p"), st("dsa_v_p"), st("dsa_ki_p"), st("ffn_c_p"),
            st("gdn_s_s"), st("gdn_c_s"), st("fox_k_s"), st("fox_v_s"), st("fox_lf_s"),
            st("dsa_k_s"), st("dsa_v_s"), st("dsa_ki_s"), st("ffn_c_s"))
```

```python
import functools
import math

import jax
import jax.numpy as jnp
from jax import lax
from jax.experimental import pallas as pl
from jax.experimental.pallas import tpu as pltpu

F32 = jnp.float32
BF16 = jnp.bfloat16

PAGE = 128
GDN_QK_HEADS = 16
GDN_V_HEADS = 32
GDN_HEAD = 128
GDN_CHUNK = 64
ATT_HEADS = 16
HEAD_DIM = 128
IDX_HEADS = 16
IDX_DIM = 128
TOPK_MAX = 256
LN_EPS = 1e-5
RMS_EPS = 1e-6
L2_EPS = 1e-6

V7X_VMEM_LIMIT_BYTES = 56 * 1024 * 1024
NEG_BIG = -1e30
INT_MIN = -(2 ** 31)


def _params(*sem):
    return pltpu.CompilerParams(dimension_semantics=sem, vmem_limit_bytes=V7X_VMEM_LIMIT_BYTES)


def _mm_body(*refs, nk, has_res, alpha):
    if has_res:
        x_ref, w_ref, r_ref, o_ref, *scratch = refs
    else:
        x_ref, w_ref, o_ref, *scratch = refs
        r_ref = None

    def finish(acc):
        if has_res:
            acc = alpha * r_ref[...] + acc
        o_ref[...] = acc.astype(o_ref.dtype)

    part = jnp.dot(x_ref[...], w_ref[...].astype(BF16), preferred_element_type=F32)
    if nk == 1:
        finish(part)
        return
    acc_ref, = scratch
    k = pl.program_id(2)

    @pl.when(k == 0)
    def _():
        acc_ref[...] = part

    @pl.when(k > 0)
    def _():
        acc_ref[...] += part

    @pl.when(k == nk - 1)
    def _():
        finish(acc_ref[...])


def matmul(x, w, layer, *, col0=0, n=None, tm, tn, tk=None, out_dtype=F32, res=None, alpha=None, name="mm"):
    m, kdim = x.shape
    assert x.dtype == BF16 and w.shape[1] == kdim
    n = w.shape[2] - col0 if n is None else n
    tk = kdim if tk is None else tk
    assert m % tm == 0 and n % tn == 0 and kdim % tk == 0 and col0 % tn == 0
    nk = kdim // tk
    cb0 = col0 // tn
    in_specs = [
        pl.BlockSpec((tm, tk), lambda i, j, k: (i, k)),
        pl.BlockSpec((None, tk, tn), lambda i, j, k: (layer, k, cb0 + j)),
    ]
    args = [x, w]
    if res is not None:
        assert res.shape == (m, n)
        in_specs.append(pl.BlockSpec((tm, tn), lambda i, j, k: (i, j)))
        args.append(res)
    return pl.pallas_call(
        functools.partial(_mm_body, nk=nk, has_res=res is not None, alpha=alpha),
        out_shape=jax.ShapeDtypeStruct((m, n), out_dtype),
        grid=(m // tm, n // tn, nk),
        in_specs=in_specs,
        out_specs=pl.BlockSpec((tm, tn), lambda i, j, k: (i, j)),
        scratch_shapes=[pltpu.VMEM((tm, tn), F32)] if nk > 1 else [],
        compiler_params=_params("parallel", "parallel", "arbitrary"),
        name=name,
    )(*args)


def _ln_body(x_ref, g_ref, b_ref, of_ref, ob_ref, *, eps):
    x = x_ref[...]
    mu = jnp.mean(x, -1, keepdims=True)
    xc = x - mu
    var = jnp.mean(xc * xc, -1, keepdims=True)
    y = xc * lax.rsqrt(var + eps) * g_ref[...] + b_ref[...]
    of_ref[...] = y
    ob_ref[...] = y.astype(BF16)


def layer_norm(x, g, b, layer, *, tm, eps=LN_EPS, name="ln"):
    m, n = x.shape
    assert m % tm == 0
    g3 = g.reshape(g.shape[0], 1, n)
    b3 = b.reshape(b.shape[0], 1, n)
    vec = pl.BlockSpec((None, 1, n), lambda i: (layer, 0, 0))
    row = pl.BlockSpec((tm, n), lambda i: (i, 0))
    return pl.pallas_call(
        functools.partial(_ln_body, eps=eps),
        out_shape=(jax.ShapeDtypeStruct((m, n), F32), jax.ShapeDtypeStruct((m, n), BF16)),
        grid=(m // tm,),
        in_specs=[row, vec, vec],
        out_specs=(row, row),
        compiler_params=_params("parallel"),
        name=name,
    )(x, g3, b3)


def _ple_body(xb_ref, pb_ref, wg_ref, wp_ref, xr_ref, of_ref, ob_ref):
    gate = jnp.dot(xb_ref[...], wg_ref[...].astype(BF16), preferred_element_type=F32)
    proj = jnp.dot(pb_ref[...], wp_ref[...].astype(BF16), preferred_element_type=F32)
    y = xr_ref[...] + proj * jax.nn.sigmoid(gate)
    of_ref[...] = y
    ob_ref[...] = y.astype(BF16)


def per_layer_embed(x_f, x_b, p_b, w_proj, w_gate, layer, *, tm, tn, name="ple"):
    m, d = x_f.shape
    pd = p_b.shape[1]
    assert m % tm == 0 and d % tn == 0
    out = pl.BlockSpec((tm, tn), lambda i, j: (i, j))
    return pl.pallas_call(
        _ple_body,
        out_shape=(jax.ShapeDtypeStruct((m, d), F32), jax.ShapeDtypeStruct((m, d), BF16)),
        grid=(m // tm, d // tn),
        in_specs=[
            pl.BlockSpec((tm, d), lambda i, j: (i, 0)),
            pl.BlockSpec((tm, pd), lambda i, j: (i, 0)),
            pl.BlockSpec((None, d, tn), lambda i, j: (layer, 0, j)),
            pl.BlockSpec((None, pd, tn), lambda i, j: (layer, 0, j)),
            out,
        ],
        out_specs=(out, out),
        compiler_params=_params("parallel", "parallel"),
        name=name,
    )(x_b, p_b, w_gate, w_proj, x_f)


HALO_ROWS = 16


def _ffn_up_body(x_ref, xh_ref, wg_ref, wv_ref, cwg_ref, cwv_ref, cbg_ref, cbv_ref, hg_ref, hv_ref,
                 act_ref, tg_ref, tv_ref, hs_ref, *, tiles_per_seq, tm):
    first = (pl.program_id(0) % tiles_per_seq) == 0
    x = x_ref[...]
    xh = xh_ref[...]
    conv = []
    for w_ref, cw_ref, cb_ref, hist_ref, tail_ref in ((wg_ref, cwg_ref, cbg_ref, hg_ref, tg_ref),
                                                     (wv_ref, cwv_ref, cbv_ref, hv_ref, tv_ref)):
        w = w_ref[...].astype(BF16)
        h = jnp.dot(x, w, preferred_element_type=F32)
        halo = jnp.dot(xh, w, preferred_element_type=F32)
        hs_ref[0:8, :] = jnp.where(first, hist_ref[...], halo[HALO_ROWS - 8:, :])
        hs_ref[8:8 + tm, :] = h
        cw = cw_ref[...]
        c = hs_ref[6:6 + tm, :] * cw[0:1, :] + hs_ref[7:7 + tm, :] * cw[1:2, :] + h * cw[2:3, :] + cb_ref[...]
        conv.append(c)
        tail_ref[...] = hs_ref[8 + tm - 2:8 + tm, :]
    act_ref[...] = (jax.nn.silu(conv[0]) * conv[1]).astype(act_ref.dtype)


def ffn_up(x_b, hist, w_up, conv_w, conv_b, layer, *, seq_len, tm, tn, name="ffn_up"):
    m, d = x_b.shape
    f2 = w_up.shape[2]
    f = f2 // 2
    nb = m // seq_len
    assert seq_len % tm == 0 and f % tn == 0 and tm % HALO_ROWS == 0
    tiles_per_seq = seq_len // tm
    nj = f // tn
    hist8 = jnp.concatenate([jnp.zeros((nb, 6, f2), F32), hist.astype(F32)], axis=1)
    cw = conv_w
    cb = conv_b.reshape(conv_b.shape[0], 1, f2)
    halo_blocks = tm // HALO_ROWS

    def wspec(off):
        return pl.BlockSpec((None, d, tn), lambda i, j: (layer, 0, off + j))

    def cwspec(off):
        return pl.BlockSpec((None, cw.shape[1], tn), lambda i, j: (layer, 0, off + j))

    def cbspec(off):
        return pl.BlockSpec((None, 1, tn), lambda i, j: (layer, 0, off + j))

    def hspec(off):
        return pl.BlockSpec((None, 8, tn), lambda i, j: (i // tiles_per_seq, 0, off + j))

    tail = pl.BlockSpec((None, 2, tn), lambda i, j: (i, 0, j))
    act, tail_g, tail_v = pl.pallas_call(
        functools.partial(_ffn_up_body, tiles_per_seq=tiles_per_seq, tm=tm),
        out_shape=(jax.ShapeDtypeStruct((m, f), BF16),
                   jax.ShapeDtypeStruct((m // tm, 2, f), F32), jax.ShapeDtypeStruct((m // tm, 2, f), F32)),
        grid=(m // tm, nj),
        in_specs=[
            pl.BlockSpec((tm, d), lambda i, j: (i, 0)),
            pl.BlockSpec((HALO_ROWS, d), lambda i, j: (jnp.maximum(i * halo_blocks - 1, 0), 0)),
            wspec(0), wspec(nj), cwspec(0), cwspec(nj), cbspec(0), cbspec(nj), hspec(0), hspec(nj),
        ],
        out_specs=(pl.BlockSpec((tm, tn), lambda i, j: (i, j)), tail, tail),
        scratch_shapes=[pltpu.VMEM((tm + 8, tn), F32)],
        compiler_params=_params("arbitrary", "arbitrary"),
        name=name,
    )(x_b, x_b, w_up, w_up, cw, cw, cb, cb, hist8, hist8)
    last = slice(tiles_per_seq - 1, None, tiles_per_seq)
    return act, jnp.concatenate([tail_g[last], tail_v[last]], axis=-1)


def _flash_body(*refs, tq, tk, scale, has_bias, has_mask):
    refs = list(refs)
    q_ref, k_ref, v_ref = refs[:3]
    rest = refs[3:]
    if has_bias:
        rq_ref, rk_ref = rest[:2]
        rest = rest[2:]
    if has_mask:
        mask_ref = rest[0]
        rest = rest[1:]
    if has_bias:
        o_ref, m_ref, l_ref, acc_ref, rqb_ref = rest
    else:
        o_ref, m_ref, l_ref, acc_ref = rest
    qi = pl.program_id(2)
    lanes = m_ref.shape[1]
    nblk = tk // lanes
    m_ref[...] = jnp.full(m_ref.shape, NEG_BIG, F32)
    l_ref[...] = jnp.zeros(l_ref.shape, F32)
    acc_ref[...] = jnp.zeros(acc_ref.shape, F32)
    if has_bias:
        rqb_ref[...] = jnp.broadcast_to(rq_ref[...], rqb_ref.shape)
    tri = lax.broadcasted_iota(jnp.int32, (tq, lanes), 1) - lax.broadcasted_iota(jnp.int32, (tq, lanes), 0)

    def chunk(c, diagonal):
        off = pl.multiple_of(c * tk, tk)
        kc = k_ref[pl.ds(off, tk), :].astype(BF16)
        vc = v_ref[pl.ds(off, tk), :].astype(BF16)
        s = lax.dot_general(q_ref[...], kc, (((1,), (1,)), ((), ())), preferred_element_type=F32) * scale
        blocks = [s[:, j * lanes:(j + 1) * lanes] for j in range(nblk)]
        if has_bias:
            rk = rk_ref[c]
            rqb = rqb_ref[...]
            blocks = [blocks[j] + rk[:, j * lanes:(j + 1) * lanes] - rqb for j in range(nblk)]
        if has_mask:
            pieces = tk // mask_ref.shape[-1]
            keep = jnp.concatenate([mask_ref[c * pieces + i] for i in range(pieces)], axis=1).astype(jnp.int32)
            blocks = [jnp.where(keep[:, j * lanes:(j + 1) * lanes] != 0, blocks[j], NEG_BIG) for j in range(nblk)]
        elif diagonal:
            blocks = [jnp.where(tri + j * lanes <= 0, blocks[j], NEG_BIG) for j in range(nblk)]
        bmax = blocks[0]
        for blk in blocks[1:]:
            bmax = jnp.maximum(bmax, blk)
        m_old = m_ref[...]
        m_new = jnp.maximum(m_old, jnp.max(bmax, -1, keepdims=True))
        a = jnp.exp(m_old - m_new)
        ps = [jnp.exp(blk - m_new) for blk in blocks]
        psum = ps[0]
        for pj in ps[1:]:
            psum = psum + pj
        l_ref[...] = a * l_ref[...] + psum
        p = jnp.concatenate(ps, axis=1).astype(BF16)
        acc_ref[...] = a * acc_ref[...] + jnp.dot(p, vc, preferred_element_type=F32)
        m_ref[...] = m_new

    def full_chunk(c, carry):
        chunk(c, False)
        return carry

    if has_mask:
        lax.fori_loop(0, qi + 1, full_chunk, 0)
    else:
        lax.fori_loop(0, qi, full_chunk, 0)
        chunk(qi, True)
    o_ref[...] = (acc_ref[...] / jnp.sum(l_ref[...], -1, keepdims=True)).astype(o_ref.dtype)


def flash_attention(q, k, v, *, r=None, mask=None, tq, tk, name="flash"):
    b, n, hd = q.shape
    h = hd // HEAD_DIM
    assert n % tq == 0 and tq == tk and tk % HEAD_DIM == 0
    nkc = n // tk
    in_specs = [
        pl.BlockSpec((None, tq, HEAD_DIM), lambda bi, hi, qi: (bi, qi, hi)),
        pl.BlockSpec((None, n, HEAD_DIM), lambda bi, hi, qi: (bi, 0, hi)),
        pl.BlockSpec((None, n, HEAD_DIM), lambda bi, hi, qi: (bi, 0, hi)),
    ]
    args = [q, k, v]
    if r is not None:
        rt = r.transpose(0, 2, 1)
        in_specs.append(pl.BlockSpec((None, None, tq, 1), lambda bi, hi, qi: (bi, hi, qi, 0)))
        in_specs.append(pl.BlockSpec((None, None, nkc, 1, tk), lambda bi, hi, qi: (bi, hi, 0, 0, 0)))
        args += [rt.reshape(b, h, n, 1), rt.reshape(b, h, nkc, 1, tk)]
    if mask is not None:
        mtk = mask.shape[3]
        assert tk % mtk == 0 and mask.shape[1] * mtk == n
        in_specs.append(pl.BlockSpec((None, n // mtk, tq, mtk), lambda bi, hi, qi: (bi, 0, qi, 0)))
        args.append(mask)
    return pl.pallas_call(
        functools.partial(_flash_body, tq=tq, tk=tk, scale=HEAD_DIM ** -0.5,
                          has_bias=r is not None, has_mask=mask is not None),
        out_shape=jax.ShapeDtypeStruct((b, n, hd), BF16),
        grid=(b, h, n // tq),
        in_specs=in_specs,
        out_specs=pl.BlockSpec((None, tq, HEAD_DIM), lambda bi, hi, qi: (bi, qi, hi)),
        scratch_shapes=[pltpu.VMEM((tq, HEAD_DIM), F32)] * (4 if r is not None else 3),
        compiler_params=_params("parallel", "parallel", "arbitrary"),
        name=name,
    )(*args)


def _dsa_select_body(q_ref, kidx_ref, wh_ref, mask_ref, keys_ref, *, tq, tk, topk, n_keys):
    qi = pl.program_id(1)
    n_valid = ((qi + 1) * tq + tk - 1) // tk
    whs = wh_ref[...] * (IDX_DIM ** -0.5)
    q_pos = qi * tq + lax.broadcasted_iota(jnp.int32, (tq, tk), 0)
    k_iota = lax.broadcasted_iota(jnp.int32, (tq, tk), 1)
    n_heads = wh_ref.shape[-1]
    lanes = 128

    def score_chunk(c, carry):
        off = pl.multiple_of(c * tk, tk)
        kc = kidx_ref[pl.ds(off, tk), :].astype(BF16)
        acc = jnp.zeros((tq, tk), F32)
        for h in range(n_heads):
            d = lax.dot_general(q_ref[:, h * IDX_DIM:(h + 1) * IDX_DIM], kc, (((1,), (1,)), ((), ())),
                                preferred_element_type=F32)
            acc = acc + jnp.maximum(d, 0.0) * whs[:, h:h + 1]
        bits = pltpu.bitcast(acc, jnp.int32)
        key = bits ^ ((bits >> 31) & 0x7FFFFFFF)
        keys_ref[c] = jnp.where((off + k_iota) <= q_pos, key, INT_MIN)
        return carry

    lax.fori_loop(0, n_valid, score_chunk, 0)

    def count(pred):
        def body(c, part):
            ind = jnp.where(pred(keys_ref[c], c * tk), 1.0, 0.0)
            for s in range(tk // lanes):
                part = part + ind[:, s * lanes:(s + 1) * lanes]
            return part
        part = lax.fori_loop(0, n_valid, body, jnp.zeros((tq, lanes), F32))
        return jnp.sum(part, -1, keepdims=True)

    kf = float(topk)

    def value_bit(it, t):
        cand = t + lax.shift_left(jnp.int32(1), 31 - it)
        cnt = count(lambda kk, off: kk >= cand)
        return jnp.where(cnt >= kf, cand, t)

    thr = lax.fori_loop(0, 32, value_bit, jnp.full((tq, 1), INT_MIN, jnp.int32))
    cnt_ge = count(lambda kk, off: kk >= thr)
    cnt_gt = count(lambda kk, off: kk > thr)
    need = kf - cnt_gt
    idx_bits = int(n_keys).bit_length()

    def tie_search(_):
        def index_bit(it, p):
            cand = p + lax.shift_left(jnp.int32(1), idx_bits - 1 - it)
            cnt = count(lambda kk, off: (kk == thr) & ((off + k_iota) < cand))
            return jnp.where(cnt < need, cand, p)
        return lax.fori_loop(0, idx_bits, index_bit, jnp.zeros((tq, 1), jnp.int32))

    excess = jnp.max(cnt_ge - kf) > 0.0
    last_tie = lax.cond(excess, tie_search, lambda _: jnp.full((tq, 1), n_keys, jnp.int32), 0)

    mask_ref[...] = jnp.zeros(mask_ref.shape, mask_ref.dtype)

    def emit(c, carry):
        off = c * tk
        kk = keys_ref[c]
        k_pos = off + k_iota
        sel = (kk > thr) | ((kk == thr) & (k_pos <= last_tie))
        sel = sel & (k_pos <= q_pos)
        mask_ref[c] = jnp.where(sel, 1, 0).astype(mask_ref.dtype)
        return carry

    lax.fori_loop(0, n_valid, emit, 0)


def dsa_select(qidx, kidx, wh, *, tq, tk, topk, name="dsa_select"):
    b, n, _ = qidx.shape
    assert n % tq == 0 and n % tk == 0
    nkc = n // tk
    return pl.pallas_call(
        functools.partial(_dsa_select_body, tq=tq, tk=tk, topk=topk, n_keys=n),
        out_shape=jax.ShapeDtypeStruct((b, nkc, n, tk), jnp.int8),
        grid=(b, n // tq),
        in_specs=[
            pl.BlockSpec((None, tq, qidx.shape[2]), lambda bi, qi: (bi, qi, 0)),
            pl.BlockSpec((None, n, kidx.shape[2]), lambda bi, qi: (bi, 0, 0)),
            pl.BlockSpec((None, tq, wh.shape[2]), lambda bi, qi: (bi, qi, 0)),
        ],
        out_specs=pl.BlockSpec((None, nkc, tq, tk), lambda bi, qi: (bi, 0, qi, 0)),
        scratch_shapes=[pltpu.VMEM((nkc, tq, tk), jnp.int32)],
        compiler_params=_params("parallel", "arbitrary"),
        name=name,
    )(qidx, kidx, wh)


def _decode_attn_body(*refs, n_pages, nq, n_heads, scale, has_bias, has_mask):
    refs = list(refs)
    pt_ref, q_ref, k_ref, v_ref, kn_ref, vn_ref = refs[:6]
    rest = refs[6:]
    if has_bias:
        rq_ref, rk_ref = rest[:2]
        rest = rest[2:]
    if has_mask:
        mask_ref = rest[0]
        rest = rest[1:]
    o_ref, m_s, l_s, acc_s = rest
    p = pl.program_id(1)
    rows = nq * n_heads
    page = k_ref.shape[0]

    @pl.when(p == 0)
    def _():
        m_s[...] = jnp.full(m_s.shape, NEG_BIG, F32)
        l_s[...] = jnp.zeros(l_s.shape, F32)
        acc_s[...] = jnp.zeros(acc_s.shape, F32)

    def attend(k_src, v_src, new_tokens):
        kc = k_src[...].astype(BF16)
        vc = v_src[...].astype(BF16)
        s = _dot_nt(q_ref[...], kc) * scale
        if has_bias:
            s = s + jnp.concatenate([rk_ref[...]] * nq, axis=0) - rq_ref[...]
        ok = None
        if has_mask:
            mk = mask_ref[...]
            ok = jnp.concatenate([jnp.broadcast_to(mk[i:i + 1, :], (n_heads, page)) for i in range(nq)], axis=0) > 0.0
        elif new_tokens:
            q_idx = lax.broadcasted_iota(jnp.int32, (rows, page), 0) // n_heads
            t_idx = lax.broadcasted_iota(jnp.int32, (rows, page), 1)
            ok = (t_idx <= q_idx) & (t_idx < nq)
        if ok is not None:
            s = jnp.where(ok, s, NEG_BIG)
        m_old = m_s[...]
        m_new = jnp.maximum(m_old, jnp.max(s, -1, keepdims=True))
        pr = jnp.exp(s - m_new)
        a = jnp.exp(m_old - m_new)
        l_s[...] = a * l_s[...] + jnp.sum(pr, -1, keepdims=True)
        acc_s[...] = a * acc_s[...] + _dot(pr.astype(BF16), vc)
        m_s[...] = m_new

    @pl.when(p < n_pages)
    def _():
        attend(k_ref, v_ref, False)

    @pl.when(p == n_pages)
    def _():
        attend(kn_ref, vn_ref, True)
        hd = acc_s.shape[1] // n_heads
        o = acc_s[...] / l_s[...]
        r_head = lax.broadcasted_iota(jnp.int32, o.shape, 0) % n_heads
        c_head = lax.broadcasted_iota(jnp.int32, o.shape, 1) // hd
        o = jnp.where(r_head == c_head, o, 0.0)
        o_ref[...] = jnp.sum(o.reshape(nq, n_heads, o.shape[1]), axis=1)


def decode_attention(q, pool_k, pool_v, layer, page_table, k_new, v_new, *, r_q=None, r_k=None, mask=None,
                     name="decode_attn"):
    b, nq, h, d = q.shape
    n_pool, page = pool_k.shape[1], pool_k.shape[2]
    n_pages = page_table.shape[1]
    hd = h * d
    rows = nq * h
    eye = jnp.eye(h, dtype=q.dtype)
    q_exp = (q[:, :, :, None, :] * eye[None, None, :, :, None]).reshape(b, rows, hd).astype(BF16)
    pad = lambda a: jnp.pad(a.reshape(b, nq, hd), ((0, 0), (0, page - nq), (0, 0)))
    pk = pool_k.reshape(pool_k.shape[0], n_pool, page, hd)
    pv = pool_v.reshape(pool_v.shape[0], n_pool, page, hd)
    last = n_pages - 1
    pool_spec = pl.BlockSpec((None, None, page, hd), lambda bi, p, pt: (layer, pt[bi, jnp.minimum(p, last)], 0, 0))
    new_spec = pl.BlockSpec((None, page, hd), lambda bi, p, pt: (bi, 0, 0))
    in_specs = [pl.BlockSpec((None, rows, hd), lambda bi, p, pt: (bi, 0, 0)), pool_spec, pool_spec, new_spec, new_spec]
    args = [q_exp, pk, pv, pad(k_new), pad(v_new)]
    if r_q is not None:
        n_tot = r_k.shape[1]
        rk = jnp.pad(r_k, ((0, 0), (0, (n_pages + 1) * page - n_tot), (0, 0)))
        rk = rk.reshape(b, n_pages + 1, page, h).transpose(0, 1, 3, 2)
        in_specs.append(pl.BlockSpec((None, rows, 1), lambda bi, p, pt: (bi, 0, 0)))
        in_specs.append(pl.BlockSpec((None, None, h, page), lambda bi, p, pt: (bi, p, 0, 0)))
        args += [r_q.reshape(b, rows, 1), rk]
    if mask is not None:
        in_specs.append(pl.BlockSpec((None, None, 8, page), lambda bi, p, pt: (bi, p, 0, 0)))
        args.append(mask)
    return pl.pallas_call(
        functools.partial(_decode_attn_body, n_pages=n_pages, nq=nq, n_heads=h, scale=d ** -0.5,
                          has_bias=r_q is not None, has_mask=mask is not None),
        out_shape=jax.ShapeDtypeStruct((b, nq, hd), F32),
        grid_spec=pltpu.PrefetchScalarGridSpec(
            num_scalar_prefetch=1,
            grid=(b, n_pages + 1),
            in_specs=in_specs,
            out_specs=pl.BlockSpec((None, nq, hd), lambda bi, p, pt: (bi, 0, 0)),
            scratch_shapes=[pltpu.VMEM((rows, 1), F32), pltpu.VMEM((rows, 1), F32), pltpu.VMEM((rows, hd), F32)],
        ),
        compiler_params=_params("parallel", "arbitrary"),
        name=name,
    )(page_table, *args)


def _dsa_decode_select_body(pt_ref, q_ref, wh_ref, kidx_ref, kin_ref, mask_ref, keys_s, *, n_pages, nq, n_heads,
                            topk, n_keys):
    p = pl.program_id(1)
    page = kidx_ref.shape[0]
    lane = lax.broadcasted_iota(jnp.int32, (8, page), 1)
    q_row = lax.broadcasted_iota(jnp.int32, (8, page), 0)

    def score_page(src):
        kc = src[...].astype(BF16)
        d = _dot_nt(q_ref[...], kc)
        rel = jnp.maximum(d, 0.0) * (wh_ref[...] * (IDX_DIM ** -0.5))
        sc = jnp.sum(rel.reshape(nq, n_heads, page), axis=1)
        sc = jnp.concatenate([sc, jnp.zeros((8 - nq, page), F32)], axis=0)
        bits = pltpu.bitcast(sc, jnp.int32)
        return bits ^ ((bits >> 31) & 0x7FFFFFFF)

    @pl.when(p < n_pages)
    def _():
        keys_s[p] = score_page(kidx_ref)

    @pl.when(p == n_pages)
    def _():
        key = score_page(kin_ref)
        keys_s[p] = jnp.where((lane <= q_row) & (lane < nq), key, INT_MIN)

        def count(pred):
            def body(c, part):
                return part + jnp.where(pred(keys_s[c], c * page + lane), 1.0, 0.0)
            part = lax.fori_loop(0, n_pages + 1, body, jnp.zeros((8, page), F32))
            return jnp.sum(part, -1, keepdims=True)

        kf = float(topk)

        def value_bit(it, t):
            cand = t + lax.shift_left(jnp.int32(1), 31 - it)
            return jnp.where(count(lambda kk, pos: kk >= cand) >= kf, cand, t)

        thr = lax.fori_loop(0, 32, value_bit, jnp.full((8, 1), INT_MIN, jnp.int32))
        need = kf - count(lambda kk, pos: kk > thr)
        idx_bits = int(n_keys).bit_length()

        def index_bit(it, pos_max):
            cand = pos_max + lax.shift_left(jnp.int32(1), idx_bits - 1 - it)
            cnt = count(lambda kk, pos: (kk == thr) & (pos < cand))
            return jnp.where(cnt < need, cand, pos_max)

        last_tie = lax.fori_loop(0, idx_bits, index_bit, jnp.zeros((8, 1), jnp.int32))

        def emit(c, carry):
            kk = keys_s[c]
            pos = c * page + lane
            sel = (kk > thr) | ((kk == thr) & (pos <= last_tie))
            sel = sel & (kk != INT_MIN)
            mask_ref[c] = jnp.where(sel, 1.0, 0.0)
            return carry

        lax.fori_loop(0, n_pages + 1, emit, 0)


def dsa_decode_select(qidx, wh, pool_kidx, layer, page_table, kidx_new, *, topk, name="dsa_decode_select"):
    b, nq, hi, di = qidx.shape
    page = pool_kidx.shape[2]
    n_pages = page_table.shape[1]
    rows = nq * hi
    last = n_pages - 1
    kin = jnp.pad(kidx_new, ((0, 0), (0, page - nq), (0, 0)))
    return pl.pallas_call(
        functools.partial(_dsa_decode_select_body, n_pages=n_pages, nq=nq, n_heads=hi, topk=topk,
                          n_keys=n_pages * page + nq),
        out_shape=jax.ShapeDtypeStruct((b, n_pages + 1, 8, page), F32),
        grid_spec=pltpu.PrefetchScalarGridSpec(
            num_scalar_prefetch=1,
            grid=(b, n_pages + 1),
            in_specs=[
                pl.BlockSpec((None, rows, di), lambda bi, p, pt: (bi, 0, 0)),
                pl.BlockSpec((None, rows, 1), lambda bi, p, pt: (bi, 0, 0)),
                pl.BlockSpec((None, None, page, di), lambda bi, p, pt: (layer, pt[bi, jnp.minimum(p, last)], 0, 0)),
                pl.BlockSpec((None, page, di), lambda bi, p, pt: (bi, 0, 0)),
            ],
            out_specs=pl.BlockSpec((None, n_pages + 1, 8, page), lambda bi, p, pt: (bi, 0, 0, 0)),
            scratch_shapes=[pltpu.VMEM((n_pages + 1, 8, page), jnp.int32)],
        ),
        compiler_params=_params("parallel", "arbitrary"),
        name=name,
    )(page_table, qidx.reshape(b, rows, di).astype(BF16), wh.reshape(b, rows, 1), pool_kidx, kin)


def _paged_attn_body(*refs, n_steps, pg, nq, n_heads, scale, has_bias, has_mask):
    refs = list(refs)
    pt_ref, q_ref, hm_ref = refs[:3]
    k_refs = refs[3:3 + pg]
    v_refs = refs[3 + pg:3 + 2 * pg]
    kn_ref, vn_ref = refs[3 + 2 * pg:5 + 2 * pg]
    rest = refs[5 + 2 * pg:]
    if has_bias:
        rq_ref, rk_ref = rest[:2]
        rest = rest[2:]
    if has_mask:
        mask_ref, expand_ref = rest[:2]
        rest = rest[2:]
    if has_bias:
        o_ref, m_s, l_s, acc_s, rqb_s = rest
    else:
        o_ref, m_s, l_s, acc_s = rest
    p = pl.program_id(1)
    rows = nq * n_heads
    cols = k_refs[0].shape[0]
    lanes = m_s.shape[1]
    nblk = cols // lanes
    assert lanes % n_heads == 0

    @pl.when(p == 0)
    def _():
        m_s[...] = jnp.full(m_s.shape, NEG_BIG, F32)
        l_s[...] = jnp.zeros(l_s.shape, F32)
        acc_s[...] = jnp.zeros(acc_s.shape, F32)
        if has_bias:
            rqb_s[...] = jnp.broadcast_to(rq_ref[...], rqb_s.shape)

    def attend(k_srcs, v_srcs, slots, new_tokens):
        q = q_ref[...]
        hm = hm_ref[...]
        blocks = []
        for k_src, slot in zip(k_srcs, slots):
            s = _dot_nt(q, k_src[...].astype(BF16)) * scale
            if has_mask:
                mk = mask_ref[slot]
                mk = jnp.concatenate([jnp.broadcast_to(mk[i:i + 1, :], (n_heads, mk.shape[1])) for i in range(nq)], axis=0)
                sel = _dot(mk.astype(BF16), expand_ref[...])
            for j in range(nblk):
                blk = s[:, j * lanes:(j + 1) * lanes]
                if has_bias:
                    blk = blk + rk_ref[slot][:, j * lanes:(j + 1) * lanes] - rqb_s[...]
                if has_mask:
                    blk = jnp.where(sel[:, j * lanes:(j + 1) * lanes] > 0.5, blk, NEG_BIG)
                elif new_tokens:
                    q_idx = lax.broadcasted_iota(jnp.int32, (rows, lanes), 0) // n_heads
                    t_idx = (j * lanes + lax.broadcasted_iota(jnp.int32, (rows, lanes), 1)) // n_heads
                    blk = jnp.where((t_idx <= q_idx) & (t_idx < nq), blk, NEG_BIG)
                blocks.append(blk + hm)
        bmax = blocks[0]
        for blk in blocks[1:]:
            bmax = jnp.maximum(bmax, blk)
        m_old = m_s[...]
        m_new = jnp.maximum(m_old, jnp.max(bmax, -1, keepdims=True))
        a = jnp.exp(m_old - m_new)
        ps = [jnp.exp(blk - m_new) for blk in blocks]
        psum = ps[0]
        for pj in ps[1:]:
            psum = psum + pj
        l_s[...] = a * l_s[...] + psum
        acc = a * acc_s[...]
        for i, v_src in enumerate(v_srcs):
            pr = jnp.concatenate(ps[i * nblk:(i + 1) * nblk], axis=1).astype(BF16)
            acc = acc + _dot(pr, v_src[...].astype(BF16))
        acc_s[...] = acc
        m_s[...] = m_new

    @pl.when(p < n_steps - 1)
    def _():
        attend(k_refs, v_refs, list(range(pg)), False)

    @pl.when(p == n_steps - 1)
    def _():
        attend([kn_ref], [vn_ref], [0], True)
        o_ref[...] = acc_s[...] / jnp.sum(l_s[...], -1, keepdims=True)


def paged_attention(q, pool_k, pool_v, layer, page_table, k_new, v_new, *, pg, r_q=None, r_k=None, mask=None,
                    name="paged_attn"):
    b, nq, h, d = q.shape
    n_pool, page = pool_k.shape[1], pool_k.shape[2]
    n_pages = page_table.shape[1]
    assert n_pages % pg == 0 and nq <= 8
    n_steps = n_pages // pg + 1
    rows, cols = nq * h, page * h
    pk = pool_k.reshape(pool_k.shape[0], n_pool, cols, d)
    pv = pool_v.reshape(pool_v.shape[0], n_pool, cols, d)
    pad = lambda a: jnp.pad(a, ((0, 0), (0, page - nq), (0, 0), (0, 0))).reshape(b, cols, d)
    ri = lax.broadcasted_iota(jnp.int32, (rows, d), 0) % h
    ci = lax.broadcasted_iota(jnp.int32, (rows, d), 1) % h
    head_match = jnp.where(ri == ci, 0.0, NEG_BIG).astype(F32)
    last = n_pages - 1

    def pool_spec(i):
        return pl.BlockSpec((None, None, cols, d),
                            lambda bi, p, pt: (layer, pt[bi, jnp.minimum(p * pg + i, last)], 0, 0))

    const2 = lambda shape: pl.BlockSpec(shape, lambda bi, p, pt: (0, 0))
    per_b = lambda shape: pl.BlockSpec((None,) + shape, lambda bi, p, pt: (bi,) + (0,) * len(shape))
    in_specs = ([per_b((rows, d)), const2((rows, d))] + [pool_spec(i) for i in range(pg)] * 2
                + [per_b((cols, d)), per_b((cols, d))])
    args = [q.reshape(b, rows, d).astype(BF16), head_match] + [pk] * pg + [pv] * pg + [pad(k_new), pad(v_new)]
    if r_q is not None:
        n_tot = r_k.shape[1]
        rk = jnp.pad(r_k, ((0, 0), (0, n_pages * page + pg * page - n_tot), (0, 0)))
        rk = rk.reshape(b, n_steps, pg, 1, cols)
        in_specs += [per_b((rows, 1)), pl.BlockSpec((None, None, pg, 1, cols), lambda bi, p, pt: (bi, p, 0, 0, 0))]
        args += [r_q.reshape(b, rows, 1), rk]
    if mask is not None:
        expand = (lax.broadcasted_iota(jnp.int32, (page, cols), 1) // h
                  == lax.broadcasted_iota(jnp.int32, (page, cols), 0)).astype(BF16)
        in_specs += [pl.BlockSpec((None, None, pg, 8, page), lambda bi, p, pt: (bi, p, 0, 0, 0)), const2((page, cols))]
        assert mask.shape[1] >= n_steps * pg
        args += [mask[:, :n_steps * pg].reshape(b, n_steps, pg, 8, page), expand]
    out = pl.pallas_call(
        functools.partial(_paged_attn_body, n_steps=n_steps, pg=pg, nq=nq, n_heads=h, scale=d ** -0.5,
                          has_bias=r_q is not None, has_mask=mask is not None),
        out_shape=jax.ShapeDtypeStruct((b, rows, d), F32),
        grid_spec=pltpu.PrefetchScalarGridSpec(
            num_scalar_prefetch=1,
            grid=(b, n_steps),
            in_specs=in_specs,
            out_specs=per_b((rows, d)),
            scratch_shapes=[pltpu.VMEM((rows, d), F32)] * (4 if r_q is not None else 3),
        ),
        compiler_params=_params("parallel", "arbitrary"),
        name=name,
    )(page_table, *args)
    return out.reshape(b, nq, h * d)


def _paged_select_body(*refs, n_steps, pg, nq, n_heads, topk, n_keys):
    pt_ref, q_ref, wh_ref = refs[:3]
    kidx_refs = refs[3:3 + pg]
    kin_ref, mask_ref, keys_s = refs[3 + pg:]
    p = pl.program_id(1)
    page = kin_ref.shape[0]
    lane = lax.broadcasted_iota(jnp.int32, (8, page), 1)
    q_row = lax.broadcasted_iota(jnp.int32, (8, page), 0)
    n_slots = n_steps * pg

    def score_page(src):
        d = _dot_nt(q_ref[...], src[...].astype(BF16))
        rel = jnp.maximum(d, 0.0) * (wh_ref[...] * (IDX_DIM ** -0.5))
        sc = jnp.sum(rel.reshape(nq, n_heads, page), axis=1)
        sc = jnp.concatenate([sc, jnp.zeros((8 - nq, page), F32)], axis=0)
        bits = pltpu.bitcast(sc, jnp.int32)
        return bits ^ ((bits >> 31) & 0x7FFFFFFF)

    @pl.when(p < n_steps - 1)
    def _():
        for i in range(pg):
            keys_s[p * pg + i] = score_page(kidx_refs[i])

    @pl.when(p == n_steps - 1)
    def _():
        first_new = (n_steps - 1) * pg
        keys_s[first_new] = jnp.where((lane <= q_row) & (lane < nq), score_page(kin_ref), INT_MIN)
        for i in range(1, pg):
            keys_s[first_new + i] = jnp.full((8, page), INT_MIN, jnp.int32)

        keys = keys_s[...]
        pos = (lax.broadcasted_iota(jnp.int32, keys.shape, 0) * page
               + lax.broadcasted_iota(jnp.int32, keys.shape, 2))

        def count(pred):
            return jnp.sum(jnp.sum(jnp.where(pred, 1.0, 0.0), axis=0), -1, keepdims=True)

        kf = float(topk)

        def value_bit(it, t):
            cand = t + lax.shift_left(jnp.int32(1), 31 - it)
            return jnp.where(count(keys >= cand) >= kf, cand, t)

        thr = lax.fori_loop(0, 32, value_bit, jnp.full((8, 1), INT_MIN, jnp.int32))
        need = kf - count(keys > thr)
        idx_bits = int(n_keys).bit_length()

        def index_bit(it, pos_max):
            cand = pos_max + lax.shift_left(jnp.int32(1), idx_bits - 1 - it)
            return jnp.where(count((keys == thr) & (pos < cand)) < need, cand, pos_max)

        last_tie = lax.fori_loop(0, idx_bits, index_bit, jnp.zeros((8, 1), jnp.int32))
        sel = (keys > thr) | ((keys == thr) & (pos <= last_tie))
        mask_ref[...] = jnp.where(sel & (keys != INT_MIN), 1.0, 0.0)


def paged_select(qidx, wh, pool_kidx, layer, page_table, kidx_new, *, topk, pg, name="paged_select"):
    b, nq, hi, di = qidx.shape
    page = pool_kidx.shape[2]
    n_pages = page_table.shape[1]
    assert n_pages % pg == 0 and nq <= 8
    n_steps = n_pages // pg + 1
    rows = nq * hi
    last = n_pages - 1
    kin = jnp.pad(kidx_new, ((0, 0), (0, page - nq), (0, 0)))

    def pool_spec(i):
        return pl.BlockSpec((None, None, page, di),
                            lambda bi, p, pt: (layer, pt[bi, jnp.minimum(p * pg + i, last)], 0, 0))

    per_b = lambda shape: pl.BlockSpec((None,) + shape, lambda bi, p, pt: (bi,) + (0,) * len(shape))
    mask = pl.pallas_call(
        functools.partial(_paged_select_body, n_steps=n_steps, pg=pg, nq=nq, n_heads=hi, topk=topk,
                          n_keys=n_pages * page + nq),
        out_shape=jax.ShapeDtypeStruct((b, n_steps * pg, 8, page), F32),
        grid_spec=pltpu.PrefetchScalarGridSpec(
            num_scalar_prefetch=1,
            grid=(b, n_steps),
            in_specs=[per_b((rows, di)), per_b((rows, 1))] + [pool_spec(i) for i in range(pg)] + [per_b((page, di))],
            out_specs=per_b((n_steps * pg, 8, page)),
            scratch_shapes=[pltpu.VMEM((n_steps * pg, 8, page), jnp.int32)],
        ),
        compiler_params=_params("parallel", "arbitrary"),
        name=name,
    )(page_table, qidx.reshape(b, rows, di).astype(BF16), wh.reshape(b, rows, 1), *([pool_kidx] * pg), kin)
    return mask


GDN_HALO = 8


def _dot(a, b, **kw):
    return jnp.dot(a, b, preferred_element_type=F32, **kw)


def _dot_nt(a, b, **kw):
    return lax.dot_general(a, b, (((1,), (1,)), ((), ())), preferred_element_type=F32, **kw)


def _gdn_body(q_ref, k_ref, v_ref, z_ref, ba_ref, cwq_ref, cwk_ref, cwv_ref, gate_ref, nw_ref, h0_ref, s0_ref,
              o_ref, sout_ref,
              xq_s, xk_s, xv_s, s_s, u_s, wq_s, qkkt_s, gt_s, *, blk, chunk, n_vh, hg, cpi):
    jg = pl.program_id(1)
    sb = pl.program_id(2)
    n_sb = pl.num_programs(2)
    nch = blk // chunk
    hd = GDN_HEAD
    exact = dict(precision=lax.Precision.HIGHEST)

    @pl.when(sb == 0)
    def _():
        xq_s[0:GDN_HALO, :] = h0_ref[:, 0:hg * hd]
        xk_s[0:GDN_HALO, :] = h0_ref[:, hg * hd:2 * hg * hd]
        xv_s[0:GDN_HALO, :] = h0_ref[:, 2 * hg * hd:]
        s_s[...] = s0_ref[...]

    xq_s[GDN_HALO:, :] = q_ref[...]
    xk_s[GDN_HALO:, :] = k_ref[...]
    xv_s[GDN_HALO:, :] = v_ref[...]

    row = lax.broadcasted_iota(jnp.int32, (chunk, chunk), 0)
    col = lax.broadcasted_iota(jnp.int32, (chunk, chunk), 1)
    lower = row >= col
    strict = row > col
    ltri = jnp.where(lower, 1.0, 0.0).astype(F32)
    lane = lax.broadcasted_iota(jnp.int32, (chunk, 128), 1)
    sel_row = lax.broadcasted_iota(jnp.int32, (8, 128), 0)
    sel_lane = lax.broadcasted_iota(jnp.int32, (8, 128), 1)
    hv0 = 2 * hg * jg
    pick = jnp.where((sel_row < 2 * hg) & (sel_lane == n_vh + hv0 + sel_row), 1.0, 0.0).astype(F32)
    neg_a = -jnp.exp(gate_ref[0:1, :])
    dt_bias = gate_ref[1:2, :]

    def conv_silu(xs_ref, w_ref, r0):
        x = xs_ref[pl.ds(r0, chunk + GDN_HALO), :]
        w = w_ref[...]
        taps = w.shape[0]
        acc = None
        for t in range(taps):
            sh = taps - 1 - t
            xt = x if sh == 0 else pltpu.roll(x, sh, axis=0)
            term = xt[GDN_HALO:, :] * w[t:t + 1, :]
            acc = term if acc is None else acc + term
        return jax.nn.silu(acc)

    def l2n(x):
        return x * lax.rsqrt(jnp.sum(x * x, -1, keepdims=True) + L2_EPS)

    def prep(ci, carry):
        cs = [ci * cpi + u for u in range(cpi)]
        r0s = [pl.multiple_of(c * chunk, chunk) for c in cs]
        q_all = [conv_silu(xq_s, cwq_ref, r0) for r0 in r0s]
        k_all = [conv_silu(xk_s, cwk_ref, r0) for r0 in r0s]
        v_all = [conv_silu(xv_s, cwv_ref, r0) for r0 in r0s]
        ba = [ba_ref[pl.ds(r0, chunk), :] for r0 in r0s]
        beta_all = [jax.nn.sigmoid(x) for x in ba]
        g_all = [neg_a * jax.nn.softplus(x + dt_bias) for x in ba]
        gc_all = [_dot(ltri, g, **exact) for g in g_all]
        gc_rows = [_dot_nt(pick, g, **exact) for g in gc_all]
        pairs = [(u, h) for u in range(cpi) for h in range(hg)]
        units = [(u, e) for u in range(cpi) for e in range(2 * hg)]
        qs = {(u, h): l2n(q_all[u][:, h * hd:(h + 1) * hd]) * (hd ** -0.5) for u, h in pairs}
        ks = {(u, h): l2n(k_all[u][:, h * hd:(h + 1) * hd]) for u, h in pairs}
        k16 = {p: ks[p].astype(BF16) for p in pairs}
        gram = {p: _dot_nt(k16[p], k16[p]) for p in pairs}
        qk_raw = {p: _dot_nt(qs[p].astype(BF16), k16[p]) for p in pairs}
        beta = {(u, e): jnp.sum(jnp.where(lane == hv0 + e, beta_all[u], 0.0), -1, keepdims=True) for u, e in units}
        gcol = {(u, e): jnp.sum(jnp.where(lane == n_vh + hv0 + e, gc_all[u], 0.0), -1, keepdims=True) for u, e in units}
        g_last = {t: gcol[t][chunk - 1:chunk, :] for t in units}
        decay = {(u, e): jnp.where(lower, jnp.exp(jnp.where(lower, gcol[u, e] - gc_rows[u][e:e + 1, :], 0.0)), 0.0)
                 for u, e in units}
        nmat = {(u, e): jnp.where(strict, -(beta[u, e] * gram[u, e // 2] * decay[u, e]), 0.0) for u, e in units}
        cpow = {t: nmat[t].astype(BF16) for t in units}
        for _ in range(max(chunk.bit_length() - 2, 0)):
            cnew = {t: _dot(cpow[t], cpow[t]) for t in units}
            cpow = {t: cnew[t].astype(BF16) for t in units}
            corr = {t: _dot(nmat[t].astype(BF16), cpow[t]) for t in units}
            nmat = {t: nmat[t] + cnew[t] + corr[t] for t in units}
        eg = {t: jnp.exp(gcol[t]) for t in units}
        rhs = {(u, e): jnp.concatenate([v_all[u][:, e * hd:(e + 1) * hd] * beta[u, e],
                                        (ks[u, e // 2] * beta[u, e]) * eg[u, e]], axis=1) for u, e in units}
        sol = {t: rhs[t] + _dot(nmat[t].astype(BF16), rhs[t].astype(BF16)) for t in units}
        for u, e in units:
            c, t = cs[u], (u, e)
            u_s[e, pl.ds(r0s[u], chunk), :] = sol[t][:, :hd]
            wq_s[e, c, 0:chunk, :] = sol[t][:, hd:].astype(BF16)
            wq_s[e, c, chunk:2 * chunk, :] = (qs[u, e // 2] * eg[t]).astype(BF16)
            qkkt_s[e, c, 0:chunk, :] = jnp.where(lower, qk_raw[u, e // 2] * decay[t], 0.0).astype(BF16)
            k_tail = ks[u, e // 2] * jnp.exp(g_last[t] - gcol[t])
            qkkt_s[e, c, chunk:, :] = k_tail.T.astype(BF16)
            gt_s[e, c] = jnp.broadcast_to(jnp.exp(g_last[t]), (8, 128))
        return carry

    lax.fori_loop(0, nch // cpi, prep, 0)

    nw = nw_ref[...]

    def recur(c, carry):
        r0 = pl.multiple_of(c * chunk, chunk)
        heads = range(2 * hg)
        s = [s_s[e] for e in heads]
        ws = [_dot(wq_s[e, c], s[e].astype(BF16)) for e in heads]
        v_new = [u_s[e, pl.ds(r0, chunk), :] - ws[e][0:chunk] for e in heads]
        mix = [_dot(qkkt_s[e, c], v_new[e].astype(BF16)) for e in heads]
        for e in heads:
            s_s[e] = s[e] * gt_s[e, c][0:1, 0:1] + mix[e][chunk:]
            o = ws[e][chunk:] + mix[e][0:chunk]
            o = o * lax.rsqrt(jnp.mean(o * o, -1, keepdims=True) + RMS_EPS) * nw
            o = o * jax.nn.silu(z_ref[pl.ds(r0, chunk), e * hd:(e + 1) * hd])
            o_ref[pl.ds(r0, chunk), e * hd:(e + 1) * hd] = o.astype(o_ref.dtype)
        return carry

    lax.fori_loop(0, nch, recur, 0)

    xq_s[0:GDN_HALO, :] = xq_s[blk:blk + GDN_HALO, :]
    xk_s[0:GDN_HALO, :] = xk_s[blk:blk + GDN_HALO, :]
    xv_s[0:GDN_HALO, :] = xv_s[blk:blk + GDN_HALO, :]

    @pl.when(sb == n_sb - 1)
    def _():
        sout_ref[...] = s_s[...]


def gdn_mixer(qkv, z, ba, conv_hist, s0, conv_w, a_log, dt_bias, norm_w, layer, *, blk, hg, cpi, chunk=GDN_CHUNK,
              name="gdn"):
    b, n, conv_dim = qkv.shape
    hd = GDN_HEAD
    n_qk, n_vh = GDN_QK_HEADS, GDN_V_HEADS
    assert n_vh == 2 * n_qk and n % blk == 0 and blk % chunk == 0 and 2 * n_vh <= ba.shape[2]
    assert n_qk % hg == 0 and 2 * hg <= 8 and (blk // chunk) % cpi == 0
    ng = n_qk // hg
    taps = conv_w.shape[1]
    hist = jnp.concatenate([jnp.zeros((b, GDN_HALO - (taps - 1), conv_dim), F32), conv_hist.astype(F32)], axis=1)
    hq = hist[:, :, :n_qk * hd].reshape(b, GDN_HALO, ng, hg * hd)
    hk = hist[:, :, n_qk * hd:2 * n_qk * hd].reshape(b, GDN_HALO, ng, hg * hd)
    hv = hist[:, :, 2 * n_qk * hd:].reshape(b, GDN_HALO, ng, 2 * hg * hd)
    h0 = jnp.concatenate([hq, hk, hv], axis=-1).transpose(0, 2, 1, 3)
    gate = jnp.zeros((8, ba.shape[2]), F32)
    gate = gate.at[0, n_vh:2 * n_vh].set(a_log[layer]).at[1, n_vh:2 * n_vh].set(dt_bias[layer])
    nw = norm_w.reshape(norm_w.shape[0], 1, hd)
    nch = blk // chunk
    wq, wv = hg * hd, 2 * hg * hd
    o, s_out = pl.pallas_call(
        functools.partial(_gdn_body, blk=blk, chunk=chunk, n_vh=n_vh, hg=hg, cpi=cpi),
        out_shape=(jax.ShapeDtypeStruct((b, n, n_vh * hd), BF16), jax.ShapeDtypeStruct(s0.shape, F32)),
        grid=(b, ng, n // blk),
        in_specs=[
            pl.BlockSpec((None, blk, wq), lambda bi, j, sb: (bi, sb, j)),
            pl.BlockSpec((None, blk, wq), lambda bi, j, sb: (bi, sb, ng + j)),
            pl.BlockSpec((None, blk, wv), lambda bi, j, sb: (bi, sb, ng + j)),
            pl.BlockSpec((None, blk, wv), lambda bi, j, sb: (bi, sb, j)),
            pl.BlockSpec((None, blk, ba.shape[2]), lambda bi, j, sb: (bi, sb, 0)),
            pl.BlockSpec((None, taps, wq), lambda bi, j, sb: (layer, 0, j)),
            pl.BlockSpec((None, taps, wq), lambda bi, j, sb: (layer, 0, ng + j)),
            pl.BlockSpec((None, taps, wv), lambda bi, j, sb: (layer, 0, ng + j)),
            pl.BlockSpec((8, ba.shape[2]), lambda bi, j, sb: (0, 0)),
            pl.BlockSpec((None, 1, hd), lambda bi, j, sb: (layer, 0, 0)),
            pl.BlockSpec((None, None, GDN_HALO, 4 * wq), lambda bi, j, sb: (bi, j, 0, 0)),
            pl.BlockSpec((None, 2 * hg, hd, hd), lambda bi, j, sb: (bi, j, 0, 0)),
        ],
        out_specs=(
            pl.BlockSpec((None, blk, wv), lambda bi, j, sb: (bi, sb, j)),
            pl.BlockSpec((None, 2 * hg, hd, hd), lambda bi, j, sb: (bi, j, 0, 0)),
        ),
        scratch_shapes=[
            pltpu.VMEM((blk + GDN_HALO, wq), F32), pltpu.VMEM((blk + GDN_HALO, wq), F32),
            pltpu.VMEM((blk + GDN_HALO, wv), F32),
            pltpu.VMEM((2 * hg, hd, hd), F32),
            pltpu.VMEM((2 * hg, blk, hd), F32),
            pltpu.VMEM((2 * hg, nch, 2 * chunk, hd), BF16),
            pltpu.VMEM((2 * hg, nch, chunk + hd, chunk), BF16),
            pltpu.VMEM((2 * hg, nch, 8, 128), F32),
        ],
        compiler_params=_params("parallel", "parallel", "arbitrary"),
        name=name,
    )(qkv, qkv, qkv, z, ba, conv_w, conv_w, conv_w, gate, nw, h0, s0)
    return o, s_out


def _l2norm(x):
    return x * lax.rsqrt(jnp.sum(x * x, -1, keepdims=True) + L2_EPS)


def _causal_dwconv(x_ext, w):
    width = w.shape[0]
    n = x_ext.shape[1] - width + 1
    return sum(x_ext[:, j:j + n] * w[j] for j in range(width))


def _chunk_gated_delta(q, k, v, g, beta, s0):
    b, n, h, dk = k.shape
    dv = v.shape[-1]
    c = min(GDN_CHUNK, n)
    pad = (-n) % c
    if pad:
        padf = lambda a: jnp.pad(a, [(0, 0), (0, pad)] + [(0, 0)] * (a.ndim - 2))
        q, k, v, g, beta = padf(q), padf(k), padf(v), padf(g), padf(beta)
    nc = (n + pad) // c
    qh, kh, vh = [a.transpose(0, 2, 1, 3).reshape(b, h, nc, c, a.shape[-1]) for a in (q, k, v)]
    gh, bh = [a.transpose(0, 2, 1).reshape(b, h, nc, c) for a in (g, beta)]
    gc = jnp.cumsum(gh, -1)
    diff = gc[..., :, None] - gc[..., None, :]
    lower = jnp.tril(jnp.ones((c, c), bool))
    strict = jnp.tril(jnp.ones((c, c), bool), -1)
    decay = jnp.where(lower, jnp.exp(jnp.where(lower, diff, 0.0)), 0.0)
    kb = kh * bh[..., None]
    a_mat = jnp.where(strict, jnp.einsum('bhnid,bhnjd->bhnij', kb, kh) * decay, 0.0)
    rhs = jnp.concatenate([vh * bh[..., None], kb * jnp.exp(gc)[..., None]], -1)
    sol = lax.linalg.triangular_solve(a_mat + jnp.eye(c, dtype=a_mat.dtype), rhs,
                                      left_side=True, lower=True, unit_diagonal=True)
    u, w = sol[..., :dv], sol[..., dv:]
    qk = jnp.where(lower, jnp.einsum('bhnid,bhnjd->bhnij', qh, kh) * decay, 0.0)
    q_dec = qh * jnp.exp(gc)[..., None]
    k_tail = kh * jnp.exp(gc[..., -1:] - gc)[..., None]
    g_tot = jnp.exp(gc[..., -1])

    def step(s, xs_n):
        u_n, w_n, qk_n, qd_n, kt_n, gt_n = xs_n
        v_new = u_n - jnp.einsum('bhcd,bhde->bhce', w_n, s)
        o = jnp.einsum('bhcd,bhde->bhce', qd_n, s) + jnp.einsum('bhij,bhje->bhie', qk_n, v_new)
        s = s * gt_n[..., None, None] + jnp.einsum('bhcd,bhce->bhde', kt_n, v_new)
        return s, o

    xs_all = tuple(jnp.moveaxis(a, 2, 0) for a in (u, w, qk, q_dec, k_tail, g_tot))
    s_fin, o = lax.scan(step, s0, xs_all)
    o = jnp.moveaxis(o, 0, 2).reshape(b, h, nc * c, dv)[:, :, :n].transpose(0, 2, 1, 3)
    return o, s_fin


def _gdn_core(qkv, z, bt, at, conv_buf, s0, conv_w, a_log, dt_bias, norm_w):
    b, n, _ = qkv.shape
    key_dim = GDN_QK_HEADS * GDN_HEAD
    ext = jnp.concatenate([conv_buf.astype(qkv.dtype), qkv], axis=1)
    new_buf = ext[:, -(conv_w.shape[0] - 1):]
    qkv = jax.nn.silu(_causal_dwconv(ext, conv_w))
    q, k, v = jnp.split(qkv, [key_dim, 2 * key_dim], axis=-1)
    rep = GDN_V_HEADS // GDN_QK_HEADS
    q = jnp.repeat(_l2norm(q.reshape(b, n, GDN_QK_HEADS, GDN_HEAD)), rep, axis=2) * (GDN_HEAD ** -0.5)
    k = jnp.repeat(_l2norm(k.reshape(b, n, GDN_QK_HEADS, GDN_HEAD)), rep, axis=2)
    v = v.reshape(b, n, GDN_V_HEADS, GDN_HEAD)
    beta = jax.nn.sigmoid(bt)
    g = -jnp.exp(a_log) * jax.nn.softplus(at + dt_bias)
    o, s_fin = _chunk_gated_delta(q, k, v, g, beta, s0)
    o = o * lax.rsqrt(jnp.mean(o * o, -1, keepdims=True) + RMS_EPS) * norm_w
    o = o * jax.nn.silu(z.reshape(b, n, GDN_V_HEADS, GDN_HEAD))
    return o.reshape(b, n, GDN_V_HEADS * GDN_HEAD), new_buf, s_fin


def _suffix_exclusive(logf):
    return lax.cumsum(logf, axis=1, reverse=True) - logf


def _paged_rows(pool, page_table):
    g = pool[page_table]
    return g.reshape((g.shape[0], g.shape[1] * g.shape[2]) + g.shape[3:])


def _take_rows(a, idx):
    return jax.vmap(lambda ab, ib: ab[ib])(a, idx)


def _fox_attention_small(q, k, v, r_q, r_k, q_pos, k_pos):
    d = q.shape[-1]
    rk = r_k.transpose(0, 2, 1)
    s = jnp.einsum('bqhd,bkhd->bhqk', q, k, preferred_element_type=F32) * (d ** -0.5)
    s = s + rk[:, :, None, :] - r_q.transpose(0, 2, 1)[:, :, :, None]
    s = jnp.where(k_pos[None, None, None, :] <= q_pos[None, None, :, None], s, -jnp.inf)
    p = jax.nn.softmax(s, axis=-1)
    return jnp.einsum('bhqk,bkhd->bqhd', p, v)


def _dsa_attention_small(q, qi, wh, q_pos, k_idx, n_keys, gather_kv):
    d = q.shape[-1]
    topk = min(TOPK_MAX, n_keys // 4)
    k_pos = jnp.arange(k_idx.shape[1])
    rel = jax.nn.relu(jnp.einsum('bqhd,bkd->bqhk', qi, k_idx, preferred_element_type=F32) * (IDX_DIM ** -0.5))
    score = jnp.einsum('bqhk,bqh->bqk', rel, wh)
    score = jnp.where(k_pos[None, None, :] <= q_pos[None, :, None], score, -jnp.inf)
    _, sel = lax.top_k(score, topk)
    valid = sel <= q_pos[None, :, None]
    k_sel, v_sel = gather_kv(sel)
    s = jnp.einsum('bqhd,bqkhd->bqhk', q, k_sel, preferred_element_type=F32) * (d ** -0.5)
    s = jnp.where(valid[:, :, None, :], s, -jnp.inf)
    p = jax.nn.softmax(s, axis=-1)
    return jnp.einsum('bqhk,bqkhd->bqhd', p, v_sel)


def _gather_paged_or_new(pool_k, pool_v, page_table, k_new, v_new, sel):
    n_past = page_table.shape[1] * PAGE
    nb = sel.shape[0]
    in_past = (sel < n_past)[..., None, None]
    sp = jnp.minimum(sel, n_past - 1)
    phys = jnp.take_along_axis(page_table, (sp // PAGE).reshape(nb, -1), axis=1).reshape(sel.shape)
    off = sp % PAGE
    sn = jnp.clip(sel - n_past, 0, k_new.shape[1] - 1)
    k_sel = jnp.where(in_past, pool_k[phys, off], _take_rows(k_new, sn))
    v_sel = jnp.where(in_past, pool_v[phys, off], _take_rows(v_new, sn))
    return k_sel, v_sel


def _tile_n(n, col0, cap):
    for t in (1024, 512, 256, 128):
        if t <= cap and n % t == 0 and col0 % t == 0:
            return t
    raise ValueError((n, col0))


MM_VMEM_BUDGET_BYTES = 44 * 1024 * 1024


def _mm_tiles(tm, kdim, n, col0, has_res):
    def footprint(tn, tk):
        blocks = 2 * (tm * tk * 2 + tk * tn * 4 + tm * tn * 4 * (2 if has_res else 1))
        temps = tk * tn * 2 + tm * tn * 4 + (tm * tn * 4 if tk < kdim else 0)
        return blocks + temps
    for tk in (kdim, kdim // 2, kdim // 4):
        if tk % 128:
            continue
        for tn in (1024, 512, 256, 128):
            if n % tn == 0 and col0 % tn == 0 and footprint(tn, tk) <= MM_VMEM_BUDGET_BYTES:
                return tn, tk
    raise ValueError((tm, kdim, n, col0))


def kernel(x_prompt, x_sample, cache_fox_k, cache_fox_v, cache_fox_logf, cache_dsa_k, cache_dsa_v, cache_dsa_kidx, state_gdn, state_gdn_conv, state_ffn_conv, page_table, p_prompt, p_sample, gdn_w_in, gdn_conv_w, gdn_a_log, gdn_dt_bias, gdn_norm_w, gdn_w_out, fox_w_in, fox_b_f, fox_w_out, dsa_w_in, dsa_idx_ln_g, dsa_idx_ln_b, dsa_w_out, ffn_w_up, ffn_conv_w, ffn_conv_b, ffn_w_down, ln_mix_g, ln_mix_b, ln_ffn_g, ln_ffn_b, ple_w_proj, ple_w_gate):
    b, n, d = x_prompt.shape
    db, ns, _ = x_sample.shape
    depth = ffn_w_up.shape[0]
    n_past = page_table.shape[1] * PAGE
    alpha = (2.0 * depth) ** 0.25
    f2 = ffn_w_up.shape[2]
    key_dim = GDN_QK_HEADS * GDN_HEAD
    val_dim = GDN_V_HEADS * GDN_HEAD
    conv_dim = 2 * key_dim + val_dim

    groups = {
        "p": dict(nb=b, n=n, tm=1024, tn_cap=512, xf=x_prompt.reshape(b * n, d)),
        "s": dict(nb=db, n=ns, tm=db * ns, tn_cap=1024, xf=x_sample.reshape(db * ns, d)),
    }
    for gr in groups.values():
        gr["xb"] = gr["xf"].astype(BF16)

    def proj(gr, w, layer, col0, ncols, out_dtype=F32, **kw):
        tn, tk = _mm_tiles(gr["tm"], w.shape[1], ncols, col0, has_res=False)
        return matmul(gr["xb"], w, layer, col0=col0, n=ncols, tm=gr["tm"], tn=tn, tk=tk, out_dtype=out_dtype, **kw)

    def tail_proj(gr, w, layer, col0, padded):
        wt = jnp.pad(w[layer, :, col0:], ((0, 0), (0, padded - (w.shape[2] - col0))))[None]
        return matmul(gr["xb"], wt, 0, tm=gr["tm"], tn=padded, name="mm_tail")

    def out_proj(gr, o_b, w, layer, name="out_proj"):
        tn, tk = _mm_tiles(gr["tm"], o_b.shape[1], d, 0, has_res=True)
        return matmul(o_b, w, layer, tm=gr["tm"], tn=tn, tk=tk, res=gr["xf"], alpha=alpha, name=name)

    outs = {k: [] for k in ("gdn_s_p", "gdn_c_p", "gdn_s_s", "gdn_c_s", "fox_k_p", "fox_v_p", "fox_lf_p",
                            "fox_k_s", "fox_v_s", "fox_lf_s", "dsa_k_p", "dsa_v_p", "dsa_ki_p",
                            "dsa_k_s", "dsa_v_s", "dsa_ki_s", "ffn_c_p", "ffn_c_s")}
    pos_sq = n_past + jnp.arange(ns)
    pos_sk = jnp.arange(n_past + ns)

    for i in range(depth):
        kind, j = i % 3, i // 3
        for tag, gr in groups.items():
            nb, nn = gr["nb"], gr["n"]
            if kind == 0:
                qkv = proj(gr, gdn_w_in, j, 0, conv_dim).reshape(nb, nn, conv_dim)
                z = proj(gr, gdn_w_in, j, conv_dim, val_dim).reshape(nb, nn, val_dim)
                ba = tail_proj(gr, gdn_w_in, j, conv_dim + val_dim, 128).reshape(nb, nn, 128)
                bt, at = ba[..., :GDN_V_HEADS], ba[..., GDN_V_HEADS:2 * GDN_V_HEADS]
                if tag == "p":
                    conv_buf = jnp.zeros((nb, gdn_conv_w.shape[1] - 1, conv_dim), F32)
                    s0 = jnp.zeros((nb, GDN_V_HEADS, GDN_HEAD, GDN_HEAD), F32)
                    o, s_new = gdn_mixer(qkv, z, ba, conv_buf, s0, gdn_conv_w, gdn_a_log, gdn_dt_bias, gdn_norm_w, j,
                                         blk=512, hg=4, cpi=4)
                    c_new = qkv[:, nn - (gdn_conv_w.shape[1] - 1):]
                else:
                    conv_buf, s0 = state_gdn_conv[j], state_gdn[j]
                    o, c_new, s_new = _gdn_core(qkv, z, bt, at, conv_buf, s0, gdn_conv_w[j], gdn_a_log[j],
                                                gdn_dt_bias[j], gdn_norm_w[j])
                    o = o.astype(BF16)
                outs["gdn_s_" + tag].append(s_new)
                outs["gdn_c_" + tag].append(c_new)
                r = out_proj(gr, o.reshape(nb * nn, val_dim), gdn_w_out, j)
            elif kind == 1:
                k = proj(gr, fox_w_in, j, d, d)
                v = proj(gr, fox_w_in, j, 2 * d, d)
                f = tail_proj(gr, fox_w_in, j, 3 * d, 128)[:, :ATT_HEADS].reshape(nb, nn, ATT_HEADS)
                lf = jax.nn.log_sigmoid(f + fox_b_f[j])
                k4, v4 = k.reshape(nb, nn, ATT_HEADS, HEAD_DIM), v.reshape(nb, nn, ATT_HEADS, HEAD_DIM)
                outs["fox_k_" + tag].append(k4)
                outs["fox_v_" + tag].append(v4)
                outs["fox_lf_" + tag].append(lf)
                if tag == "p":
                    q = proj(gr, fox_w_in, j, 0, d, out_dtype=BF16)
                    o = flash_attention(q.reshape(nb, nn, d), k.reshape(nb, nn, d), v.reshape(nb, nn, d),
                                        r=_suffix_exclusive(lf), tq=1024, tk=1024, name="fox_flash")
                    o = o.reshape(nb * nn, d)
                else:
                    q = proj(gr, fox_w_in, j, 0, d).reshape(nb, nn, ATT_HEADS, HEAD_DIM)
                    lf_all = jnp.concatenate([_paged_rows(cache_fox_logf[j], page_table), lf], axis=1)
                    rr = _suffix_exclusive(lf_all)
                    o = paged_attention(q, cache_fox_k, cache_fox_v, j, page_table, k4, v4, pg=4,
                                        r_q=rr[:, n_past:], r_k=rr, name="fox_decode")
                    o = o.reshape(nb * nn, d).astype(BF16)
                r = out_proj(gr, o, fox_w_out, j)
            else:
                k = proj(gr, dsa_w_in, j, d, d)
                v = proj(gr, dsa_w_in, j, 2 * d, d)
                tail = tail_proj(gr, dsa_w_in, j, 3 * d + IDX_HEADS * IDX_DIM, 256)
                ki, _ = layer_norm(tail[:, :IDX_DIM], dsa_idx_ln_g, dsa_idx_ln_b, j, tm=min(gr["tm"], 512),
                                   name="dsa_ki_ln")
                wh = tail[:, IDX_DIM:IDX_DIM + IDX_HEADS] * (IDX_HEADS ** -0.5)
                k4, v4 = k.reshape(nb, nn, ATT_HEADS, HEAD_DIM), v.reshape(nb, nn, ATT_HEADS, HEAD_DIM)
                ki3 = ki.reshape(nb, nn, IDX_DIM)
                outs["dsa_k_" + tag].append(k4)
                outs["dsa_v_" + tag].append(v4)
                outs["dsa_ki_" + tag].append(ki3)
                if tag == "p":
                    q = proj(gr, dsa_w_in, j, 0, d, out_dtype=BF16)
                    qidx = proj(gr, dsa_w_in, j, 3 * d, IDX_HEADS * IDX_DIM, out_dtype=BF16)
                    mask = dsa_select(qidx.reshape(nb, nn, IDX_HEADS * IDX_DIM), ki3, wh.reshape(nb, nn, IDX_HEADS),
                                      tq=256, tk=512, topk=min(TOPK_MAX, nn // 4))
                    o = flash_attention(q.reshape(nb, nn, d), k.reshape(nb, nn, d), v.reshape(nb, nn, d),
                                        mask=mask, tq=1024, tk=1024, name="dsa_flash")
                    o = o.reshape(nb * nn, d)
                else:
                    q = proj(gr, dsa_w_in, j, 0, d).reshape(nb, nn, ATT_HEADS, HEAD_DIM)
                    qidx = proj(gr, dsa_w_in, j, 3 * d, IDX_HEADS * IDX_DIM).reshape(nb, nn, IDX_HEADS, IDX_DIM)
                    sel = paged_select(qidx, wh.reshape(nb, nn, IDX_HEADS), cache_dsa_kidx, j, page_table, ki3,
                                       topk=min(TOPK_MAX, (n_past + nn) // 4), pg=16, name="dsa_decode_select")
                    o = paged_attention(q, cache_dsa_k, cache_dsa_v, j, page_table, k4, v4, pg=4, mask=sel,
                                        name="dsa_decode")
                    o = o.reshape(nb * nn, d).astype(BF16)
                r = out_proj(gr, o, dsa_w_out, j)

            ln_tm = min(gr["tm"], 512)
            gr["xf"], gr["xb"] = layer_norm(r, ln_mix_g, ln_mix_b, i, tm=ln_tm, name="ln_mix")

            if tag == "p":
                act, c_new = ffn_up(gr["xb"], jnp.zeros((nb, 2, f2), F32), ffn_w_up, ffn_conv_w, ffn_conv_b, i,
                                    seq_len=nn, tm=gr["tm"], tn=512)
            else:
                hcur = proj(gr, ffn_w_up, i, 0, f2).reshape(nb, nn, f2)
                ext = jnp.concatenate([state_ffn_conv[i], hcur], axis=1)
                c_new = ext[:, -(ffn_conv_w.shape[1] - 1):]
                hc = _causal_dwconv(ext, ffn_conv_w[i]) + ffn_conv_b[i]
                gate, val = jnp.split(hc, [f2 // 2], axis=-1)
                act = (jax.nn.silu(gate) * val).reshape(nb * nn, f2 // 2).astype(BF16)
            outs["ffn_c_" + tag].append(c_new)
            r = out_proj(gr, act, ffn_w_down, i, name="ffn_down")
            x2f, x2b = layer_norm(r, ln_ffn_g, ln_ffn_b, i, tm=ln_tm, name="ln_ffn")
            p_in = (p_prompt if tag == "p" else p_sample)[i].reshape(nb * nn, -1).astype(BF16)
            gr["xf"], gr["xb"] = per_layer_embed(x2f, x2b, p_in, ple_w_proj, ple_w_gate, i, tm=gr["tm"],
                                                 tn=_tile_n(d, 0, gr["tn_cap"]))

    st = lambda key: jnp.stack(outs[key])
    return (groups["p"]["xf"].reshape(b, n, d), groups["s"]["xf"].reshape(db, ns, d),
            st("gdn_s_p"), st("gdn_c_p"), st("fox_k_p"), st("fox_v_p"), st("fox_lf_p"),
            st("dsa_k_p"), st("dsa_v_p"), st("dsa_ki_p"), st("ffn_c_p"),
            st("gdn_s_s"), st("gdn_c_s"), st("fox_k_s"), st("fox_v_s"), st("fox_lf_s"),
            st("dsa_k_s"), st("dsa_v_s"), st("dsa_ki_s"), st("ffn_c_s"))
```

```python
import functools
import math

import jax
import jax.numpy as jnp
from jax import lax
from jax.experimental import pallas as pl
from jax.experimental.pallas import tpu as pltpu

F32 = jnp.float32
BF16 = jnp.bfloat16

PAGE = 128
GDN_QK_HEADS = 16
GDN_V_HEADS = 32
GDN_HEAD = 128
GDN_CHUNK = 64
ATT_HEADS = 16
HEAD_DIM = 128
IDX_HEADS = 16
IDX_DIM = 128
TOPK_MAX = 256
LN_EPS = 1e-5
RMS_EPS = 1e-6
L2_EPS = 1e-6

V7X_VMEM_LIMIT_BYTES = 56 * 1024 * 1024
NEG_BIG = -1e30
INT_MIN = -(2 ** 31)


def _params(*sem):
    return pltpu.CompilerParams(dimension_semantics=sem, vmem_limit_bytes=V7X_VMEM_LIMIT_BYTES)


def _dot(a, b, **kw):
    return jnp.dot(a, b, preferred_element_type=F32, **kw)


def _dot_nt(a, b, **kw):
    return lax.dot_general(a, b, (((1,), (1,)), ((), ())), preferred_element_type=F32, **kw)


def _mm_body(*refs, nk, has_res, alpha):
    if has_res:
        x_ref, w_ref, r_ref, o_ref, *scratch = refs
    else:
        x_ref, w_ref, o_ref, *scratch = refs
        r_ref = None

    def finish(acc):
        if has_res:
            acc = alpha * r_ref[...] + acc
        o_ref[...] = acc.astype(o_ref.dtype)

    part = jnp.dot(x_ref[...], w_ref[...].astype(BF16), preferred_element_type=F32)
    if nk == 1:
        finish(part)
        return
    acc_ref, = scratch
    k = pl.program_id(2)

    @pl.when(k == 0)
    def _():
        acc_ref[...] = part

    @pl.when(k > 0)
    def _():
        acc_ref[...] += part

    @pl.when(k == nk - 1)
    def _():
        finish(acc_ref[...])


def matmul(x, w, layer, *, col0=0, n=None, tm, tn, tk=None, out_dtype=F32, res=None, alpha=None, name="mm"):
    m, kdim = x.shape
    assert x.dtype == BF16 and w.shape[1] == kdim
    n = w.shape[2] - col0 if n is None else n
    tk = kdim if tk is None else tk
    assert m % tm == 0 and n % tn == 0 and kdim % tk == 0 and col0 % tn == 0
    nk = kdim // tk
    cb0 = col0 // tn
    in_specs = [
        pl.BlockSpec((tm, tk), lambda i, j, k: (i, k)),
        pl.BlockSpec((None, tk, tn), lambda i, j, k: (layer, k, cb0 + j)),
    ]
    args = [x, w]
    if res is not None:
        assert res.shape == (m, n)
        in_specs.append(pl.BlockSpec((tm, tn), lambda i, j, k: (i, j)))
        args.append(res)
    return pl.pallas_call(
        functools.partial(_mm_body, nk=nk, has_res=res is not None, alpha=alpha),
        out_shape=jax.ShapeDtypeStruct((m, n), out_dtype),
        grid=(m // tm, n // tn, nk),
        in_specs=in_specs,
        out_specs=pl.BlockSpec((tm, tn), lambda i, j, k: (i, j)),
        scratch_shapes=[pltpu.VMEM((tm, tn), F32)] if nk > 1 else [],
        compiler_params=_params("parallel", "parallel", "arbitrary"),
        name=name,
    )(*args)


def _ln_body(x_ref, g_ref, b_ref, of_ref, ob_ref, *, eps):
    x = x_ref[...]
    mu = jnp.mean(x, -1, keepdims=True)
    xc = x - mu
    var = jnp.mean(xc * xc, -1, keepdims=True)
    y = xc * lax.rsqrt(var + eps) * g_ref[...] + b_ref[...]
    of_ref[...] = y
    ob_ref[...] = y.astype(BF16)


def layer_norm(x, g, b, layer, *, tm, eps=LN_EPS, name="ln"):
    m, n = x.shape
    assert m % tm == 0
    g3 = g.reshape(g.shape[0], 1, n)
    b3 = b.reshape(b.shape[0], 1, n)
    vec = pl.BlockSpec((None, 1, n), lambda i: (layer, 0, 0))
    row = pl.BlockSpec((tm, n), lambda i: (i, 0))
    return pl.pallas_call(
        functools.partial(_ln_body, eps=eps),
        out_shape=(jax.ShapeDtypeStruct((m, n), F32), jax.ShapeDtypeStruct((m, n), BF16)),
        grid=(m // tm,),
        in_specs=[row, vec, vec],
        out_specs=(row, row),
        compiler_params=_params("parallel"),
        name=name,
    )(x, g3, b3)


def _ple_body(xb_ref, pb_ref, wg_ref, wp_ref, xr_ref, of_ref, ob_ref):
    gate = jnp.dot(xb_ref[...], wg_ref[...].astype(BF16), preferred_element_type=F32)
    proj = jnp.dot(pb_ref[...], wp_ref[...].astype(BF16), preferred_element_type=F32)
    y = xr_ref[...] + proj * jax.nn.sigmoid(gate)
    of_ref[...] = y
    ob_ref[...] = y.astype(BF16)


def per_layer_embed(x_f, x_b, p_b, w_proj, w_gate, layer, *, tm, tn, name="ple"):
    m, d = x_f.shape
    pd = p_b.shape[1]
    assert m % tm == 0 and d % tn == 0
    out = pl.BlockSpec((tm, tn), lambda i, j: (i, j))
    return pl.pallas_call(
        _ple_body,
        out_shape=(jax.ShapeDtypeStruct((m, d), F32), jax.ShapeDtypeStruct((m, d), BF16)),
        grid=(m // tm, d // tn),
        in_specs=[
            pl.BlockSpec((tm, d), lambda i, j: (i, 0)),
            pl.BlockSpec((tm, pd), lambda i, j: (i, 0)),
            pl.BlockSpec((None, d, tn), lambda i, j: (layer, 0, j)),
            pl.BlockSpec((None, pd, tn), lambda i, j: (layer, 0, j)),
            out,
        ],
        out_specs=(out, out),
        compiler_params=_params("parallel", "parallel"),
        name=name,
    )(x_b, p_b, w_gate, w_proj, x_f)


HALO_ROWS = 16


def _ffn_up_body(x_ref, xh_ref, wg_ref, wv_ref, cwg_ref, cwv_ref, cbg_ref, cbv_ref, hg_ref, hv_ref,
                 act_ref, tg_ref, tv_ref, hs_ref, *, tiles_per_seq, tm):
    first = (pl.program_id(0) % tiles_per_seq) == 0
    x = x_ref[...]
    xh = xh_ref[...]
    conv = []
    for w_ref, cw_ref, cb_ref, hist_ref, tail_ref in ((wg_ref, cwg_ref, cbg_ref, hg_ref, tg_ref),
                                                     (wv_ref, cwv_ref, cbv_ref, hv_ref, tv_ref)):
        w = w_ref[...].astype(BF16)
        h = jnp.dot(x, w, preferred_element_type=F32)
        halo = jnp.dot(xh, w, preferred_element_type=F32)
        hs_ref[0:8, :] = jnp.where(first, hist_ref[...], halo[HALO_ROWS - 8:, :])
        hs_ref[8:8 + tm, :] = h
        cw = cw_ref[...]
        c = hs_ref[6:6 + tm, :] * cw[0:1, :] + hs_ref[7:7 + tm, :] * cw[1:2, :] + h * cw[2:3, :] + cb_ref[...]
        conv.append(c)
        tail_ref[...] = hs_ref[8 + tm - 2:8 + tm, :]
    act_ref[...] = (jax.nn.silu(conv[0]) * conv[1]).astype(act_ref.dtype)


def ffn_up(x_b, hist, w_up, conv_w, conv_b, layer, *, seq_len, tm, tn, name="ffn_up"):
    m, d = x_b.shape
    f2 = w_up.shape[2]
    f = f2 // 2
    nb = m // seq_len
    assert seq_len % tm == 0 and f % tn == 0 and tm % HALO_ROWS == 0
    tiles_per_seq = seq_len // tm
    nj = f // tn
    hist8 = jnp.concatenate([jnp.zeros((nb, 6, f2), F32), hist.astype(F32)], axis=1)
    cw = conv_w
    cb = conv_b.reshape(conv_b.shape[0], 1, f2)
    halo_blocks = tm // HALO_ROWS

    def wspec(off):
        return pl.BlockSpec((None, d, tn), lambda i, j: (layer, 0, off + j))

    def cwspec(off):
        return pl.BlockSpec((None, cw.shape[1], tn), lambda i, j: (layer, 0, off + j))

    def cbspec(off):
        return pl.BlockSpec((None, 1, tn), lambda i, j: (layer, 0, off + j))

    def hspec(off):
        return pl.BlockSpec((None, 8, tn), lambda i, j: (i // tiles_per_seq, 0, off + j))

    tail = pl.BlockSpec((None, 2, tn), lambda i, j: (i, 0, j))
    act, tail_g, tail_v = pl.pallas_call(
        functools.partial(_ffn_up_body, tiles_per_seq=tiles_per_seq, tm=tm),
        out_shape=(jax.ShapeDtypeStruct((m, f), BF16),
                   jax.ShapeDtypeStruct((m // tm, 2, f), F32), jax.ShapeDtypeStruct((m // tm, 2, f), F32)),
        grid=(m // tm, nj),
        in_specs=[
            pl.BlockSpec((tm, d), lambda i, j: (i, 0)),
            pl.BlockSpec((HALO_ROWS, d), lambda i, j: (jnp.maximum(i * halo_blocks - 1, 0), 0)),
            wspec(0), wspec(nj), cwspec(0), cwspec(nj), cbspec(0), cbspec(nj), hspec(0), hspec(nj),
        ],
        out_specs=(pl.BlockSpec((tm, tn), lambda i, j: (i, j)), tail, tail),
        scratch_shapes=[pltpu.VMEM((tm + 8, tn), F32)],
        compiler_params=_params("arbitrary", "arbitrary"),
        name=name,
    )(x_b, x_b, w_up, w_up, cw, cw, cb, cb, hist8, hist8)
    last = slice(tiles_per_seq - 1, None, tiles_per_seq)
    return act, jnp.concatenate([tail_g[last], tail_v[last]], axis=-1)


def _flash_body(*refs, tq, tk, scale, has_bias, has_mask):
    refs = list(refs)
    q_ref, k_ref, v_ref = refs[:3]
    rest = refs[3:]
    if has_bias:
        rq_ref, rk_ref = rest[:2]
        rest = rest[2:]
    if has_mask:
        mask_ref = rest[0]
        rest = rest[1:]
    if has_bias:
        o_ref, m_ref, l_ref, acc_ref, rqb_ref = rest
    else:
        o_ref, m_ref, l_ref, acc_ref = rest
    qi = pl.program_id(2)
    lanes = m_ref.shape[1]
    nblk = tk // lanes
    m_ref[...] = jnp.full(m_ref.shape, NEG_BIG, F32)
    l_ref[...] = jnp.zeros(l_ref.shape, F32)
    acc_ref[...] = jnp.zeros(acc_ref.shape, F32)
    if has_bias:
        rqb_ref[...] = jnp.broadcast_to(rq_ref[...], rqb_ref.shape)
    tri = lax.broadcasted_iota(jnp.int32, (tq, lanes), 1) - lax.broadcasted_iota(jnp.int32, (tq, lanes), 0)

    def chunk(c, diagonal):
        off = pl.multiple_of(c * tk, tk)
        kc = k_ref[pl.ds(off, tk), :].astype(BF16)
        vc = v_ref[pl.ds(off, tk), :].astype(BF16)
        s = lax.dot_general(q_ref[...], kc, (((1,), (1,)), ((), ())), preferred_element_type=F32) * scale
        blocks = [s[:, j * lanes:(j + 1) * lanes] for j in range(nblk)]
        if has_bias:
            rk = rk_ref[c]
            rqb = rqb_ref[...]
            blocks = [blocks[j] + rk[:, j * lanes:(j + 1) * lanes] - rqb for j in range(nblk)]
        if has_mask:
            pieces = tk // mask_ref.shape[-1]
            keep = jnp.concatenate([mask_ref[c * pieces + i] for i in range(pieces)], axis=1).astype(jnp.int32)
            blocks = [jnp.where(keep[:, j * lanes:(j + 1) * lanes] != 0, blocks[j], NEG_BIG) for j in range(nblk)]
        elif diagonal:
            blocks = [jnp.where(tri + j * lanes <= 0, blocks[j], NEG_BIG) for j in range(nblk)]
        bmax = blocks[0]
        for blk in blocks[1:]:
            bmax = jnp.maximum(bmax, blk)
        m_old = m_ref[...]
        m_new = jnp.maximum(m_old, jnp.max(bmax, -1, keepdims=True))
        a = jnp.exp(m_old - m_new)
        ps = [jnp.exp(blk - m_new) for blk in blocks]
        psum = ps[0]
        for pj in ps[1:]:
            psum = psum + pj
        l_ref[...] = a * l_ref[...] + psum
        p = jnp.concatenate(ps, axis=1).astype(BF16)
        acc_ref[...] = a * acc_ref[...] + jnp.dot(p, vc, preferred_element_type=F32)
        m_ref[...] = m_new

    def full_chunk(c, carry):
        chunk(c, False)
        return carry

    if has_mask:
        lax.fori_loop(0, qi + 1, full_chunk, 0)
    else:
        lax.fori_loop(0, qi, full_chunk, 0)
        chunk(qi, True)
    o_ref[...] = (acc_ref[...] / jnp.sum(l_ref[...], -1, keepdims=True)).astype(o_ref.dtype)


def flash_attention(q, k, v, *, r=None, mask=None, tq, tk, name="flash"):
    b, n, hd = q.shape
    h = hd // HEAD_DIM
    assert n % tq == 0 and tq == tk and tk % HEAD_DIM == 0
    nkc = n // tk
    in_specs = [
        pl.BlockSpec((None, tq, HEAD_DIM), lambda bi, hi, qi: (bi, qi, hi)),
        pl.BlockSpec((None, n, HEAD_DIM), lambda bi, hi, qi: (bi, 0, hi)),
        pl.BlockSpec((None, n, HEAD_DIM), lambda bi, hi, qi: (bi, 0, hi)),
    ]
    args = [q, k, v]
    if r is not None:
        rt = r.transpose(0, 2, 1)
        in_specs.append(pl.BlockSpec((None, None, tq, 1), lambda bi, hi, qi: (bi, hi, qi, 0)))
        in_specs.append(pl.BlockSpec((None, None, nkc, 1, tk), lambda bi, hi, qi: (bi, hi, 0, 0, 0)))
        args += [rt.reshape(b, h, n, 1), rt.reshape(b, h, nkc, 1, tk)]
    if mask is not None:
        mtk = mask.shape[3]
        assert tk % mtk == 0 and mask.shape[1] * mtk == n
        in_specs.append(pl.BlockSpec((None, n // mtk, tq, mtk), lambda bi, hi, qi: (bi, 0, qi, 0)))
        args.append(mask)
    return pl.pallas_call(
        functools.partial(_flash_body, tq=tq, tk=tk, scale=HEAD_DIM ** -0.5,
                          has_bias=r is not None, has_mask=mask is not None),
        out_shape=jax.ShapeDtypeStruct((b, n, hd), BF16),
        grid=(b, h, n // tq),
        in_specs=in_specs,
        out_specs=pl.BlockSpec((None, tq, HEAD_DIM), lambda bi, hi, qi: (bi, qi, hi)),
        scratch_shapes=[pltpu.VMEM((tq, HEAD_DIM), F32)] * (4 if r is not None else 3),
        compiler_params=_params("parallel", "parallel", "arbitrary"),
        name=name,
    )(*args)


def _dsa_select_body(q_ref, kidx_ref, wh_ref, mask_ref, keys_ref, *, tq, tk, topk, n_keys):
    qi = pl.program_id(1)
    n_valid = ((qi + 1) * tq + tk - 1) // tk
    whs = wh_ref[...] * (IDX_DIM ** -0.5)
    q_pos = qi * tq + lax.broadcasted_iota(jnp.int32, (tq, tk), 0)
    k_iota = lax.broadcasted_iota(jnp.int32, (tq, tk), 1)
    n_heads = wh_ref.shape[-1]
    lanes = 128

    def score_chunk(c, carry):
        off = pl.multiple_of(c * tk, tk)
        kc = kidx_ref[pl.ds(off, tk), :].astype(BF16)
        acc = jnp.zeros((tq, tk), F32)
        for h in range(n_heads):
            d = lax.dot_general(q_ref[:, h * IDX_DIM:(h + 1) * IDX_DIM], kc, (((1,), (1,)), ((), ())),
                                preferred_element_type=F32)
            acc = acc + jnp.maximum(d, 0.0) * whs[:, h:h + 1]
        bits = pltpu.bitcast(acc, jnp.int32)
        key = bits ^ ((bits >> 31) & 0x7FFFFFFF)
        keys_ref[c] = jnp.where((off + k_iota) <= q_pos, key, INT_MIN)
        return carry

    lax.fori_loop(0, n_valid, score_chunk, 0)

    def count(pred):
        def body(c, part):
            ind = jnp.where(pred(keys_ref[c], c * tk), 1.0, 0.0)
            for s in range(tk // lanes):
                part = part + ind[:, s * lanes:(s + 1) * lanes]
            return part
        part = lax.fori_loop(0, n_valid, body, jnp.zeros((tq, lanes), F32))
        return jnp.sum(part, -1, keepdims=True)

    kf = float(topk)

    def value_bit(it, t):
        cand = t + lax.shift_left(jnp.int32(1), 31 - it)
        cnt = count(lambda kk, off: kk >= cand)
        return jnp.where(cnt >= kf, cand, t)

    thr = lax.fori_loop(0, 32, value_bit, jnp.full((tq, 1), INT_MIN, jnp.int32))
    cnt_ge = count(lambda kk, off: kk >= thr)
    cnt_gt = count(lambda kk, off: kk > thr)
    need = kf - cnt_gt
    idx_bits = int(n_keys).bit_length()

    def tie_search(_):
        def index_bit(it, p):
            cand = p + lax.shift_left(jnp.int32(1), idx_bits - 1 - it)
            cnt = count(lambda kk, off: (kk == thr) & ((off + k_iota) < cand))
            return jnp.where(cnt < need, cand, p)
        return lax.fori_loop(0, idx_bits, index_bit, jnp.zeros((tq, 1), jnp.int32))

    excess = jnp.max(cnt_ge - kf) > 0.0
    last_tie = lax.cond(excess, tie_search, lambda _: jnp.full((tq, 1), n_keys, jnp.int32), 0)

    mask_ref[...] = jnp.zeros(mask_ref.shape, mask_ref.dtype)

    def emit(c, carry):
        off = c * tk
        kk = keys_ref[c]
        k_pos = off + k_iota
        sel = (kk > thr) | ((kk == thr) & (k_pos <= last_tie))
        sel = sel & (k_pos <= q_pos)
        mask_ref[c] = jnp.where(sel, 1, 0).astype(mask_ref.dtype)
        return carry

    lax.fori_loop(0, n_valid, emit, 0)


def dsa_select(qidx, kidx, wh, *, tq, tk, topk, name="dsa_select"):
    b, n, _ = qidx.shape
    assert n % tq == 0 and n % tk == 0
    nkc = n // tk
    return pl.pallas_call(
        functools.partial(_dsa_select_body, tq=tq, tk=tk, topk=topk, n_keys=n),
        out_shape=jax.ShapeDtypeStruct((b, nkc, n, tk), jnp.int8),
        grid=(b, n // tq),
        in_specs=[
            pl.BlockSpec((None, tq, qidx.shape[2]), lambda bi, qi: (bi, qi, 0)),
            pl.BlockSpec((None, n, kidx.shape[2]), lambda bi, qi: (bi, 0, 0)),
            pl.BlockSpec((None, tq, wh.shape[2]), lambda bi, qi: (bi, qi, 0)),
        ],
        out_specs=pl.BlockSpec((None, nkc, tq, tk), lambda bi, qi: (bi, 0, qi, 0)),
        scratch_shapes=[pltpu.VMEM((nkc, tq, tk), jnp.int32)],
        compiler_params=_params("parallel", "arbitrary"),
        name=name,
    )(qidx, kidx, wh)


def _paged_attn_body(*refs, n_steps, pg, nq, n_heads, scale, has_bias, has_mask):
    refs = list(refs)
    pt_ref, q_ref, hm_ref = refs[:3]
    k_refs = refs[3:3 + pg]
    v_refs = refs[3 + pg:3 + 2 * pg]
    kn_ref, vn_ref = refs[3 + 2 * pg:5 + 2 * pg]
    rest = refs[5 + 2 * pg:]
    if has_bias:
        rq_ref, rk_ref = rest[:2]
        rest = rest[2:]
    if has_mask:
        mask_ref, expand_ref = rest[:2]
        rest = rest[2:]
    if has_bias:
        o_ref, m_s, l_s, acc_s, rqb_s = rest
    else:
        o_ref, m_s, l_s, acc_s = rest
    p = pl.program_id(1)
    rows = nq * n_heads
    cols = k_refs[0].shape[0]
    lanes = m_s.shape[1]
    nblk = cols // lanes
    assert lanes % n_heads == 0

    @pl.when(p == 0)
    def _():
        m_s[...] = jnp.full(m_s.shape, NEG_BIG, F32)
        l_s[...] = jnp.zeros(l_s.shape, F32)
        acc_s[...] = jnp.zeros(acc_s.shape, F32)
        if has_bias:
            rqb_s[...] = jnp.broadcast_to(rq_ref[...], rqb_s.shape)

    def attend(k_srcs, v_srcs, slots, new_tokens):
        q = q_ref[...]
        hm = hm_ref[...]
        blocks = []
        for k_src, slot in zip(k_srcs, slots):
            s = _dot_nt(q, k_src[...].astype(BF16)) * scale
            if has_mask:
                mk = mask_ref[slot]
                mk = jnp.concatenate([jnp.broadcast_to(mk[i:i + 1, :], (n_heads, mk.shape[1])) for i in range(nq)], axis=0)
                sel = _dot(mk.astype(BF16), expand_ref[...])
            for j in range(nblk):
                blk = s[:, j * lanes:(j + 1) * lanes]
                if has_bias:
                    blk = blk + rk_ref[slot][:, j * lanes:(j + 1) * lanes] - rqb_s[...]
                if has_mask:
                    blk = jnp.where(sel[:, j * lanes:(j + 1) * lanes] > 0.5, blk, NEG_BIG)
                elif new_tokens:
                    q_idx = lax.broadcasted_iota(jnp.int32, (rows, lanes), 0) // n_heads
                    t_idx = (j * lanes + lax.broadcasted_iota(jnp.int32, (rows, lanes), 1)) // n_heads
                    blk = jnp.where((t_idx <= q_idx) & (t_idx < nq), blk, NEG_BIG)
                blocks.append(blk + hm)
        bmax = blocks[0]
        for blk in blocks[1:]:
            bmax = jnp.maximum(bmax, blk)
        m_old = m_s[...]
        m_new = jnp.maximum(m_old, jnp.max(bmax, -1, keepdims=True))
        a = jnp.exp(m_old - m_new)
        ps = [jnp.exp(blk - m_new) for blk in blocks]
        psum = ps[0]
        for pj in ps[1:]:
            psum = psum + pj
        l_s[...] = a * l_s[...] + psum
        acc = a * acc_s[...]
        for i, v_src in enumerate(v_srcs):
            pr = jnp.concatenate(ps[i * nblk:(i + 1) * nblk], axis=1).astype(BF16)
            acc = acc + _dot(pr, v_src[...].astype(BF16))
        acc_s[...] = acc
        m_s[...] = m_new

    @pl.when(p < n_steps - 1)
    def _():
        attend(k_refs, v_refs, list(range(pg)), False)

    @pl.when(p == n_steps - 1)
    def _():
        attend([kn_ref], [vn_ref], [0], True)
        o_ref[...] = acc_s[...] / jnp.sum(l_s[...], -1, keepdims=True)


def paged_attention(q, pool_k, pool_v, layer, page_table, k_new, v_new, *, pg, r_q=None, r_k=None, mask=None,
                    name="paged_attn"):
    b, nq, h, d = q.shape
    n_pool, page = pool_k.shape[1], pool_k.shape[2]
    n_pages = page_table.shape[1]
    assert n_pages % pg == 0 and nq <= 8
    n_steps = n_pages // pg + 1
    rows, cols = nq * h, page * h
    pk = pool_k.reshape(pool_k.shape[0], n_pool, cols, d)
    pv = pool_v.reshape(pool_v.shape[0], n_pool, cols, d)
    pad = lambda a: jnp.pad(a, ((0, 0), (0, page - nq), (0, 0), (0, 0))).reshape(b, cols, d)
    ri = lax.broadcasted_iota(jnp.int32, (rows, d), 0) % h
    ci = lax.broadcasted_iota(jnp.int32, (rows, d), 1) % h
    head_match = jnp.where(ri == ci, 0.0, NEG_BIG).astype(F32)
    last = n_pages - 1

    def pool_spec(i):
        return pl.BlockSpec((None, None, cols, d),
                            lambda bi, p, pt: (layer, pt[bi, jnp.minimum(p * pg + i, last)], 0, 0))

    const2 = lambda shape: pl.BlockSpec(shape, lambda bi, p, pt: (0, 0))
    per_b = lambda shape: pl.BlockSpec((None,) + shape, lambda bi, p, pt: (bi,) + (0,) * len(shape))
    in_specs = ([per_b((rows, d)), const2((rows, d))] + [pool_spec(i) for i in range(pg)] * 2
                + [per_b((cols, d)), per_b((cols, d))])
    args = [q.reshape(b, rows, d).astype(BF16), head_match] + [pk] * pg + [pv] * pg + [pad(k_new), pad(v_new)]
    if r_q is not None:
        n_tot = r_k.shape[1]
        rk = jnp.pad(r_k, ((0, 0), (0, n_pages * page + pg * page - n_tot), (0, 0)))
        rk = rk.reshape(b, n_steps, pg, 1, cols)
        in_specs += [per_b((rows, 1)), pl.BlockSpec((None, None, pg, 1, cols), lambda bi, p, pt: (bi, p, 0, 0, 0))]
        args += [r_q.reshape(b, rows, 1), rk]
    if mask is not None:
        expand = (lax.broadcasted_iota(jnp.int32, (page, cols), 1) // h
                  == lax.broadcasted_iota(jnp.int32, (page, cols), 0)).astype(BF16)
        in_specs += [pl.BlockSpec((None, None, pg, 8, page), lambda bi, p, pt: (bi, p, 0, 0, 0)), const2((page, cols))]
        assert mask.shape[1] >= n_steps * pg
        args += [mask[:, :n_steps * pg].reshape(b, n_steps, pg, 8, page), expand]
    out = pl.pallas_call(
        functools.partial(_paged_attn_body, n_steps=n_steps, pg=pg, nq=nq, n_heads=h, scale=d ** -0.5,
                          has_bias=r_q is not None, has_mask=mask is not None),
        out_shape=jax.ShapeDtypeStruct((b, rows, d), F32),
        grid_spec=pltpu.PrefetchScalarGridSpec(
            num_scalar_prefetch=1,
            grid=(b, n_steps),
            in_specs=in_specs,
            out_specs=per_b((rows, d)),
            scratch_shapes=[pltpu.VMEM((rows, d), F32)] * (4 if r_q is not None else 3),
        ),
        compiler_params=_params("parallel", "arbitrary"),
        name=name,
    )(page_table, *args)
    return out.reshape(b, nq, h * d)


def _paged_select_body(*refs, n_steps, pg, nq, n_heads, topk, n_keys):
    pt_ref, q_ref, wh_ref = refs[:3]
    kidx_refs = refs[3:3 + pg]
    kin_ref, mask_ref, keys_s = refs[3 + pg:]
    p = pl.program_id(1)
    page = kin_ref.shape[0]
    lane = lax.broadcasted_iota(jnp.int32, (8, page), 1)
    q_row = lax.broadcasted_iota(jnp.int32, (8, page), 0)
    n_slots = n_steps * pg

    def score_page(src):
        d = _dot_nt(q_ref[...], src[...].astype(BF16))
        rel = jnp.maximum(d, 0.0) * (wh_ref[...] * (IDX_DIM ** -0.5))
        sc = jnp.sum(rel.reshape(nq, n_heads, page), axis=1)
        sc = jnp.concatenate([sc, jnp.zeros((8 - nq, page), F32)], axis=0)
        bits = pltpu.bitcast(sc, jnp.int32)
        return bits ^ ((bits >> 31) & 0x7FFFFFFF)

    @pl.when(p < n_steps - 1)
    def _():
        for i in range(pg):
            keys_s[p * pg + i] = score_page(kidx_refs[i])

    @pl.when(p == n_steps - 1)
    def _():
        first_new = (n_steps - 1) * pg
        keys_s[first_new] = jnp.where((lane <= q_row) & (lane < nq), score_page(kin_ref), INT_MIN)
        for i in range(1, pg):
            keys_s[first_new + i] = jnp.full((8, page), INT_MIN, jnp.int32)

        keys = keys_s[...]
        pos = (lax.broadcasted_iota(jnp.int32, keys.shape, 0) * page
               + lax.broadcasted_iota(jnp.int32, keys.shape, 2))

        def count(pred):
            return jnp.sum(jnp.sum(jnp.where(pred, 1.0, 0.0), axis=0), -1, keepdims=True)

        kf = float(topk)

        def value_bit(it, t):
            cand = t + lax.shift_left(jnp.int32(1), 31 - it)
            return jnp.where(count(keys >= cand) >= kf, cand, t)

        thr = lax.fori_loop(0, 32, value_bit, jnp.full((8, 1), INT_MIN, jnp.int32))
        need = kf - count(keys > thr)
        idx_bits = int(n_keys).bit_length()

        def index_bit(it, pos_max):
            cand = pos_max + lax.shift_left(jnp.int32(1), idx_bits - 1 - it)
            return jnp.where(count((keys == thr) & (pos < cand)) < need, cand, pos_max)

        last_tie = lax.fori_loop(0, idx_bits, index_bit, jnp.zeros((8, 1), jnp.int32))
        sel = (keys > thr) | ((keys == thr) & (pos <= last_tie))
        mask_ref[...] = jnp.where(sel & (keys != INT_MIN), 1.0, 0.0)


def paged_select(qidx, wh, pool_kidx, layer, page_table, kidx_new, *, topk, pg, name="paged_select"):
    b, nq, hi, di = qidx.shape
    page = pool_kidx.shape[2]
    n_pages = page_table.shape[1]
    assert n_pages % pg == 0 and nq <= 8
    n_steps = n_pages // pg + 1
    rows = nq * hi
    last = n_pages - 1
    kin = jnp.pad(kidx_new, ((0, 0), (0, page - nq), (0, 0)))

    def pool_spec(i):
        return pl.BlockSpec((None, None, page, di),
                            lambda bi, p, pt: (layer, pt[bi, jnp.minimum(p * pg + i, last)], 0, 0))

    per_b = lambda shape: pl.BlockSpec((None,) + shape, lambda bi, p, pt: (bi,) + (0,) * len(shape))
    mask = pl.pallas_call(
        functools.partial(_paged_select_body, n_steps=n_steps, pg=pg, nq=nq, n_heads=hi, topk=topk,
                          n_keys=n_pages * page + nq),
        out_shape=jax.ShapeDtypeStruct((b, n_steps * pg, 8, page), F32),
        grid_spec=pltpu.PrefetchScalarGridSpec(
            num_scalar_prefetch=1,
            grid=(b, n_steps),
            in_specs=[per_b((rows, di)), per_b((rows, 1))] + [pool_spec(i) for i in range(pg)] + [per_b((page, di))],
            out_specs=per_b((n_steps * pg, 8, page)),
            scratch_shapes=[pltpu.VMEM((n_steps * pg, 8, page), jnp.int32)],
        ),
        compiler_params=_params("parallel", "arbitrary"),
        name=name,
    )(page_table, qidx.reshape(b, rows, di).astype(BF16), wh.reshape(b, rows, 1), *([pool_kidx] * pg), kin)
    return mask


GDN_HALO = 8


def _gdn_body(q_ref, k_ref, v_ref, z_ref, ba_ref, cwq_ref, cwk_ref, cwv_ref, gate_ref, nw_ref, h0_ref, s0_ref,
              o_ref, sout_ref,
              xq_s, xk_s, xv_s, s_s, u_s, wq_s, qkkt_s, gt_s, *, blk, chunk, n_vh, hg, cpi):
    jg = pl.program_id(1)
    sb = pl.program_id(2)
    n_sb = pl.num_programs(2)
    nch = blk // chunk
    hd = GDN_HEAD
    exact = dict(precision=lax.Precision.HIGHEST)

    @pl.when(sb == 0)
    def _():
        xq_s[0:GDN_HALO, :] = h0_ref[:, 0:hg * hd]
        xk_s[0:GDN_HALO, :] = h0_ref[:, hg * hd:2 * hg * hd]
        xv_s[0:GDN_HALO, :] = h0_ref[:, 2 * hg * hd:]
        s_s[...] = s0_ref[...]

    xq_s[GDN_HALO:, :] = q_ref[...]
    xk_s[GDN_HALO:, :] = k_ref[...]
    xv_s[GDN_HALO:, :] = v_ref[...]

    row = lax.broadcasted_iota(jnp.int32, (chunk, chunk), 0)
    col = lax.broadcasted_iota(jnp.int32, (chunk, chunk), 1)
    lower = row >= col
    strict = row > col
    ltri = jnp.where(lower, 1.0, 0.0).astype(F32)
    lane = lax.broadcasted_iota(jnp.int32, (chunk, 128), 1)
    sel_row = lax.broadcasted_iota(jnp.int32, (8, 128), 0)
    sel_lane = lax.broadcasted_iota(jnp.int32, (8, 128), 1)
    hv0 = 2 * hg * jg
    pick = jnp.where((sel_row < 2 * hg) & (sel_lane == n_vh + hv0 + sel_row), 1.0, 0.0).astype(F32)
    neg_a = -jnp.exp(gate_ref[0:1, :])
    dt_bias = gate_ref[1:2, :]

    def conv_silu(xs_ref, w_ref, r0):
        x = xs_ref[pl.ds(r0, chunk + GDN_HALO), :]
        w = w_ref[...]
        taps = w.shape[0]
        acc = None
        for t in range(taps):
            sh = taps - 1 - t
            xt = x if sh == 0 else pltpu.roll(x, sh, axis=0)
            term = xt[GDN_HALO:, :] * w[t:t + 1, :]
            acc = term if acc is None else acc + term
        return jax.nn.silu(acc)

    def l2n(x):
        return x * lax.rsqrt(jnp.sum(x * x, -1, keepdims=True) + L2_EPS)

    def prep(ci, carry):
        cs = [ci * cpi + u for u in range(cpi)]
        r0s = [pl.multiple_of(c * chunk, chunk) for c in cs]
        q_all = [conv_silu(xq_s, cwq_ref, r0) for r0 in r0s]
        k_all = [conv_silu(xk_s, cwk_ref, r0) for r0 in r0s]
        v_all = [conv_silu(xv_s, cwv_ref, r0) for r0 in r0s]
        ba = [ba_ref[pl.ds(r0, chunk), :] for r0 in r0s]
        beta_all = [jax.nn.sigmoid(x) for x in ba]
        g_all = [neg_a * jax.nn.softplus(x + dt_bias) for x in ba]
        gc_all = [_dot(ltri, g, **exact) for g in g_all]
        gc_rows = [_dot_nt(pick, g, **exact) for g in gc_all]
        pairs = [(u, h) for u in range(cpi) for h in range(hg)]
        units = [(u, e) for u in range(cpi) for e in range(2 * hg)]
        qs = {(u, h): l2n(q_all[u][:, h * hd:(h + 1) * hd]) * (hd ** -0.5) for u, h in pairs}
        ks = {(u, h): l2n(k_all[u][:, h * hd:(h + 1) * hd]) for u, h in pairs}
        k16 = {p: ks[p].astype(BF16) for p in pairs}
        gram = {p: _dot_nt(k16[p], k16[p]) for p in pairs}
        qk_raw = {p: _dot_nt(qs[p].astype(BF16), k16[p]) for p in pairs}
        beta = {(u, e): jnp.sum(jnp.where(lane == hv0 + e, beta_all[u], 0.0), -1, keepdims=True) for u, e in units}
        gcol = {(u, e): jnp.sum(jnp.where(lane == n_vh + hv0 + e, gc_all[u], 0.0), -1, keepdims=True) for u, e in units}
        g_last = {t: gcol[t][chunk - 1:chunk, :] for t in units}
        decay = {(u, e): jnp.where(lower, jnp.exp(jnp.where(lower, gcol[u, e] - gc_rows[u][e:e + 1, :], 0.0)), 0.0)
                 for u, e in units}
        nmat = {(u, e): jnp.where(strict, -(beta[u, e] * gram[u, e // 2] * decay[u, e]), 0.0) for u, e in units}
        cpow = {t: nmat[t].astype(BF16) for t in units}
        for _ in range(max(chunk.bit_length() - 2, 0)):
            cnew = {t: _dot(cpow[t], cpow[t]) for t in units}
            cpow = {t: cnew[t].astype(BF16) for t in units}
            corr = {t: _dot(nmat[t].astype(BF16), cpow[t]) for t in units}
            nmat = {t: nmat[t] + cnew[t] + corr[t] for t in units}
        eg = {t: jnp.exp(gcol[t]) for t in units}
        rhs = {(u, e): jnp.concatenate([v_all[u][:, e * hd:(e + 1) * hd] * beta[u, e],
                                        (ks[u, e // 2] * beta[u, e]) * eg[u, e]], axis=1) for u, e in units}
        sol = {t: rhs[t] + _dot(nmat[t].astype(BF16), rhs[t].astype(BF16)) for t in units}
        for u, e in units:
            c, t = cs[u], (u, e)
            u_s[e, pl.ds(r0s[u], chunk), :] = sol[t][:, :hd]
            wq_s[e, c, 0:chunk, :] = sol[t][:, hd:].astype(BF16)
            wq_s[e, c, chunk:2 * chunk, :] = (qs[u, e // 2] * eg[t]).astype(BF16)
            qkkt_s[e, c, 0:chunk, :] = jnp.where(lower, qk_raw[u, e // 2] * decay[t], 0.0).astype(BF16)
            k_tail = ks[u, e // 2] * jnp.exp(g_last[t] - gcol[t])
            qkkt_s[e, c, chunk:, :] = k_tail.T.astype(BF16)
            gt_s[e, c] = jnp.broadcast_to(jnp.exp(g_last[t]), (8, 128))
        return carry

    lax.fori_loop(0, nch // cpi, prep, 0)

    nw = nw_ref[...]

    def recur(c, carry):
        r0 = pl.multiple_of(c * chunk, chunk)
        heads = range(2 * hg)
        s = [s_s[e] for e in heads]
        ws = [_dot(wq_s[e, c], s[e].astype(BF16)) for e in heads]
        v_new = [u_s[e, pl.ds(r0, chunk), :] - ws[e][0:chunk] for e in heads]
        mix = [_dot(qkkt_s[e, c], v_new[e].astype(BF16)) for e in heads]
        for e in heads:
            s_s[e] = s[e] * gt_s[e, c][0:1, 0:1] + mix[e][chunk:]
            o = ws[e][chunk:] + mix[e][0:chunk]
            o = o * lax.rsqrt(jnp.mean(o * o, -1, keepdims=True) + RMS_EPS) * nw
            o = o * jax.nn.silu(z_ref[pl.ds(r0, chunk), e * hd:(e + 1) * hd])
            o_ref[pl.ds(r0, chunk), e * hd:(e + 1) * hd] = o.astype(o_ref.dtype)
        return carry

    lax.fori_loop(0, nch, recur, 0)

    xq_s[0:GDN_HALO, :] = xq_s[blk:blk + GDN_HALO, :]
    xk_s[0:GDN_HALO, :] = xk_s[blk:blk + GDN_HALO, :]
    xv_s[0:GDN_HALO, :] = xv_s[blk:blk + GDN_HALO, :]

    @pl.when(sb == n_sb - 1)
    def _():
        sout_ref[...] = s_s[...]


def gdn_mixer(qkv, z, ba, conv_hist, s0, conv_w, a_log, dt_bias, norm_w, layer, *, blk, hg, cpi, chunk=GDN_CHUNK,
              name="gdn"):
    b, n, conv_dim = qkv.shape
    hd = GDN_HEAD
    n_qk, n_vh = GDN_QK_HEADS, GDN_V_HEADS
    assert n_vh == 2 * n_qk and n % blk == 0 and blk % chunk == 0 and 2 * n_vh <= ba.shape[2]
    assert n_qk % hg == 0 and 2 * hg <= 8 and (blk // chunk) % cpi == 0
    ng = n_qk // hg
    taps = conv_w.shape[1]
    hist = jnp.concatenate([jnp.zeros((b, GDN_HALO - (taps - 1), conv_dim), F32), conv_hist.astype(F32)], axis=1)
    hq = hist[:, :, :n_qk * hd].reshape(b, GDN_HALO, ng, hg * hd)
    hk = hist[:, :, n_qk * hd:2 * n_qk * hd].reshape(b, GDN_HALO, ng, hg * hd)
    hv = hist[:, :, 2 * n_qk * hd:].reshape(b, GDN_HALO, ng, 2 * hg * hd)
    h0 = jnp.concatenate([hq, hk, hv], axis=-1).transpose(0, 2, 1, 3)
    gate = jnp.zeros((8, ba.shape[2]), F32)
    gate = gate.at[0, n_vh:2 * n_vh].set(a_log[layer]).at[1, n_vh:2 * n_vh].set(dt_bias[layer])
    nw = norm_w.reshape(norm_w.shape[0], 1, hd)
    nch = blk // chunk
    wq, wv = hg * hd, 2 * hg * hd
    o, s_out = pl.pallas_call(
        functools.partial(_gdn_body, blk=blk, chunk=chunk, n_vh=n_vh, hg=hg, cpi=cpi),
        out_shape=(jax.ShapeDtypeStruct((b, n, n_vh * hd), BF16), jax.ShapeDtypeStruct(s0.shape, F32)),
        grid=(b, ng, n // blk),
        in_specs=[
            pl.BlockSpec((None, blk, wq), lambda bi, j, sb: (bi, sb, j)),
            pl.BlockSpec((None, blk, wq), lambda bi, j, sb: (bi, sb, ng + j)),
            pl.BlockSpec((None, blk, wv), lambda bi, j, sb: (bi, sb, ng + j)),
            pl.BlockSpec((None, blk, wv), lambda bi, j, sb: (bi, sb, j)),
            pl.BlockSpec((None, blk, ba.shape[2]), lambda bi, j, sb: (bi, sb, 0)),
            pl.BlockSpec((None, taps, wq), lambda bi, j, sb: (layer, 0, j)),
            pl.BlockSpec((None, taps, wq), lambda bi, j, sb: (layer, 0, ng + j)),
            pl.BlockSpec((None, taps, wv), lambda bi, j, sb: (layer, 0, ng + j)),
            pl.BlockSpec((8, ba.shape[2]), lambda bi, j, sb: (0, 0)),
            pl.BlockSpec((None, 1, hd), lambda bi, j, sb: (layer, 0, 0)),
            pl.BlockSpec((None, None, GDN_HALO, 4 * wq), lambda bi, j, sb: (bi, j, 0, 0)),
            pl.BlockSpec((None, 2 * hg, hd, hd), lambda bi, j, sb: (bi, j, 0, 0)),
        ],
        out_specs=(
            pl.BlockSpec((None, blk, wv), lambda bi, j, sb: (bi, sb, j)),
            pl.BlockSpec((None, 2 * hg, hd, hd), lambda bi, j, sb: (bi, j, 0, 0)),
        ),
        scratch_shapes=[
            pltpu.VMEM((blk + GDN_HALO, wq), F32), pltpu.VMEM((blk + GDN_HALO, wq), F32),
            pltpu.VMEM((blk + GDN_HALO, wv), F32),
            pltpu.VMEM((2 * hg, hd, hd), F32),
            pltpu.VMEM((2 * hg, blk, hd), F32),
            pltpu.VMEM((2 * hg, nch, 2 * chunk, hd), BF16),
            pltpu.VMEM((2 * hg, nch, chunk + hd, chunk), BF16),
            pltpu.VMEM((2 * hg, nch, 8, 128), F32),
        ],
        compiler_params=_params("parallel", "parallel", "arbitrary"),
        name=name,
    )(qkv, qkv, qkv, z, ba, conv_w, conv_w, conv_w, gate, nw, h0, s0)
    return o, s_out


def _l2norm(x):
    return x * lax.rsqrt(jnp.sum(x * x, -1, keepdims=True) + L2_EPS)


def _causal_dwconv(x_ext, w):
    width = w.shape[0]
    n = x_ext.shape[1] - width + 1
    return sum(x_ext[:, j:j + n] * w[j] for j in range(width))


def _chunk_gated_delta(q, k, v, g, beta, s0):
    b, n, h, dk = k.shape
    dv = v.shape[-1]
    c = min(GDN_CHUNK, n)
    pad = (-n) % c
    if pad:
        padf = lambda a: jnp.pad(a, [(0, 0), (0, pad)] + [(0, 0)] * (a.ndim - 2))
        q, k, v, g, beta = padf(q), padf(k), padf(v), padf(g), padf(beta)
    nc = (n + pad) // c
    qh, kh, vh = [a.transpose(0, 2, 1, 3).reshape(b, h, nc, c, a.shape[-1]) for a in (q, k, v)]
    gh, bh = [a.transpose(0, 2, 1).reshape(b, h, nc, c) for a in (g, beta)]
    gc = jnp.cumsum(gh, -1)
    diff = gc[..., :, None] - gc[..., None, :]
    lower = jnp.tril(jnp.ones((c, c), bool))
    strict = jnp.tril(jnp.ones((c, c), bool), -1)
    decay = jnp.where(lower, jnp.exp(jnp.where(lower, diff, 0.0)), 0.0)
    kb = kh * bh[..., None]
    a_mat = jnp.where(strict, jnp.einsum('bhnid,bhnjd->bhnij', kb, kh) * decay, 0.0)
    rhs = jnp.concatenate([vh * bh[..., None], kb * jnp.exp(gc)[..., None]], -1)
    sol = lax.linalg.triangular_solve(a_mat + jnp.eye(c, dtype=a_mat.dtype), rhs,
                                      left_side=True, lower=True, unit_diagonal=True)
    u, w = sol[..., :dv], sol[..., dv:]
    qk = jnp.where(lower, jnp.einsum('bhnid,bhnjd->bhnij', qh, kh) * decay, 0.0)
    q_dec = qh * jnp.exp(gc)[..., None]
    k_tail = kh * jnp.exp(gc[..., -1:] - gc)[..., None]
    g_tot = jnp.exp(gc[..., -1])

    def step(s, xs_n):
        u_n, w_n, qk_n, qd_n, kt_n, gt_n = xs_n
        v_new = u_n - jnp.einsum('bhcd,bhde->bhce', w_n, s)
        o = jnp.einsum('bhcd,bhde->bhce', qd_n, s) + jnp.einsum('bhij,bhje->bhie', qk_n, v_new)
        s = s * gt_n[..., None, None] + jnp.einsum('bhcd,bhce->bhde', kt_n, v_new)
        return s, o

    xs_all = tuple(jnp.moveaxis(a, 2, 0) for a in (u, w, qk, q_dec, k_tail, g_tot))
    s_fin, o = lax.scan(step, s0, xs_all)
    o = jnp.moveaxis(o, 0, 2).reshape(b, h, nc * c, dv)[:, :, :n].transpose(0, 2, 1, 3)
    return o, s_fin


def _gdn_core(qkv, z, bt, at, conv_buf, s0, conv_w, a_log, dt_bias, norm_w):
    b, n, _ = qkv.shape
    key_dim = GDN_QK_HEADS * GDN_HEAD
    ext = jnp.concatenate([conv_buf.astype(qkv.dtype), qkv], axis=1)
    new_buf = ext[:, -(conv_w.shape[0] - 1):]
    qkv = jax.nn.silu(_causal_dwconv(ext, conv_w))
    q, k, v = jnp.split(qkv, [key_dim, 2 * key_dim], axis=-1)
    rep = GDN_V_HEADS // GDN_QK_HEADS
    q = jnp.repeat(_l2norm(q.reshape(b, n, GDN_QK_HEADS, GDN_HEAD)), rep, axis=2) * (GDN_HEAD ** -0.5)
    k = jnp.repeat(_l2norm(k.reshape(b, n, GDN_QK_HEADS, GDN_HEAD)), rep, axis=2)
    v = v.reshape(b, n, GDN_V_HEADS, GDN_HEAD)
    beta = jax.nn.sigmoid(bt)
    g = -jnp.exp(a_log) * jax.nn.softplus(at + dt_bias)
    o, s_fin = _chunk_gated_delta(q, k, v, g, beta, s0)
    o = o * lax.rsqrt(jnp.mean(o * o, -1, keepdims=True) + RMS_EPS) * norm_w
    o = o * jax.nn.silu(z.reshape(b, n, GDN_V_HEADS, GDN_HEAD))
    return o.reshape(b, n, GDN_V_HEADS * GDN_HEAD), new_buf, s_fin


def _suffix_exclusive(logf):
    return lax.cumsum(logf, axis=1, reverse=True) - logf


def _paged_rows(pool, page_table):
    g = pool[page_table]
    return g.reshape((g.shape[0], g.shape[1] * g.shape[2]) + g.shape[3:])


def _tile_n(n, col0, cap):
    for t in (1024, 512, 256, 128):
        if t <= cap and n % t == 0 and col0 % t == 0:
            return t
    raise ValueError((n, col0))


MM_VMEM_BUDGET_BYTES = 44 * 1024 * 1024


def _mm_tiles(tm, kdim, n, col0, has_res):
    def footprint(tn, tk):
        blocks = 2 * (tm * tk * 2 + tk * tn * 4 + tm * tn * 4 * (2 if has_res else 1))
        temps = tk * tn * 2 + tm * tn * 4 + (tm * tn * 4 if tk < kdim else 0)
        return blocks + temps
    for tk in (kdim, kdim // 2, kdim // 4):
        if tk % 128:
            continue
        for tn in (1024, 512, 256, 128):
            if n % tn == 0 and col0 % tn == 0 and footprint(tn, tk) <= MM_VMEM_BUDGET_BYTES:
                return tn, tk
    raise ValueError((tm, kdim, n, col0))


def kernel(x_prompt, x_sample, cache_fox_k, cache_fox_v, cache_fox_logf, cache_dsa_k, cache_dsa_v, cache_dsa_kidx, state_gdn, state_gdn_conv, state_ffn_conv, page_table, p_prompt, p_sample, gdn_w_in, gdn_conv_w, gdn_a_log, gdn_dt_bias, gdn_norm_w, gdn_w_out, fox_w_in, fox_b_f, fox_w_out, dsa_w_in, dsa_idx_ln_g, dsa_idx_ln_b, dsa_w_out, ffn_w_up, ffn_conv_w, ffn_conv_b, ffn_w_down, ln_mix_g, ln_mix_b, ln_ffn_g, ln_ffn_b, ple_w_proj, ple_w_gate):
    b, n, d = x_prompt.shape
    db, ns, _ = x_sample.shape
    depth = ffn_w_up.shape[0]
    n_past = page_table.shape[1] * PAGE
    alpha = (2.0 * depth) ** 0.25
    f2 = ffn_w_up.shape[2]
    key_dim = GDN_QK_HEADS * GDN_HEAD
    val_dim = GDN_V_HEADS * GDN_HEAD
    conv_dim = 2 * key_dim + val_dim

    groups = {
        "p": dict(nb=b, n=n, tm=1024, xf=x_prompt.reshape(b * n, d)),
        "s": dict(nb=db, n=ns, tm=db * ns, xf=x_sample.reshape(db * ns, d)),
    }
    for gr in groups.values():
        gr["xb"] = gr["xf"].astype(BF16)

    def proj(gr, w, layer, col0, ncols, out_dtype=F32, **kw):
        tn, tk = _mm_tiles(gr["tm"], w.shape[1], ncols, col0, has_res=False)
        return matmul(gr["xb"], w, layer, col0=col0, n=ncols, tm=gr["tm"], tn=tn, tk=tk, out_dtype=out_dtype, **kw)

    def tail_proj(gr, w, layer, col0, padded):
        wt = jnp.pad(w[layer, :, col0:], ((0, 0), (0, padded - (w.shape[2] - col0))))[None]
        return matmul(gr["xb"], wt, 0, tm=gr["tm"], tn=padded, name="mm_tail")

    def out_proj(gr, o_b, w, layer, name="out_proj"):
        tn, tk = _mm_tiles(gr["tm"], o_b.shape[1], d, 0, has_res=True)
        return matmul(o_b, w, layer, tm=gr["tm"], tn=tn, tk=tk, res=gr["xf"], alpha=alpha, name=name)

    outs = {k: [] for k in ("gdn_s_p", "gdn_c_p", "gdn_s_s", "gdn_c_s", "fox_k_p", "fox_v_p", "fox_lf_p",
                            "fox_k_s", "fox_v_s", "fox_lf_s", "dsa_k_p", "dsa_v_p", "dsa_ki_p",
                            "dsa_k_s", "dsa_v_s", "dsa_ki_s", "ffn_c_p", "ffn_c_s")}

    for i in range(depth):
        kind, j = i % 3, i // 3
        for tag, gr in groups.items():
            nb, nn = gr["nb"], gr["n"]
            if kind == 0:
                qkv = proj(gr, gdn_w_in, j, 0, conv_dim).reshape(nb, nn, conv_dim)
                z = proj(gr, gdn_w_in, j, conv_dim, val_dim).reshape(nb, nn, val_dim)
                ba = tail_proj(gr, gdn_w_in, j, conv_dim + val_dim, 128).reshape(nb, nn, 128)
                bt, at = ba[..., :GDN_V_HEADS], ba[..., GDN_V_HEADS:2 * GDN_V_HEADS]
                if tag == "p":
                    conv_buf = jnp.zeros((nb, gdn_conv_w.shape[1] - 1, conv_dim), F32)
                    s0 = jnp.zeros((nb, GDN_V_HEADS, GDN_HEAD, GDN_HEAD), F32)
                    o, s_new = gdn_mixer(qkv, z, ba, conv_buf, s0, gdn_conv_w, gdn_a_log, gdn_dt_bias, gdn_norm_w, j,
                                         blk=512, hg=4, cpi=4)
                    c_new = qkv[:, nn - (gdn_conv_w.shape[1] - 1):]
                else:
                    conv_buf, s0 = state_gdn_conv[j], state_gdn[j]
                    o, c_new, s_new = _gdn_core(qkv, z, bt, at, conv_buf, s0, gdn_conv_w[j], gdn_a_log[j],
                                                gdn_dt_bias[j], gdn_norm_w[j])
                    o = o.astype(BF16)
                outs["gdn_s_" + tag].append(s_new)
                outs["gdn_c_" + tag].append(c_new)
                r = out_proj(gr, o.reshape(nb * nn, val_dim), gdn_w_out, j)
            elif kind == 1:
                k = proj(gr, fox_w_in, j, d, d)
                v = proj(gr, fox_w_in, j, 2 * d, d)
                f = tail_proj(gr, fox_w_in, j, 3 * d, 128)[:, :ATT_HEADS].reshape(nb, nn, ATT_HEADS)
                lf = jax.nn.log_sigmoid(f + fox_b_f[j])
                k4, v4 = k.reshape(nb, nn, ATT_HEADS, HEAD_DIM), v.reshape(nb, nn, ATT_HEADS, HEAD_DIM)
                outs["fox_k_" + tag].append(k4)
                outs["fox_v_" + tag].append(v4)
                outs["fox_lf_" + tag].append(lf)
                if tag == "p":
                    q = proj(gr, fox_w_in, j, 0, d, out_dtype=BF16)
                    o = flash_attention(q.reshape(nb, nn, d), k.reshape(nb, nn, d), v.reshape(nb, nn, d),
                                        r=_suffix_exclusive(lf), tq=1024, tk=1024, name="fox_flash")
                    o = o.reshape(nb * nn, d)
                else:
                    q = proj(gr, fox_w_in, j, 0, d).reshape(nb, nn, ATT_HEADS, HEAD_DIM)
                    lf_all = jnp.concatenate([_paged_rows(cache_fox_logf[j], page_table), lf], axis=1)
                    rr = _suffix_exclusive(lf_all)
                    o = paged_attention(q, cache_fox_k, cache_fox_v, j, page_table, k4, v4, pg=8,
                                        r_q=rr[:, n_past:], r_k=rr, name="fox_decode")
                    o = o.reshape(nb * nn, d).astype(BF16)
                r = out_proj(gr, o, fox_w_out, j)
            else:
                k = proj(gr, dsa_w_in, j, d, d)
                v = proj(gr, dsa_w_in, j, 2 * d, d)
                tail = tail_proj(gr, dsa_w_in, j, 3 * d + IDX_HEADS * IDX_DIM, 256)
                ki, _ = layer_norm(tail[:, :IDX_DIM], dsa_idx_ln_g, dsa_idx_ln_b, j, tm=min(gr["tm"], 512),
                                   name="dsa_ki_ln")
                wh = tail[:, IDX_DIM:IDX_DIM + IDX_HEADS] * (IDX_HEADS ** -0.5)
                k4, v4 = k.reshape(nb, nn, ATT_HEADS, HEAD_DIM), v.reshape(nb, nn, ATT_HEADS, HEAD_DIM)
                ki3 = ki.reshape(nb, nn, IDX_DIM)
                outs["dsa_k_" + tag].append(k4)
                outs["dsa_v_" + tag].append(v4)
                outs["dsa_ki_" + tag].append(ki3)
                if tag == "p":
                    q = proj(gr, dsa_w_in, j, 0, d, out_dtype=BF16)
                    qidx = proj(gr, dsa_w_in, j, 3 * d, IDX_HEADS * IDX_DIM, out_dtype=BF16)
                    mask = dsa_select(qidx.reshape(nb, nn, IDX_HEADS * IDX_DIM), ki3, wh.reshape(nb, nn, IDX_HEADS),
                                      tq=256, tk=512, topk=min(TOPK_MAX, nn // 4))
                    o = flash_attention(q.reshape(nb, nn, d), k.reshape(nb, nn, d), v.reshape(nb, nn, d),
                                        mask=mask, tq=1024, tk=1024, name="dsa_flash")
                    o = o.reshape(nb * nn, d)
                else:
                    q = proj(gr, dsa_w_in, j, 0, d).reshape(nb, nn, ATT_HEADS, HEAD_DIM)
                    qidx = proj(gr, dsa_w_in, j, 3 * d, IDX_HEADS * IDX_DIM).reshape(nb, nn, IDX_HEADS, IDX_DIM)
                    sel = paged_select(qidx, wh.reshape(nb, nn, IDX_HEADS), cache_dsa_kidx, j, page_table, ki3,
                                       topk=min(TOPK_MAX, (n_past + nn) // 4), pg=16, name="dsa_decode_select")
                    o = paged_attention(q, cache_dsa_k, cache_dsa_v, j, page_table, k4, v4, pg=8, mask=sel,
                                        name="dsa_decode")
                    o = o.reshape(nb * nn, d).astype(BF16)
                r = out_proj(gr, o, dsa_w_out, j)

            ln_tm = min(gr["tm"], 512)
            gr["xf"], gr["xb"] = layer_norm(r, ln_mix_g, ln_mix_b, i, tm=ln_tm, name="ln_mix")

            if tag == "p":
                act, c_new = ffn_up(gr["xb"], jnp.zeros((nb, 2, f2), F32), ffn_w_up, ffn_conv_w, ffn_conv_b, i,
                                    seq_len=nn, tm=gr["tm"], tn=512)
            else:
                hcur = proj(gr, ffn_w_up, i, 0, f2).reshape(nb, nn, f2)
                ext = jnp.concatenate([state_ffn_conv[i], hcur], axis=1)
                c_new = ext[:, -(ffn_conv_w.shape[1] - 1):]
                hc = _causal_dwconv(ext, ffn_conv_w[i]) + ffn_conv_b[i]
                gate, val = jnp.split(hc, [f2 // 2], axis=-1)
                act = (jax.nn.silu(gate) * val).reshape(nb * nn, f2 // 2).astype(BF16)
            outs["ffn_c_" + tag].append(c_new)
            r = out_proj(gr, act, ffn_w_down, i, name="ffn_down")
            x2f, x2b = layer_norm(r, ln_ffn_g, ln_ffn_b, i, tm=ln_tm, name="ln_ffn")
            p_in = (p_prompt if tag == "p" else p_sample)[i].reshape(nb * nn, -1).astype(BF16)
            gr["xf"], gr["xb"] = per_layer_embed(x2f, x2b, p_in, ple_w_proj, ple_w_gate, i, tm=gr["tm"],
                                                 tn=_tile_n(d, 0, 512))

    st = lambda key: jnp.stack(outs[key])
    return (groups["p"]["xf"].reshape(b, n, d), groups["s"]["xf"].reshape(db, ns, d),
            st("gdn_s_p"), st("gdn_c_p"), st("fox_k_p"), st("fox_v_p"), st("fox_lf_p"),
            st("dsa_k_p"), st("dsa_v_p"), st("dsa_ki_p"), st("ffn_c_p"),
            st("gdn_s_s"), st("gdn_c_s"), st("fox_k_s"), st("fox_v_s"), st("fox_lf_s"),
            st("dsa_k_s"), st("dsa_v_s"), st("dsa_ki_s"), st("ffn_c_s"))
```

```python
import functools
import math

import jax
import jax.numpy as jnp
from jax import lax
from jax.experimental import pallas as pl
from jax.experimental.pallas import tpu as pltpu

F32 = jnp.float32
BF16 = jnp.bfloat16

PAGE = 128
GDN_QK_HEADS = 16
GDN_V_HEADS = 32
GDN_HEAD = 128
GDN_CHUNK = 64
ATT_HEADS = 16
HEAD_DIM = 128
IDX_HEADS = 16
IDX_DIM = 128
TOPK_MAX = 256
LN_EPS = 1e-5
RMS_EPS = 1e-6
L2_EPS = 1e-6

V7X_VMEM_LIMIT_BYTES = 56 * 1024 * 1024
NEG_BIG = -1e30
INT_MIN = -(2 ** 31)


def _params(*sem):
    return pltpu.CompilerParams(dimension_semantics=sem, vmem_limit_bytes=V7X_VMEM_LIMIT_BYTES)


def _dot(a, b, **kw):
    return jnp.dot(a, b, preferred_element_type=F32, **kw)


def _dot_nt(a, b, **kw):
    return lax.dot_general(a, b, (((1,), (1,)), ((), ())), preferred_element_type=F32, **kw)


def _mm_body(*refs, nk, has_res, alpha):
    if has_res:
        x_ref, w_ref, r_ref, o_ref, *scratch = refs
    else:
        x_ref, w_ref, o_ref, *scratch = refs
        r_ref = None

    def finish(acc):
        if has_res:
            acc = alpha * r_ref[...] + acc
        o_ref[...] = acc.astype(o_ref.dtype)

    part = jnp.dot(x_ref[...], w_ref[...].astype(BF16), preferred_element_type=F32)
    if nk == 1:
        finish(part)
        return
    acc_ref, = scratch
    k = pl.program_id(2)

    @pl.when(k == 0)
    def _():
        acc_ref[...] = part

    @pl.when(k > 0)
    def _():
        acc_ref[...] += part

    @pl.when(k == nk - 1)
    def _():
        finish(acc_ref[...])


def matmul(x, w, layer, *, col0=0, n=None, tm, tn, tk=None, out_dtype=F32, res=None, alpha=None, name="mm"):
    m, kdim = x.shape
    assert x.dtype == BF16 and w.shape[1] == kdim
    n = w.shape[2] - col0 if n is None else n
    tk = kdim if tk is None else tk
    assert m % tm == 0 and n % tn == 0 and kdim % tk == 0 and col0 % tn == 0
    nk = kdim // tk
    cb0 = col0 // tn
    rows_inner = nk == 1 and tm * tk * x.dtype.itemsize < tk * tn * w.dtype.itemsize
    if rows_inner:
        grid = (n // tn, m // tm, nk)
        ij = lambda a, b: (b, a)
    else:
        grid = (m // tm, n // tn, nk)
        ij = lambda a, b: (a, b)
    in_specs = [
        pl.BlockSpec((tm, tk), lambda a, b, k: (ij(a, b)[0], k)),
        pl.BlockSpec((None, tk, tn), lambda a, b, k: (layer, k, cb0 + ij(a, b)[1])),
    ]
    args = [x, w]
    if res is not None:
        assert res.shape == (m, n)
        in_specs.append(pl.BlockSpec((tm, tn), lambda a, b, k: ij(a, b)))
        args.append(res)
    return pl.pallas_call(
        functools.partial(_mm_body, nk=nk, has_res=res is not None, alpha=alpha),
        out_shape=jax.ShapeDtypeStruct((m, n), out_dtype),
        grid=grid,
        in_specs=in_specs,
        out_specs=pl.BlockSpec((tm, tn), lambda a, b, k: ij(a, b)),
        scratch_shapes=[pltpu.VMEM((tm, tn), F32)] if nk > 1 else [],
        compiler_params=_params("parallel", "parallel", "arbitrary"),
        name=name,
    )(*args)


def _ln_body(x_ref, g_ref, b_ref, of_ref, ob_ref, *, eps):
    x = x_ref[...]
    mu = jnp.mean(x, -1, keepdims=True)
    xc = x - mu
    var = jnp.mean(xc * xc, -1, keepdims=True)
    y = xc * lax.rsqrt(var + eps) * g_ref[...] + b_ref[...]
    of_ref[...] = y
    ob_ref[...] = y.astype(BF16)


def layer_norm(x, g, b, layer, *, tm, eps=LN_EPS, name="ln"):
    m, n = x.shape
    assert m % tm == 0
    g3 = g.reshape(g.shape[0], 1, n)
    b3 = b.reshape(b.shape[0], 1, n)
    vec = pl.BlockSpec((None, 1, n), lambda i: (layer, 0, 0))
    row = pl.BlockSpec((tm, n), lambda i: (i, 0))
    return pl.pallas_call(
        functools.partial(_ln_body, eps=eps),
        out_shape=(jax.ShapeDtypeStruct((m, n), F32), jax.ShapeDtypeStruct((m, n), BF16)),
        grid=(m // tm,),
        in_specs=[row, vec, vec],
        out_specs=(row, row),
        compiler_params=_params("parallel"),
        name=name,
    )(x, g3, b3)


def _ple_body(xb_ref, pb_ref, wg_ref, wp_ref, xr_ref, of_ref, ob_ref):
    gate = jnp.dot(xb_ref[...], wg_ref[...].astype(BF16), preferred_element_type=F32)
    proj = jnp.dot(pb_ref[...], wp_ref[...].astype(BF16), preferred_element_type=F32)
    y = xr_ref[...] + proj * jax.nn.sigmoid(gate)
    of_ref[...] = y
    ob_ref[...] = y.astype(BF16)


def per_layer_embed(x_f, x_b, p_b, w_proj, w_gate, layer, *, tm, tn, name="ple"):
    m, d = x_f.shape
    pd = p_b.shape[1]
    assert m % tm == 0 and d % tn == 0
    out = pl.BlockSpec((tm, tn), lambda i, j: (i, j))
    return pl.pallas_call(
        _ple_body,
        out_shape=(jax.ShapeDtypeStruct((m, d), F32), jax.ShapeDtypeStruct((m, d), BF16)),
        grid=(m // tm, d // tn),
        in_specs=[
            pl.BlockSpec((tm, d), lambda i, j: (i, 0)),
            pl.BlockSpec((tm, pd), lambda i, j: (i, 0)),
            pl.BlockSpec((None, d, tn), lambda i, j: (layer, 0, j)),
            pl.BlockSpec((None, pd, tn), lambda i, j: (layer, 0, j)),
            out,
        ],
        out_specs=(out, out),
        compiler_params=_params("parallel", "parallel"),
        name=name,
    )(x_b, p_b, w_gate, w_proj, x_f)


HALO_ROWS = 16


def _ffn_up_body(x_ref, xh_ref, wg_ref, wv_ref, cwg_ref, cwv_ref, cbg_ref, cbv_ref, hg_ref, hv_ref,
                 act_ref, tg_ref, tv_ref, hs_ref, *, tiles_per_seq, tm):
    first = (pl.program_id(0) % tiles_per_seq) == 0
    x = x_ref[...]
    xh = xh_ref[...]
    conv = []
    for w_ref, cw_ref, cb_ref, hist_ref, tail_ref in ((wg_ref, cwg_ref, cbg_ref, hg_ref, tg_ref),
                                                     (wv_ref, cwv_ref, cbv_ref, hv_ref, tv_ref)):
        w = w_ref[...].astype(BF16)
        h = jnp.dot(x, w, preferred_element_type=F32)
        halo = jnp.dot(xh, w, preferred_element_type=F32)
        hs_ref[0:8, :] = jnp.where(first, hist_ref[...], halo[HALO_ROWS - 8:, :])
        hs_ref[8:8 + tm, :] = h
        cw = cw_ref[...]
        c = hs_ref[6:6 + tm, :] * cw[0:1, :] + hs_ref[7:7 + tm, :] * cw[1:2, :] + h * cw[2:3, :] + cb_ref[...]
        conv.append(c)
        tail_ref[...] = hs_ref[8 + tm - 2:8 + tm, :]
    act_ref[...] = (jax.nn.silu(conv[0]) * conv[1]).astype(act_ref.dtype)


def ffn_up(x_b, hist, w_up, conv_w, conv_b, layer, *, seq_len, tm, tn, name="ffn_up"):
    m, d = x_b.shape
    f2 = w_up.shape[2]
    f = f2 // 2
    nb = m // seq_len
    assert seq_len % tm == 0 and f % tn == 0 and tm % HALO_ROWS == 0
    tiles_per_seq = seq_len // tm
    nj = f // tn
    hist8 = jnp.concatenate([jnp.zeros((nb, 6, f2), F32), hist.astype(F32)], axis=1)
    cw = conv_w
    cb = conv_b.reshape(conv_b.shape[0], 1, f2)
    halo_blocks = tm // HALO_ROWS

    def wspec(off):
        return pl.BlockSpec((None, d, tn), lambda i, j: (layer, 0, off + j))

    def cwspec(off):
        return pl.BlockSpec((None, cw.shape[1], tn), lambda i, j: (layer, 0, off + j))

    def cbspec(off):
        return pl.BlockSpec((None, 1, tn), lambda i, j: (layer, 0, off + j))

    def hspec(off):
        return pl.BlockSpec((None, 8, tn), lambda i, j: (i // tiles_per_seq, 0, off + j))

    tail = pl.BlockSpec((None, 2, tn), lambda i, j: (i, 0, j))
    act, tail_g, tail_v = pl.pallas_call(
        functools.partial(_ffn_up_body, tiles_per_seq=tiles_per_seq, tm=tm),
        out_shape=(jax.ShapeDtypeStruct((m, f), BF16),
                   jax.ShapeDtypeStruct((m // tm, 2, f), F32), jax.ShapeDtypeStruct((m // tm, 2, f), F32)),
        grid=(m // tm, nj),
        in_specs=[
            pl.BlockSpec((tm, d), lambda i, j: (i, 0)),
            pl.BlockSpec((HALO_ROWS, d), lambda i, j: (jnp.maximum(i * halo_blocks - 1, 0), 0)),
            wspec(0), wspec(nj), cwspec(0), cwspec(nj), cbspec(0), cbspec(nj), hspec(0), hspec(nj),
        ],
        out_specs=(pl.BlockSpec((tm, tn), lambda i, j: (i, j)), tail, tail),
        scratch_shapes=[pltpu.VMEM((tm + 8, tn), F32)],
        compiler_params=_params("arbitrary", "arbitrary"),
        name=name,
    )(x_b, x_b, w_up, w_up, cw, cw, cb, cb, hist8, hist8)
    last = slice(tiles_per_seq - 1, None, tiles_per_seq)
    return act, jnp.concatenate([tail_g[last], tail_v[last]], axis=-1)


def _flash_body(*refs, tq, tk, scale, has_bias, has_mask):
    refs = list(refs)
    q_ref, k_ref, v_ref = refs[:3]
    rest = refs[3:]
    if has_bias:
        rq_ref, rk_ref = rest[:2]
        rest = rest[2:]
    if has_mask:
        mask_ref = rest[0]
        rest = rest[1:]
    if has_bias:
        o_ref, m_ref, l_ref, acc_ref, rqb_ref = rest
    else:
        o_ref, m_ref, l_ref, acc_ref = rest
    qi = pl.program_id(2)
    lanes = m_ref.shape[1]
    nblk = tk // lanes
    m_ref[...] = jnp.full(m_ref.shape, NEG_BIG, F32)
    l_ref[...] = jnp.zeros(l_ref.shape, F32)
    acc_ref[...] = jnp.zeros(acc_ref.shape, F32)
    if has_bias:
        rqb_ref[...] = jnp.broadcast_to(rq_ref[...], rqb_ref.shape)
    tri = lax.broadcasted_iota(jnp.int32, (tq, lanes), 1) - lax.broadcasted_iota(jnp.int32, (tq, lanes), 0)

    def chunk(c, diagonal):
        off = pl.multiple_of(c * tk, tk)
        kc = k_ref[pl.ds(off, tk), :].astype(BF16)
        vc = v_ref[pl.ds(off, tk), :].astype(BF16)
        s = lax.dot_general(q_ref[...], kc, (((1,), (1,)), ((), ())), preferred_element_type=F32) * scale
        blocks = [s[:, j * lanes:(j + 1) * lanes] for j in range(nblk)]
        if has_bias:
            rk = rk_ref[c]
            rqb = rqb_ref[...]
            blocks = [blocks[j] + rk[:, j * lanes:(j + 1) * lanes] - rqb for j in range(nblk)]
        if has_mask:
            pieces = tk // mask_ref.shape[-1]
            keep = jnp.concatenate([mask_ref[c * pieces + i] for i in range(pieces)], axis=1).astype(jnp.int32)
            blocks = [jnp.where(keep[:, j * lanes:(j + 1) * lanes] != 0, blocks[j], NEG_BIG) for j in range(nblk)]
        elif diagonal:
            blocks = [jnp.where(tri + j * lanes <= 0, blocks[j], NEG_BIG) for j in range(nblk)]
        bmax = blocks[0]
        for blk in blocks[1:]:
            bmax = jnp.maximum(bmax, blk)
        m_old = m_ref[...]
        m_new = jnp.maximum(m_old, jnp.max(bmax, -1, keepdims=True))
        a = jnp.exp(m_old - m_new)
        ps = [jnp.exp(blk - m_new) for blk in blocks]
        psum = ps[0]
        for pj in ps[1:]:
            psum = psum + pj
        l_ref[...] = a * l_ref[...] + psum
        p = jnp.concatenate(ps, axis=1).astype(BF16)
        acc_ref[...] = a * acc_ref[...] + jnp.dot(p, vc, preferred_element_type=F32)
        m_ref[...] = m_new

    def full_chunk(c, carry):
        chunk(c, False)
        return carry

    if has_mask:
        lax.fori_loop(0, qi + 1, full_chunk, 0)
    else:
        lax.fori_loop(0, qi, full_chunk, 0)
        chunk(qi, True)
    o_ref[...] = (acc_ref[...] / jnp.sum(l_ref[...], -1, keepdims=True)).astype(o_ref.dtype)


def flash_attention(q, k, v, *, r=None, mask=None, tq, tk, name="flash"):
    b, n, hd = q.shape
    h = hd // HEAD_DIM
    assert n % tq == 0 and tq == tk and tk % HEAD_DIM == 0
    nkc = n // tk
    in_specs = [
        pl.BlockSpec((None, tq, HEAD_DIM), lambda bi, hi, qi: (bi, qi, hi)),
        pl.BlockSpec((None, n, HEAD_DIM), lambda bi, hi, qi: (bi, 0, hi)),
        pl.BlockSpec((None, n, HEAD_DIM), lambda bi, hi, qi: (bi, 0, hi)),
    ]
    args = [q, k, v]
    if r is not None:
        rt = r.transpose(0, 2, 1)
        in_specs.append(pl.BlockSpec((None, None, tq, 1), lambda bi, hi, qi: (bi, hi, qi, 0)))
        in_specs.append(pl.BlockSpec((None, None, nkc, 1, tk), lambda bi, hi, qi: (bi, hi, 0, 0, 0)))
        args += [rt.reshape(b, h, n, 1), rt.reshape(b, h, nkc, 1, tk)]
    if mask is not None:
        mtk = mask.shape[3]
        assert tk % mtk == 0 and mask.shape[1] * mtk == n
        in_specs.append(pl.BlockSpec((None, n // mtk, tq, mtk), lambda bi, hi, qi: (bi, 0, qi, 0)))
        args.append(mask)
    return pl.pallas_call(
        functools.partial(_flash_body, tq=tq, tk=tk, scale=HEAD_DIM ** -0.5,
                          has_bias=r is not None, has_mask=mask is not None),
        out_shape=jax.ShapeDtypeStruct((b, n, hd), BF16),
        grid=(b, h, n // tq),
        in_specs=in_specs,
        out_specs=pl.BlockSpec((None, tq, HEAD_DIM), lambda bi, hi, qi: (bi, qi, hi)),
        scratch_shapes=[pltpu.VMEM((tq, HEAD_DIM), F32)] * (4 if r is not None else 3),
        compiler_params=_params("parallel", "parallel", "arbitrary"),
        name=name,
    )(*args)


def _dsa_select_body(q_ref, kidx_ref, wh_ref, mask_ref, keys_ref, *, tq, tk, topk, n_keys):
    qi = pl.program_id(1)
    n_valid = ((qi + 1) * tq + tk - 1) // tk
    whs = wh_ref[...] * (IDX_DIM ** -0.5)
    q_pos = qi * tq + lax.broadcasted_iota(jnp.int32, (tq, tk), 0)
    k_iota = lax.broadcasted_iota(jnp.int32, (tq, tk), 1)
    n_heads = wh_ref.shape[-1]
    lanes = 128

    def score_chunk(c, carry):
        off = pl.multiple_of(c * tk, tk)
        kc = kidx_ref[pl.ds(off, tk), :].astype(BF16)
        acc = jnp.zeros((tq, tk), F32)
        for h in range(n_heads):
            d = lax.dot_general(q_ref[:, h * IDX_DIM:(h + 1) * IDX_DIM], kc, (((1,), (1,)), ((), ())),
                                preferred_element_type=F32)
            acc = acc + jnp.maximum(d, 0.0) * whs[:, h:h + 1]
        bits = pltpu.bitcast(acc, jnp.int32)
        key = bits ^ ((bits >> 31) & 0x7FFFFFFF)
        keys_ref[c] = jnp.where((off + k_iota) <= q_pos, key, INT_MIN)
        return carry

    lax.fori_loop(0, n_valid, score_chunk, 0)

    def count(pred):
        def body(c, part):
            ind = jnp.where(pred(keys_ref[c], c * tk), 1.0, 0.0)
            for s in range(tk // lanes):
                part = part + ind[:, s * lanes:(s + 1) * lanes]
            return part
        part = lax.fori_loop(0, n_valid, body, jnp.zeros((tq, lanes), F32))
        return jnp.sum(part, -1, keepdims=True)

    kf = float(topk)

    def value_bit(it, t):
        cand = t + lax.shift_left(jnp.int32(1), 31 - it)
        cnt = count(lambda kk, off: kk >= cand)
        return jnp.where(cnt >= kf, cand, t)

    thr = lax.fori_loop(0, 32, value_bit, jnp.full((tq, 1), INT_MIN, jnp.int32))
    cnt_ge = count(lambda kk, off: kk >= thr)
    cnt_gt = count(lambda kk, off: kk > thr)
    need = kf - cnt_gt
    idx_bits = int(n_keys).bit_length()

    def tie_search(_):
        def index_bit(it, p):
            cand = p + lax.shift_left(jnp.int32(1), idx_bits - 1 - it)
            cnt = count(lambda kk, off: (kk == thr) & ((off + k_iota) < cand))
            return jnp.where(cnt < need, cand, p)
        return lax.fori_loop(0, idx_bits, index_bit, jnp.zeros((tq, 1), jnp.int32))

    excess = jnp.max(cnt_ge - kf) > 0.0
    last_tie = lax.cond(excess, tie_search, lambda _: jnp.full((tq, 1), n_keys, jnp.int32), 0)

    mask_ref[...] = jnp.zeros(mask_ref.shape, mask_ref.dtype)

    def emit(c, carry):
        off = c * tk
        kk = keys_ref[c]
        k_pos = off + k_iota
        sel = (kk > thr) | ((kk == thr) & (k_pos <= last_tie))
        sel = sel & (k_pos <= q_pos)
        mask_ref[c] = jnp.where(sel, 1, 0).astype(mask_ref.dtype)
        return carry

    lax.fori_loop(0, n_valid, emit, 0)


def dsa_select(qidx, kidx, wh, *, tq, tk, topk, name="dsa_select"):
    b, n, _ = qidx.shape
    assert n % tq == 0 and n % tk == 0
    nkc = n // tk
    return pl.pallas_call(
        functools.partial(_dsa_select_body, tq=tq, tk=tk, topk=topk, n_keys=n),
        out_shape=jax.ShapeDtypeStruct((b, nkc, n, tk), jnp.int8),
        grid=(b, n // tq),
        in_specs=[
            pl.BlockSpec((None, tq, qidx.shape[2]), lambda bi, qi: (bi, qi, 0)),
            pl.BlockSpec((None, n, kidx.shape[2]), lambda bi, qi: (bi, 0, 0)),
            pl.BlockSpec((None, tq, wh.shape[2]), lambda bi, qi: (bi, qi, 0)),
        ],
        out_specs=pl.BlockSpec((None, nkc, tq, tk), lambda bi, qi: (bi, 0, qi, 0)),
        scratch_shapes=[pltpu.VMEM((nkc, tq, tk), jnp.int32)],
        compiler_params=_params("parallel", "arbitrary"),
        name=name,
    )(qidx, kidx, wh)


def _paged_attn_body(*refs, n_steps, pg, nq, n_heads, scale, has_bias, has_mask):
    refs = list(refs)
    pt_ref, q_ref, hm_ref = refs[:3]
    k_refs = refs[3:3 + pg]
    v_refs = refs[3 + pg:3 + 2 * pg]
    kn_ref, vn_ref = refs[3 + 2 * pg:5 + 2 * pg]
    rest = refs[5 + 2 * pg:]
    if has_bias:
        rq_ref, rk_ref = rest[:2]
        rest = rest[2:]
    if has_mask:
        mask_ref, expand_ref = rest[:2]
        rest = rest[2:]
    if has_bias:
        o_ref, m_s, l_s, acc_s, rqb_s = rest
    else:
        o_ref, m_s, l_s, acc_s = rest
    p = pl.program_id(1)
    rows = nq * n_heads
    cols = k_refs[0].shape[0]
    lanes = m_s.shape[1]
    nblk = cols // lanes
    assert lanes % n_heads == 0

    @pl.when(p == 0)
    def _():
        m_s[...] = jnp.full(m_s.shape, NEG_BIG, F32)
        l_s[...] = jnp.zeros(l_s.shape, F32)
        acc_s[...] = jnp.zeros(acc_s.shape, F32)
        if has_bias:
            rqb_s[...] = jnp.broadcast_to(rq_ref[...], rqb_s.shape)

    def attend(k_srcs, v_srcs, slots, new_tokens):
        q = q_ref[...]
        hm = hm_ref[...]
        blocks = []
        for k_src, slot in zip(k_srcs, slots):
            s = _dot_nt(q, k_src[...].astype(BF16)) * scale
            if has_mask:
                mk = mask_ref[slot]
                mk = jnp.concatenate([jnp.broadcast_to(mk[i:i + 1, :], (n_heads, mk.shape[1])) for i in range(nq)], axis=0)
                sel = _dot(mk.astype(BF16), expand_ref[...])
            for j in range(nblk):
                blk = s[:, j * lanes:(j + 1) * lanes]
                if has_bias:
                    blk = blk + rk_ref[slot][:, j * lanes:(j + 1) * lanes] - rqb_s[...]
                if has_mask:
                    blk = jnp.where(sel[:, j * lanes:(j + 1) * lanes] > 0.5, blk, NEG_BIG)
                elif new_tokens:
                    q_idx = lax.broadcasted_iota(jnp.int32, (rows, lanes), 0) // n_heads
                    t_idx = (j * lanes + lax.broadcasted_iota(jnp.int32, (rows, lanes), 1)) // n_heads
                    blk = jnp.where((t_idx <= q_idx) & (t_idx < nq), blk, NEG_BIG)
                blocks.append(blk + hm)
        bmax = blocks[0]
        for blk in blocks[1:]:
            bmax = jnp.maximum(bmax, blk)
        m_old = m_s[...]
        m_new = jnp.maximum(m_old, jnp.max(bmax, -1, keepdims=True))
        a = jnp.exp(m_old - m_new)
        ps = [jnp.exp(blk - m_new) for blk in blocks]
        psum = ps[0]
        for pj in ps[1:]:
            psum = psum + pj
        l_s[...] = a * l_s[...] + psum
        acc = a * acc_s[...]
        for i, v_src in enumerate(v_srcs):
            pr = jnp.concatenate(ps[i * nblk:(i + 1) * nblk], axis=1).astype(BF16)
            acc = acc + _dot(pr, v_src[...].astype(BF16))
        acc_s[...] = acc
        m_s[...] = m_new

    @pl.when(p < n_steps - 1)
    def _():
        attend(k_refs, v_refs, list(range(pg)), False)

    @pl.when(p == n_steps - 1)
    def _():
        attend([kn_ref], [vn_ref], [0], True)
        o_ref[...] = acc_s[...] / jnp.sum(l_s[...], -1, keepdims=True)


def paged_attention(q, pool_k, pool_v, layer, page_table, k_new, v_new, *, pg, r_q=None, r_k=None, mask=None,
                    name="paged_attn"):
    b, nq, h, d = q.shape
    n_pool, page = pool_k.shape[1], pool_k.shape[2]
    n_pages = page_table.shape[1]
    assert n_pages % pg == 0 and nq <= 8
    n_steps = n_pages // pg + 1
    rows, cols = nq * h, page * h
    pk = pool_k.reshape(pool_k.shape[0], n_pool, cols, d)
    pv = pool_v.reshape(pool_v.shape[0], n_pool, cols, d)
    pad = lambda a: jnp.pad(a, ((0, 0), (0, page - nq), (0, 0), (0, 0))).reshape(b, cols, d)
    ri = lax.broadcasted_iota(jnp.int32, (rows, d), 0) % h
    ci = lax.broadcasted_iota(jnp.int32, (rows, d), 1) % h
    head_match = jnp.where(ri == ci, 0.0, NEG_BIG).astype(F32)
    last = n_pages - 1

    def pool_spec(i):
        return pl.BlockSpec((None, None, cols, d),
                            lambda bi, p, pt: (layer, pt[bi, jnp.minimum(p * pg + i, last)], 0, 0))

    const2 = lambda shape: pl.BlockSpec(shape, lambda bi, p, pt: (0, 0))
    per_b = lambda shape: pl.BlockSpec((None,) + shape, lambda bi, p, pt: (bi,) + (0,) * len(shape))
    in_specs = ([per_b((rows, d)), const2((rows, d))] + [pool_spec(i) for i in range(pg)] * 2
                + [per_b((cols, d)), per_b((cols, d))])
    args = [q.reshape(b, rows, d).astype(BF16), head_match] + [pk] * pg + [pv] * pg + [pad(k_new), pad(v_new)]
    if r_q is not None:
        n_tot = r_k.shape[1]
        rk = jnp.pad(r_k, ((0, 0), (0, n_pages * page + pg * page - n_tot), (0, 0)))
        rk = rk.reshape(b, n_steps, pg, 1, cols)
        in_specs += [per_b((rows, 1)), pl.BlockSpec((None, None, pg, 1, cols), lambda bi, p, pt: (bi, p, 0, 0, 0))]
        args += [r_q.reshape(b, rows, 1), rk]
    if mask is not None:
        expand = (lax.broadcasted_iota(jnp.int32, (page, cols), 1) // h
                  == lax.broadcasted_iota(jnp.int32, (page, cols), 0)).astype(BF16)
        in_specs += [pl.BlockSpec((None, None, pg, 8, page), lambda bi, p, pt: (bi, p, 0, 0, 0)), const2((page, cols))]
        assert mask.shape[1] >= n_steps * pg
        args += [mask[:, :n_steps * pg].reshape(b, n_steps, pg, 8, page), expand]
    out = pl.pallas_call(
        functools.partial(_paged_attn_body, n_steps=n_steps, pg=pg, nq=nq, n_heads=h, scale=d ** -0.5,
                          has_bias=r_q is not None, has_mask=mask is not None),
        out_shape=jax.ShapeDtypeStruct((b, rows, d), F32),
        grid_spec=pltpu.PrefetchScalarGridSpec(
            num_scalar_prefetch=1,
            grid=(b, n_steps),
            in_specs=in_specs,
            out_specs=per_b((rows, d)),
            scratch_shapes=[pltpu.VMEM((rows, d), F32)] * (4 if r_q is not None else 3),
        ),
        compiler_params=_params("parallel", "arbitrary"),
        name=name,
    )(page_table, *args)
    return out.reshape(b, nq, h * d)


def _paged_select_body(*refs, n_steps, pg, nq, n_heads, topk, n_keys):
    pt_ref, q_ref, wh_ref = refs[:3]
    kidx_refs = refs[3:3 + pg]
    kin_ref, mask_ref, keys_s = refs[3 + pg:]
    p = pl.program_id(1)
    page = kin_ref.shape[0]
    lane = lax.broadcasted_iota(jnp.int32, (8, page), 1)
    q_row = lax.broadcasted_iota(jnp.int32, (8, page), 0)
    n_slots = n_steps * pg

    def score_page(src):
        d = _dot_nt(q_ref[...], src[...].astype(BF16))
        rel = jnp.maximum(d, 0.0) * (wh_ref[...] * (IDX_DIM ** -0.5))
        sc = jnp.sum(rel.reshape(nq, n_heads, page), axis=1)
        sc = jnp.concatenate([sc, jnp.zeros((8 - nq, page), F32)], axis=0)
        bits = pltpu.bitcast(sc, jnp.int32)
        return bits ^ ((bits >> 31) & 0x7FFFFFFF)

    @pl.when(p < n_steps - 1)
    def _():
        for i in range(pg):
            keys_s[p * pg + i] = score_page(kidx_refs[i])

    @pl.when(p == n_steps - 1)
    def _():
        first_new = (n_steps - 1) * pg
        keys_s[first_new] = jnp.where((lane <= q_row) & (lane < nq), score_page(kin_ref), INT_MIN)
        for i in range(1, pg):
            keys_s[first_new + i] = jnp.full((8, page), INT_MIN, jnp.int32)

        keys = keys_s[...]
        pos = (lax.broadcasted_iota(jnp.int32, keys.shape, 0) * page
               + lax.broadcasted_iota(jnp.int32, keys.shape, 2))

        def count(pred):
            return jnp.sum(jnp.sum(jnp.where(pred, 1.0, 0.0), axis=0), -1, keepdims=True)

        kf = float(topk)

        def value_bit(it, t):
            cand = t + lax.shift_left(jnp.int32(1), 31 - it)
            return jnp.where(count(keys >= cand) >= kf, cand, t)

        thr = lax.fori_loop(0, 32, value_bit, jnp.full((8, 1), INT_MIN, jnp.int32))
        need = kf - count(keys > thr)
        idx_bits = int(n_keys).bit_length()

        def index_bit(it, pos_max):
            cand = pos_max + lax.shift_left(jnp.int32(1), idx_bits - 1 - it)
            return jnp.where(count((keys == thr) & (pos < cand)) < need, cand, pos_max)

        last_tie = lax.fori_loop(0, idx_bits, index_bit, jnp.zeros((8, 1), jnp.int32))
        sel = (keys > thr) | ((keys == thr) & (pos <= last_tie))
        mask_ref[...] = jnp.where(sel & (keys != INT_MIN), 1.0, 0.0)


def paged_select(qidx, wh, pool_kidx, layer, page_table, kidx_new, *, topk, pg, name="paged_select"):
    b, nq, hi, di = qidx.shape
    page = pool_kidx.shape[2]
    n_pages = page_table.shape[1]
    assert n_pages % pg == 0 and nq <= 8
    n_steps = n_pages // pg + 1
    rows = nq * hi
    last = n_pages - 1
    kin = jnp.pad(kidx_new, ((0, 0), (0, page - nq), (0, 0)))

    def pool_spec(i):
        return pl.BlockSpec((None, None, page, di),
                            lambda bi, p, pt: (layer, pt[bi, jnp.minimum(p * pg + i, last)], 0, 0))

    per_b = lambda shape: pl.BlockSpec((None,) + shape, lambda bi, p, pt: (bi,) + (0,) * len(shape))
    mask = pl.pallas_call(
        functools.partial(_paged_select_body, n_steps=n_steps, pg=pg, nq=nq, n_heads=hi, topk=topk,
                          n_keys=n_pages * page + nq),
        out_shape=jax.ShapeDtypeStruct((b, n_steps * pg, 8, page), F32),
        grid_spec=pltpu.PrefetchScalarGridSpec(
            num_scalar_prefetch=1,
            grid=(b, n_steps),
            in_specs=[per_b((rows, di)), per_b((rows, 1))] + [pool_spec(i) for i in range(pg)] + [per_b((page, di))],
            out_specs=per_b((n_steps * pg, 8, page)),
            scratch_shapes=[pltpu.VMEM((n_steps * pg, 8, page), jnp.int32)],
        ),
        compiler_params=_params("parallel", "arbitrary"),
        name=name,
    )(page_table, qidx.reshape(b, rows, di).astype(BF16), wh.reshape(b, rows, 1), *([pool_kidx] * pg), kin)
    return mask


GDN_HALO = 8


def _gdn_body(q_ref, k_ref, v_ref, z_ref, ba_ref, cwq_ref, cwk_ref, cwv_ref, gate_ref, nw_ref, h0_ref, s0_ref,
              o_ref, sout_ref,
              xq_s, xk_s, xv_s, s_s, u_s, wq_s, qkkt_s, gt_s, *, blk, chunk, n_vh, hg, cpi):
    jg = pl.program_id(1)
    sb = pl.program_id(2)
    n_sb = pl.num_programs(2)
    nch = blk // chunk
    hd = GDN_HEAD
    exact = dict(precision=lax.Precision.HIGHEST)

    @pl.when(sb == 0)
    def _():
        xq_s[0:GDN_HALO, :] = h0_ref[:, 0:hg * hd]
        xk_s[0:GDN_HALO, :] = h0_ref[:, hg * hd:2 * hg * hd]
        xv_s[0:GDN_HALO, :] = h0_ref[:, 2 * hg * hd:]
        s_s[...] = s0_ref[...]

    xq_s[GDN_HALO:, :] = q_ref[...]
    xk_s[GDN_HALO:, :] = k_ref[...]
    xv_s[GDN_HALO:, :] = v_ref[...]

    row = lax.broadcasted_iota(jnp.int32, (chunk, chunk), 0)
    col = lax.broadcasted_iota(jnp.int32, (chunk, chunk), 1)
    lower = row >= col
    strict = row > col
    ltri = jnp.where(lower, 1.0, 0.0).astype(F32)
    lane = lax.broadcasted_iota(jnp.int32, (chunk, 128), 1)
    sel_row = lax.broadcasted_iota(jnp.int32, (8, 128), 0)
    sel_lane = lax.broadcasted_iota(jnp.int32, (8, 128), 1)
    hv0 = 2 * hg * jg
    pick = jnp.where((sel_row < 2 * hg) & (sel_lane == n_vh + hv0 + sel_row), 1.0, 0.0).astype(F32)
    neg_a = -jnp.exp(gate_ref[0:1, :])
    dt_bias = gate_ref[1:2, :]

    def conv_silu(xs_ref, w_ref, r0):
        x = xs_ref[pl.ds(r0, chunk + GDN_HALO), :]
        w = w_ref[...]
        taps = w.shape[0]
        acc = None
        for t in range(taps):
            sh = taps - 1 - t
            xt = x if sh == 0 else pltpu.roll(x, sh, axis=0)
            term = xt[GDN_HALO:, :] * w[t:t + 1, :]
            acc = term if acc is None else acc + term
        return jax.nn.silu(acc)

    def l2n(x):
        return x * lax.rsqrt(jnp.sum(x * x, -1, keepdims=True) + L2_EPS)

    def prep(ci, carry):
        cs = [ci * cpi + u for u in range(cpi)]
        r0s = [pl.multiple_of(c * chunk, chunk) for c in cs]
        q_all = [conv_silu(xq_s, cwq_ref, r0) for r0 in r0s]
        k_all = [conv_silu(xk_s, cwk_ref, r0) for r0 in r0s]
        v_all = [conv_silu(xv_s, cwv_ref, r0) for r0 in r0s]
        ba = [ba_ref[pl.ds(r0, chunk), :] for r0 in r0s]
        beta_all = [jax.nn.sigmoid(x) for x in ba]
        g_all = [neg_a * jax.nn.softplus(x + dt_bias) for x in ba]
        gc_all = [_dot(ltri, g, **exact) for g in g_all]
        gc_rows = [_dot_nt(pick, g, **exact) for g in gc_all]
        pairs = [(u, h) for u in range(cpi) for h in range(hg)]
        units = [(u, e) for u in range(cpi) for e in range(2 * hg)]
        qs = {(u, h): l2n(q_all[u][:, h * hd:(h + 1) * hd]) * (hd ** -0.5) for u, h in pairs}
        ks = {(u, h): l2n(k_all[u][:, h * hd:(h + 1) * hd]) for u, h in pairs}
        k16 = {p: ks[p].astype(BF16) for p in pairs}
        gram = {p: _dot_nt(k16[p], k16[p]) for p in pairs}
        qk_raw = {p: _dot_nt(qs[p].astype(BF16), k16[p]) for p in pairs}
        beta = {(u, e): jnp.sum(jnp.where(lane == hv0 + e, beta_all[u], 0.0), -1, keepdims=True) for u, e in units}
        gcol = {(u, e): jnp.sum(jnp.where(lane == n_vh + hv0 + e, gc_all[u], 0.0), -1, keepdims=True) for u, e in units}
        g_last = {t: gcol[t][chunk - 1:chunk, :] for t in units}
        decay = {(u, e): jnp.where(lower, jnp.exp(jnp.where(lower, gcol[u, e] - gc_rows[u][e:e + 1, :], 0.0)), 0.0)
                 for u, e in units}
        nmat = {(u, e): jnp.where(strict, -(beta[u, e] * gram[u, e // 2] * decay[u, e]), 0.0) for u, e in units}
        cpow = {t: nmat[t].astype(BF16) for t in units}
        for _ in range(max(chunk.bit_length() - 2, 0)):
            cnew = {t: _dot(cpow[t], cpow[t]) for t in units}
            cpow = {t: cnew[t].astype(BF16) for t in units}
            corr = {t: _dot(nmat[t].astype(BF16), cpow[t]) for t in units}
            nmat = {t: nmat[t] + cnew[t] + corr[t] for t in units}
        eg = {t: jnp.exp(gcol[t]) for t in units}
        rhs = {(u, e): jnp.concatenate([v_all[u][:, e * hd:(e + 1) * hd] * beta[u, e],
                                        (ks[u, e // 2] * beta[u, e]) * eg[u, e]], axis=1) for u, e in units}
        sol = {t: rhs[t] + _dot(nmat[t].astype(BF16), rhs[t].astype(BF16)) for t in units}
        for u, e in units:
            c, t = cs[u], (u, e)
            u_s[e, pl.ds(r0s[u], chunk), :] = sol[t][:, :hd]
            wq_s[e, c, 0:chunk, :] = sol[t][:, hd:].astype(BF16)
            wq_s[e, c, chunk:2 * chunk, :] = (qs[u, e // 2] * eg[t]).astype(BF16)
            qkkt_s[e, c, 0:chunk, :] = jnp.where(lower, qk_raw[u, e // 2] * decay[t], 0.0).astype(BF16)
            k_tail = ks[u, e // 2] * jnp.exp(g_last[t] - gcol[t])
            qkkt_s[e, c, chunk:, :] = k_tail.T.astype(BF16)
            gt_s[e, c] = jnp.broadcast_to(jnp.exp(g_last[t]), (8, 128))
        return carry

    lax.fori_loop(0, nch // cpi, prep, 0)

    nw = nw_ref[...]

    def recur(c, carry):
        r0 = pl.multiple_of(c * chunk, chunk)
        heads = range(2 * hg)
        s = [s_s[e] for e in heads]
        ws = [_dot(wq_s[e, c], s[e].astype(BF16)) for e in heads]
        v_new = [u_s[e, pl.ds(r0, chunk), :] - ws[e][0:chunk] for e in heads]
        mix = [_dot(qkkt_s[e, c], v_new[e].astype(BF16)) for e in heads]
        for e in heads:
            s_s[e] = s[e] * gt_s[e, c][0:1, 0:1] + mix[e][chunk:]
            o = ws[e][chunk:] + mix[e][0:chunk]
            o = o * lax.rsqrt(jnp.mean(o * o, -1, keepdims=True) + RMS_EPS) * nw
            o = o * jax.nn.silu(z_ref[pl.ds(r0, chunk), e * hd:(e + 1) * hd])
            o_ref[pl.ds(r0, chunk), e * hd:(e + 1) * hd] = o.astype(o_ref.dtype)
        return carry

    lax.fori_loop(0, nch, recur, 0)

    xq_s[0:GDN_HALO, :] = xq_s[blk:blk + GDN_HALO, :]
    xk_s[0:GDN_HALO, :] = xk_s[blk:blk + GDN_HALO, :]
    xv_s[0:GDN_HALO, :] = xv_s[blk:blk + GDN_HALO, :]

    @pl.when(sb == n_sb - 1)
    def _():
        sout_ref[...] = s_s[...]


def gdn_mixer(qkv, z, ba, conv_hist, s0, conv_w, a_log, dt_bias, norm_w, layer, *, blk, hg, cpi, chunk=GDN_CHUNK,
              name="gdn"):
    b, n, conv_dim = qkv.shape
    hd = GDN_HEAD
    n_qk, n_vh = GDN_QK_HEADS, GDN_V_HEADS
    assert n_vh == 2 * n_qk and n % blk == 0 and blk % chunk == 0 and 2 * n_vh <= ba.shape[2]
    assert n_qk % hg == 0 and 2 * hg <= 8 and (blk // chunk) % cpi == 0
    ng = n_qk // hg
    taps = conv_w.shape[1]
    hist = jnp.concatenate([jnp.zeros((b, GDN_HALO - (taps - 1), conv_dim), F32), conv_hist.astype(F32)], axis=1)
    hq = hist[:, :, :n_qk * hd].reshape(b, GDN_HALO, ng, hg * hd)
    hk = hist[:, :, n_qk * hd:2 * n_qk * hd].reshape(b, GDN_HALO, ng, hg * hd)
    hv = hist[:, :, 2 * n_qk * hd:].reshape(b, GDN_HALO, ng, 2 * hg * hd)
    h0 = jnp.concatenate([hq, hk, hv], axis=-1).transpose(0, 2, 1, 3)
    gate = jnp.zeros((8, ba.shape[2]), F32)
    gate = gate.at[0, n_vh:2 * n_vh].set(a_log[layer]).at[1, n_vh:2 * n_vh].set(dt_bias[layer])
    nw = norm_w.reshape(norm_w.shape[0], 1, hd)
    nch = blk // chunk
    wq, wv = hg * hd, 2 * hg * hd
    o, s_out = pl.pallas_call(
        functools.partial(_gdn_body, blk=blk, chunk=chunk, n_vh=n_vh, hg=hg, cpi=cpi),
        out_shape=(jax.ShapeDtypeStruct((b, n, n_vh * hd), BF16), jax.ShapeDtypeStruct(s0.shape, F32)),
        grid=(b, ng, n // blk),
        in_specs=[
            pl.BlockSpec((None, blk, wq), lambda bi, j, sb: (bi, sb, j)),
            pl.BlockSpec((None, blk, wq), lambda bi, j, sb: (bi, sb, ng + j)),
            pl.BlockSpec((None, blk, wv), lambda bi, j, sb: (bi, sb, ng + j)),
            pl.BlockSpec((None, blk, wv), lambda bi, j, sb: (bi, sb, j)),
            pl.BlockSpec((None, blk, ba.shape[2]), lambda bi, j, sb: (bi, sb, 0)),
            pl.BlockSpec((None, taps, wq), lambda bi, j, sb: (layer, 0, j)),
            pl.BlockSpec((None, taps, wq), lambda bi, j, sb: (layer, 0, ng + j)),
            pl.BlockSpec((None, taps, wv), lambda bi, j, sb: (layer, 0, ng + j)),
            pl.BlockSpec((8, ba.shape[2]), lambda bi, j, sb: (0, 0)),
            pl.BlockSpec((None, 1, hd), lambda bi, j, sb: (layer, 0, 0)),
            pl.BlockSpec((None, None, GDN_HALO, 4 * wq), lambda bi, j, sb: (bi, j, 0, 0)),
            pl.BlockSpec((None, 2 * hg, hd, hd), lambda bi, j, sb: (bi, j, 0, 0)),
        ],
        out_specs=(
            pl.BlockSpec((None, blk, wv), lambda bi, j, sb: (bi, sb, j)),
            pl.BlockSpec((None, 2 * hg, hd, hd), lambda bi, j, sb: (bi, j, 0, 0)),
        ),
        scratch_shapes=[
            pltpu.VMEM((blk + GDN_HALO, wq), F32), pltpu.VMEM((blk + GDN_HALO, wq), F32),
            pltpu.VMEM((blk + GDN_HALO, wv), F32),
            pltpu.VMEM((2 * hg, hd, hd), F32),
            pltpu.VMEM((2 * hg, blk, hd), F32),
            pltpu.VMEM((2 * hg, nch, 2 * chunk, hd), BF16),
            pltpu.VMEM((2 * hg, nch, chunk + hd, chunk), BF16),
            pltpu.VMEM((2 * hg, nch, 8, 128), F32),
        ],
        compiler_params=_params("parallel", "parallel", "arbitrary"),
        name=name,
    )(qkv, qkv, qkv, z, ba, conv_w, conv_w, conv_w, gate, nw, h0, s0)
    return o, s_out


def _l2norm(x):
    return x * lax.rsqrt(jnp.sum(x * x, -1, keepdims=True) + L2_EPS)


def _causal_dwconv(x_ext, w):
    width = w.shape[0]
    n = x_ext.shape[1] - width + 1
    return sum(x_ext[:, j:j + n] * w[j] for j in range(width))


def _chunk_gated_delta(q, k, v, g, beta, s0):
    b, n, h, dk = k.shape
    dv = v.shape[-1]
    c = min(GDN_CHUNK, n)
    pad = (-n) % c
    if pad:
        padf = lambda a: jnp.pad(a, [(0, 0), (0, pad)] + [(0, 0)] * (a.ndim - 2))
        q, k, v, g, beta = padf(q), padf(k), padf(v), padf(g), padf(beta)
    nc = (n + pad) // c
    qh, kh, vh = [a.transpose(0, 2, 1, 3).reshape(b, h, nc, c, a.shape[-1]) for a in (q, k, v)]
    gh, bh = [a.transpose(0, 2, 1).reshape(b, h, nc, c) for a in (g, beta)]
    gc = jnp.cumsum(gh, -1)
    diff = gc[..., :, None] - gc[..., None, :]
    lower = jnp.tril(jnp.ones((c, c), bool))
    strict = jnp.tril(jnp.ones((c, c), bool), -1)
    decay = jnp.where(lower, jnp.exp(jnp.where(lower, diff, 0.0)), 0.0)
    kb = kh * bh[..., None]
    a_mat = jnp.where(strict, jnp.einsum('bhnid,bhnjd->bhnij', kb, kh) * decay, 0.0)
    rhs = jnp.concatenate([vh * bh[..., None], kb * jnp.exp(gc)[..., None]], -1)
    sol = lax.linalg.triangular_solve(a_mat + jnp.eye(c, dtype=a_mat.dtype), rhs,
                                      left_side=True, lower=True, unit_diagonal=True)
    u, w = sol[..., :dv], sol[..., dv:]
    qk = jnp.where(lower, jnp.einsum('bhnid,bhnjd->bhnij', qh, kh) * decay, 0.0)
    q_dec = qh * jnp.exp(gc)[..., None]
    k_tail = kh * jnp.exp(gc[..., -1:] - gc)[..., None]
    g_tot = jnp.exp(gc[..., -1])

    def step(s, xs_n):
        u_n, w_n, qk_n, qd_n, kt_n, gt_n = xs_n
        v_new = u_n - jnp.einsum('bhcd,bhde->bhce', w_n, s)
        o = jnp.einsum('bhcd,bhde->bhce', qd_n, s) + jnp.einsum('bhij,bhje->bhie', qk_n, v_new)
        s = s * gt_n[..., None, None] + jnp.einsum('bhcd,bhce->bhde', kt_n, v_new)
        return s, o

    xs_all = tuple(jnp.moveaxis(a, 2, 0) for a in (u, w, qk, q_dec, k_tail, g_tot))
    s_fin, o = lax.scan(step, s0, xs_all)
    o = jnp.moveaxis(o, 0, 2).reshape(b, h, nc * c, dv)[:, :, :n].transpose(0, 2, 1, 3)
    return o, s_fin


def _gdn_core(qkv, z, bt, at, conv_buf, s0, conv_w, a_log, dt_bias, norm_w):
    b, n, _ = qkv.shape
    key_dim = GDN_QK_HEADS * GDN_HEAD
    ext = jnp.concatenate([conv_buf.astype(qkv.dtype), qkv], axis=1)
    new_buf = ext[:, -(conv_w.shape[0] - 1):]
    qkv = jax.nn.silu(_causal_dwconv(ext, conv_w))
    q, k, v = jnp.split(qkv, [key_dim, 2 * key_dim], axis=-1)
    rep = GDN_V_HEADS // GDN_QK_HEADS
    q = jnp.repeat(_l2norm(q.reshape(b, n, GDN_QK_HEADS, GDN_HEAD)), rep, axis=2) * (GDN_HEAD ** -0.5)
    k = jnp.repeat(_l2norm(k.reshape(b, n, GDN_QK_HEADS, GDN_HEAD)), rep, axis=2)
    v = v.reshape(b, n, GDN_V_HEADS, GDN_HEAD)
    beta = jax.nn.sigmoid(bt)
    g = -jnp.exp(a_log) * jax.nn.softplus(at + dt_bias)
    o, s_fin = _chunk_gated_delta(q, k, v, g, beta, s0)
    o = o * lax.rsqrt(jnp.mean(o * o, -1, keepdims=True) + RMS_EPS) * norm_w
    o = o * jax.nn.silu(z.reshape(b, n, GDN_V_HEADS, GDN_HEAD))
    return o.reshape(b, n, GDN_V_HEADS * GDN_HEAD), new_buf, s_fin


def _suffix_exclusive(logf):
    return lax.cumsum(logf, axis=1, reverse=True) - logf


def _paged_rows(pool, page_table):
    g = pool[page_table]
    return g.reshape((g.shape[0], g.shape[1] * g.shape[2]) + g.shape[3:])


def _tile_n(n, col0, cap):
    for t in (1024, 512, 256, 128):
        if t <= cap and n % t == 0 and col0 % t == 0:
            return t
    raise ValueError((n, col0))


MM_VMEM_BUDGET_BYTES = 44 * 1024 * 1024


def _mm_tiles(tm, kdim, n, col0, has_res):
    def footprint(tn, tk):
        blocks = 2 * (tm * tk * 2 + tk * tn * 4 + tm * tn * 4 * (2 if has_res else 1))
        temps = tk * tn * 2 + tm * tn * 4 + (tm * tn * 4 if tk < kdim else 0)
        return blocks + temps
    for tk in (kdim, kdim // 2, kdim // 4):
        if tk % 128:
            continue
        for tn in (1024, 512, 256, 128):
            if n % tn == 0 and col0 % tn == 0 and footprint(tn, tk) <= MM_VMEM_BUDGET_BYTES:
                return tn, tk
    raise ValueError((tm, kdim, n, col0))


def kernel(x_prompt, x_sample, cache_fox_k, cache_fox_v, cache_fox_logf, cache_dsa_k, cache_dsa_v, cache_dsa_kidx, state_gdn, state_gdn_conv, state_ffn_conv, page_table, p_prompt, p_sample, gdn_w_in, gdn_conv_w, gdn_a_log, gdn_dt_bias, gdn_norm_w, gdn_w_out, fox_w_in, fox_b_f, fox_w_out, dsa_w_in, dsa_idx_ln_g, dsa_idx_ln_b, dsa_w_out, ffn_w_up, ffn_conv_w, ffn_conv_b, ffn_w_down, ln_mix_g, ln_mix_b, ln_ffn_g, ln_ffn_b, ple_w_proj, ple_w_gate):
    b, n, d = x_prompt.shape
    db, ns, _ = x_sample.shape
    depth = ffn_w_up.shape[0]
    n_past = page_table.shape[1] * PAGE
    alpha = (2.0 * depth) ** 0.25
    f2 = ffn_w_up.shape[2]
    key_dim = GDN_QK_HEADS * GDN_HEAD
    val_dim = GDN_V_HEADS * GDN_HEAD
    conv_dim = 2 * key_dim + val_dim

    groups = {
        "p": dict(nb=b, n=n, tm=1024, xf=x_prompt.reshape(b * n, d)),
        "s": dict(nb=db, n=ns, tm=db * ns, xf=x_sample.reshape(db * ns, d)),
    }
    for gr in groups.values():
        gr["xb"] = gr["xf"].astype(BF16)

    def proj(gr, w, layer, col0, ncols, out_dtype=F32, **kw):
        tn, tk = _mm_tiles(gr["tm"], w.shape[1], ncols, col0, has_res=False)
        return matmul(gr["xb"], w, layer, col0=col0, n=ncols, tm=gr["tm"], tn=tn, tk=tk, out_dtype=out_dtype, **kw)

    def tail_proj(gr, w, layer, col0, padded):
        wt = jnp.pad(w[layer, :, col0:], ((0, 0), (0, padded - (w.shape[2] - col0))))[None]
        return matmul(gr["xb"], wt, 0, tm=gr["tm"], tn=padded, name="mm_tail")

    def out_proj(gr, o_b, w, layer, name="out_proj"):
        tn, tk = _mm_tiles(gr["tm"], o_b.shape[1], d, 0, has_res=True)
        return matmul(o_b, w, layer, tm=gr["tm"], tn=tn, tk=tk, res=gr["xf"], alpha=alpha, name=name)

    outs = {k: [] for k in ("gdn_s_p", "gdn_c_p", "gdn_s_s", "gdn_c_s", "fox_k_p", "fox_v_p", "fox_lf_p",
                            "fox_k_s", "fox_v_s", "fox_lf_s", "dsa_k_p", "dsa_v_p", "dsa_ki_p",
                            "dsa_k_s", "dsa_v_s", "dsa_ki_s", "ffn_c_p", "ffn_c_s")}

    for i in range(depth):
        kind, j = i % 3, i // 3
        for tag, gr in groups.items():
            nb, nn = gr["nb"], gr["n"]
            if kind == 0:
                qkv = proj(gr, gdn_w_in, j, 0, conv_dim).reshape(nb, nn, conv_dim)
                z = proj(gr, gdn_w_in, j, conv_dim, val_dim).reshape(nb, nn, val_dim)
                ba = tail_proj(gr, gdn_w_in, j, conv_dim + val_dim, 128).reshape(nb, nn, 128)
                bt, at = ba[..., :GDN_V_HEADS], ba[..., GDN_V_HEADS:2 * GDN_V_HEADS]
                if tag == "p":
                    conv_buf = jnp.zeros((nb, gdn_conv_w.shape[1] - 1, conv_dim), F32)
                    s0 = jnp.zeros((nb, GDN_V_HEADS, GDN_HEAD, GDN_HEAD), F32)
                    o, s_new = gdn_mixer(qkv, z, ba, conv_buf, s0, gdn_conv_w, gdn_a_log, gdn_dt_bias, gdn_norm_w, j,
                                         blk=512, hg=4, cpi=4)
                    c_new = qkv[:, nn - (gdn_conv_w.shape[1] - 1):]
                else:
                    conv_buf, s0 = state_gdn_conv[j], state_gdn[j]
                    o, c_new, s_new = _gdn_core(qkv, z, bt, at, conv_buf, s0, gdn_conv_w[j], gdn_a_log[j],
                                                gdn_dt_bias[j], gdn_norm_w[j])
                    o = o.astype(BF16)
                outs["gdn_s_" + tag].append(s_new)
                outs["gdn_c_" + tag].append(c_new)
                r = out_proj(gr, o.reshape(nb * nn, val_dim), gdn_w_out, j)
            elif kind == 1:
                k = proj(gr, fox_w_in, j, d, d)
                v = proj(gr, fox_w_in, j, 2 * d, d)
                f = tail_proj(gr, fox_w_in, j, 3 * d, 128)[:, :ATT_HEADS].reshape(nb, nn, ATT_HEADS)
                lf = jax.nn.log_sigmoid(f + fox_b_f[j])
                k4, v4 = k.reshape(nb, nn, ATT_HEADS, HEAD_DIM), v.reshape(nb, nn, ATT_HEADS, HEAD_DIM)
                outs["fox_k_" + tag].append(k4)
                outs["fox_v_" + tag].append(v4)
                outs["fox_lf_" + tag].append(lf)
                if tag == "p":
                    q = proj(gr, fox_w_in, j, 0, d, out_dtype=BF16)
                    o = flash_attention(q.reshape(nb, nn, d), k.reshape(nb, nn, d), v.reshape(nb, nn, d),
                                        r=_suffix_exclusive(lf), tq=1024, tk=1024, name="fox_flash")
                    o = o.reshape(nb * nn, d)
                else:
                    q = proj(gr, fox_w_in, j, 0, d).reshape(nb, nn, ATT_HEADS, HEAD_DIM)
                    lf_all = jnp.concatenate([_paged_rows(cache_fox_logf[j], page_table), lf], axis=1)
                    rr = _suffix_exclusive(lf_all)
                    o = paged_attention(q, cache_fox_k, cache_fox_v, j, page_table, k4, v4, pg=8,
                                        r_q=rr[:, n_past:], r_k=rr, name="fox_decode")
                    o = o.reshape(nb * nn, d).astype(BF16)
                r = out_proj(gr, o, fox_w_out, j)
            else:
                k = proj(gr, dsa_w_in, j, d, d)
                v = proj(gr, dsa_w_in, j, 2 * d, d)
                tail = tail_proj(gr, dsa_w_in, j, 3 * d + IDX_HEADS * IDX_DIM, 256)
                ki, _ = layer_norm(tail[:, :IDX_DIM], dsa_idx_ln_g, dsa_idx_ln_b, j, tm=min(gr["tm"], 512),
                                   name="dsa_ki_ln")
                wh = tail[:, IDX_DIM:IDX_DIM + IDX_HEADS] * (IDX_HEADS ** -0.5)
                k4, v4 = k.reshape(nb, nn, ATT_HEADS, HEAD_DIM), v.reshape(nb, nn, ATT_HEADS, HEAD_DIM)
                ki3 = ki.reshape(nb, nn, IDX_DIM)
                outs["dsa_k_" + tag].append(k4)
                outs["dsa_v_" + tag].append(v4)
                outs["dsa_ki_" + tag].append(ki3)
                if tag == "p":
                    q = proj(gr, dsa_w_in, j, 0, d, out_dtype=BF16)
                    qidx = proj(gr, dsa_w_in, j, 3 * d, IDX_HEADS * IDX_DIM, out_dtype=BF16)
                    mask = dsa_select(qidx.reshape(nb, nn, IDX_HEADS * IDX_DIM), ki3, wh.reshape(nb, nn, IDX_HEADS),
                                      tq=256, tk=512, topk=min(TOPK_MAX, nn // 4))
                    o = flash_attention(q.reshape(nb, nn, d), k.reshape(nb, nn, d), v.reshape(nb, nn, d),
                                        mask=mask, tq=1024, tk=1024, name="dsa_flash")
                    o = o.reshape(nb * nn, d)
                else:
                    q = proj(gr, dsa_w_in, j, 0, d).reshape(nb, nn, ATT_HEADS, HEAD_DIM)
                    qidx = proj(gr, dsa_w_in, j, 3 * d, IDX_HEADS * IDX_DIM).reshape(nb, nn, IDX_HEADS, IDX_DIM)
                    sel = paged_select(qidx, wh.reshape(nb, nn, IDX_HEADS), cache_dsa_kidx, j, page_table, ki3,
                                       topk=min(TOPK_MAX, (n_past + nn) // 4), pg=16, name="dsa_decode_select")
                    o = paged_attention(q, cache_dsa_k, cache_dsa_v, j, page_table, k4, v4, pg=8, mask=sel,
                                        name="dsa_decode")
                    o = o.reshape(nb * nn, d).astype(BF16)
                r = out_proj(gr, o, dsa_w_out, j)

            ln_tm = min(gr["tm"], 512)
            gr["xf"], gr["xb"] = layer_norm(r, ln_mix_g, ln_mix_b, i, tm=ln_tm, name="ln_mix")

            if tag == "p":
                act, c_new = ffn_up(gr["xb"], jnp.zeros((nb, 2, f2), F32), ffn_w_up, ffn_conv_w, ffn_conv_b, i,
                                    seq_len=nn, tm=gr["tm"], tn=512)
            else:
                hcur = proj(gr, ffn_w_up, i, 0, f2).reshape(nb, nn, f2)
                ext = jnp.concatenate([state_ffn_conv[i], hcur], axis=1)
                c_new = ext[:, -(ffn_conv_w.shape[1] - 1):]
                hc = _causal_dwconv(ext, ffn_conv_w[i]) + ffn_conv_b[i]
                gate, val = jnp.split(hc, [f2 // 2], axis=-1)
                act = (jax.nn.silu(gate) * val).reshape(nb * nn, f2 // 2).astype(BF16)
            outs["ffn_c_" + tag].append(c_new)
            r = out_proj(gr, act, ffn_w_down, i, name="ffn_down")
            x2f, x2b = layer_norm(r, ln_ffn_g, ln_ffn_b, i, tm=ln_tm, name="ln_ffn")
            p_in = (p_prompt if tag == "p" else p_sample)[i].reshape(nb * nn, -1).astype(BF16)
            gr["xf"], gr["xb"] = per_layer_embed(x2f, x2b, p_in, ple_w_proj, ple_w_gate, i, tm=gr["tm"],
                                                 tn=_tile_n(d, 0, 512))

    st = lambda key: jnp.stack(outs[key])
    return (groups["p"]["xf"].reshape(b, n, d), groups["s"]["xf"].reshape(db, ns, d),
            st("gdn_s_p"), st("gdn_c_p"), st("fox_k_p"), st("fox_v_p"), st("fox_lf_p"),
            st("dsa_k_p"), st("dsa_v_p"), st("dsa_ki_p"), st("ffn_c_p"),
            st("gdn_s_s"), st("gdn_c_s"), st("fox_k_s"), st("fox_v_s"), st("fox_lf_s"),
            st("dsa_k_s"), st("dsa_v_s"), st("dsa_ki_s"), st("ffn_c_s"))
```
